```python
import jax, jax.numpy as jnp
from jax import lax
import numpy as np

D_MODEL = 1024
BATCH = 8
SEQ = 2048
DEPTH = 2
DEC_BATCH = 128
DEC_SEQ = 4
PAST_LEN = 16384
PAGE_SIZE = 128

CONV_CH = D_MODEL // 2
CONV_K = 31
RET_HEADS = 4
RET_DK = (D_MODEL // 2) // RET_HEADS
RET_DV = RET_DK
RET_QK = RET_HEADS * RET_DK
RET_WIDTH = RET_HEADS * RET_DV
MIX_WIDTH = CONV_CH + RET_WIDTH
MIX_IN = 2 * CONV_CH + 2 * RET_QK + 2 * RET_WIDTH
RET_CHUNK = 128
ROPE_BASE = 10000.0
D_FF = 4 * D_MODEL
N_MEM = 256
CA_HEADS = 4
CA_HEAD_DIM = D_MODEL // CA_HEADS
EPS = 1e-6
GN_EPS = 1e-5

kernel_name = "hymba_conformer_retnet_decoder_step"

F32 = jnp.float32


def rmsnorm(x, g):
    xf = x.astype(F32)
    y = xf * lax.rsqrt(jnp.mean(xf * xf, axis=-1, keepdims=True) + EPS) * g.astype(F32)
    return y.astype(x.dtype)


def swiglu(h, w1, w3, w2):
    return (jax.nn.silu(h @ w1) * (h @ w3)) @ w2


def rotary(t, pos):
    d = t.shape[-1]
    inv_freq = ROPE_BASE ** (-jnp.arange(0, d, 2, dtype=F32) / d)
    ang = pos[:, None] * inv_freq[None, :]
    cos = jnp.cos(ang)[None, :, None, :]
    sin = jnp.sin(ang)[None, :, None, :]
    t1, t2 = t[..., : d // 2], t[..., d // 2:]
    return jnp.concatenate([t1 * cos - t2 * sin, t1 * sin + t2 * cos], axis=-1)


def conv_mixer(a, gate, buf, conv_w, conv_b, ln_g, ln_b):
    u = (a.astype(F32) * jax.nn.sigmoid(gate.astype(F32)))
    full = jnp.concatenate([buf.astype(F32), u], axis=1)
    new_buf = full[:, -(CONV_K - 1):, :]
    y = lax.conv_general_dilated(
        full, conv_w.astype(F32)[:, None, :], window_strides=(1,), padding='VALID',
        dimension_numbers=('NWC', 'WIO', 'NWC'), feature_group_count=CONV_CH)
    y = y + conv_b.astype(F32)
    mu = jnp.mean(y, axis=-1, keepdims=True)
    var = jnp.mean(jnp.square(y - mu), axis=-1, keepdims=True)
    y = (y - mu) * lax.rsqrt(var + EPS) * ln_g.astype(F32) + ln_b.astype(F32)
    return jax.nn.silu(y).astype(a.dtype), new_buf.astype(buf.dtype)


def retention_chunkwise(q, k, v, s0, log_gamma):
    b, l, h, _ = q.shape
    c = RET_CHUNK if l % RET_CHUNK == 0 else l
    n = l // c
    idx = jnp.arange(c, dtype=F32)
    rel = idx[:, None] - idx[None, :]
    decay = jnp.where(rel[None] >= 0,
                      jnp.exp(log_gamma[:, None, None] * jnp.maximum(rel, 0.0)[None]), 0.0)
    q_dec = jnp.exp(log_gamma[:, None] * (idx[None, :] + 1.0))
    k_dec = jnp.exp(log_gamma[:, None] * (c - 1.0 - idx[None, :]))
    chunk_dec = jnp.exp(log_gamma * c)

    def to_chunks(t):
        return t.reshape(b, n, c, h, t.shape[-1]).transpose(1, 0, 3, 2, 4)

    def step(s, inp):
        qc, kc, vc = inp
        scores = jnp.einsum('bhid,bhjd->bhij', qc, kc) * decay[None]
        o = (jnp.einsum('bhij,bhjv->bhiv', scores, vc)
             + jnp.einsum('bhid,bhdv->bhiv', qc, s) * q_dec[None, :, :, None])
        s = (s * chunk_dec[None, :, None, None]
             + jnp.einsum('bhjd,bhjv->bhdv', kc * k_dec[None, :, :, None], vc))
        return s, o

    s, o = lax.scan(step, s0, (to_chunks(q), to_chunks(k), to_chunks(v)))
    o = o.transpose(1, 0, 3, 2, 4).reshape(b, l, h, v.shape[-1])
    return o, s


def retention_mixer(rq, rk, rv, rg, pos, s0, gn_g):
    b, l, _ = rq.shape
    log_gamma = jnp.log1p(-jnp.exp2(-5.0 - jnp.arange(RET_HEADS, dtype=F32)))
    q = rotary(rq.astype(F32).reshape(b, l, RET_HEADS, RET_DK), pos)
    k = rotary(rk.astype(F32).reshape(b, l, RET_HEADS, RET_DK), pos) * (RET_DK ** -0.5)
    v = rv.astype(F32).reshape(b, l, RET_HEADS, RET_DV)
    o, s = retention_chunkwise(q, k, v, s0.astype(F32), log_gamma)
    mu = jnp.mean(o, axis=-1, keepdims=True)
    var = jnp.mean(jnp.square(o - mu), axis=-1, keepdims=True)
    o = (o - mu) * lax.rsqrt(var + GN_EPS) * gn_g.astype(F32)[None, None]
    out = jax.nn.silu(rg.astype(F32)) * o.reshape(b, l, RET_WIDTH)
    return out.astype(rq.dtype), s.astype(s0.dtype)


def memory_kv(mem, g_mem, w_k, w_v):
    b = mem.shape[0]
    mn = rmsnorm(mem, g_mem)
    mk = (mn @ w_k).reshape(b, N_MEM, CA_HEADS, CA_HEAD_DIM)
    mv = (mn @ w_v).reshape(b, N_MEM, CA_HEADS, CA_HEAD_DIM)
    return mk, mv


def cross_attn(h, mem_k, mem_v, w_q, w_o):
    b, l, _ = h.shape
    q = (h @ w_q).reshape(b, l, CA_HEADS, CA_HEAD_DIM).astype(F32)
    s = jnp.einsum('blhd,bmhd->bhlm', q, mem_k.astype(F32)) * (CA_HEAD_DIM ** -0.5)
    p = jax.nn.softmax(s, axis=-1)
    o = jnp.einsum('bhlm,bmhd->blhd', p, mem_v.astype(F32)).reshape(b, l, D_MODEL)
    return o.astype(h.dtype) @ w_o


def trunk_layer(x, pos, conv_buf, ret_s, mem_k, mem_v,
                g_ffn1, w1a, w3a, w2a, g_mix, w_in, conv_w, conv_b, conv_ln_g, conv_ln_b,
                ret_gn_g, w_out, g_ca, w_cq, w_co, g_ffn2, w1b, w3b, w2b):
    x = x + 0.5 * swiglu(rmsnorm(x, g_ffn1), w1a, w3a, w2a)
    h = rmsnorm(x, g_mix)
    proj = h @ w_in
    cuts = [CONV_CH, 2 * CONV_CH, 2 * CONV_CH + RET_QK, 2 * CONV_CH + 2 * RET_QK,
            2 * CONV_CH + 2 * RET_QK + RET_WIDTH]
    c_a, c_g, r_q, r_k, r_v, r_g = jnp.split(proj, cuts, axis=-1)
    conv_out, new_buf = conv_mixer(c_a, c_g, conv_buf, conv_w, conv_b, conv_ln_g, conv_ln_b)
    ret_out, new_s = retention_mixer(r_q, r_k, r_v, r_g, pos, ret_s, ret_gn_g)
    x = x + jnp.concatenate([conv_out, ret_out], axis=-1) @ w_out
    x = x + cross_attn(rmsnorm(x, g_ca), mem_k, mem_v, w_cq, w_co)
    x = x + 0.5 * swiglu(rmsnorm(x, g_ffn2), w1b, w3b, w2b)
    return x, new_buf, new_s


def setup_inputs(seed: int = 0) -> dict:
    key = jax.random.key(seed)
    ks = iter(jax.random.split(key, 40))

    def nrm(shape, scale):
        return jax.random.normal(next(ks), shape, F32) * scale

    def gain(shape):
        return 1.0 + 0.05 * jax.random.normal(next(ks), shape, F32)

    return {
        "x_prompt": nrm((BATCH, SEQ, D_MODEL), 1.0),
        "x_sample": nrm((DEC_BATCH, DEC_SEQ, D_MODEL), 1.0),
        "state_conv": nrm((DEPTH, DEC_BATCH, CONV_K - 1, CONV_CH), 0.5),
        "state_ret": nrm((DEPTH, DEC_BATCH, RET_HEADS, RET_DK, RET_DV), 0.3),
        "cache_mem_k": nrm((DEPTH, DEC_BATCH, N_MEM, CA_HEADS, CA_HEAD_DIM), 1.0),
        "cache_mem_v": nrm((DEPTH, DEC_BATCH, N_MEM, CA_HEADS, CA_HEAD_DIM), 1.0),
        "mem_prompt": nrm((BATCH, N_MEM, D_MODEL), 1.0),
        "g_ffn1": gain((DEPTH, D_MODEL)),
        "w1_ffn1": nrm((DEPTH, D_MODEL, D_FF), D_MODEL ** -0.5),
        "w3_ffn1": nrm((DEPTH, D_MODEL, D_FF), D_MODEL ** -0.5),
        "w2_ffn1": nrm((DEPTH, D_FF, D_MODEL), D_FF ** -0.5),
        "g_mix": gain((DEPTH, D_MODEL)),
        "w_in": nrm((DEPTH, D_MODEL, MIX_IN), D_MODEL ** -0.5),
        "conv_w": nrm((DEPTH, CONV_K, CONV_CH), CONV_K ** -0.5),
        "conv_b": nrm((DEPTH, CONV_CH), 0.02),
        "conv_ln_g": gain((DEPTH, CONV_CH)),
        "conv_ln_b": nrm((DEPTH, CONV_CH), 0.02),
        "ret_gn_g": gain((DEPTH, RET_HEADS, RET_DV)),
        "w_out": nrm((DEPTH, MIX_WIDTH, D_MODEL), MIX_WIDTH ** -0.5),
        "g_ca": gain((DEPTH, D_MODEL)),
        "g_mem": gain((DEPTH, D_MODEL)),
        "w_cq": nrm((DEPTH, D_MODEL, D_MODEL), D_MODEL ** -0.5),
        "w_ck": nrm((DEPTH, D_MODEL, D_MODEL), D_MODEL ** -0.5),
        "w_cv": nrm((DEPTH, D_MODEL, D_MODEL), D_MODEL ** -0.5),
        "w_co": nrm((DEPTH, D_MODEL, D_MODEL), D_MODEL ** -0.5),
        "g_ffn2": gain((DEPTH, D_MODEL)),
        "w1_ffn2": nrm((DEPTH, D_MODEL, D_FF), D_MODEL ** -0.5),
        "w3_ffn2": nrm((DEPTH, D_MODEL, D_FF), D_MODEL ** -0.5),
        "w2_ffn2": nrm((DEPTH, D_FF, D_MODEL), D_FF ** -0.5),
        "g_final": gain((D_MODEL,)),
    }


def reference(x_prompt, x_sample, state_conv, state_ret, cache_mem_k, cache_mem_v, mem_prompt,
              g_ffn1, w1_ffn1, w3_ffn1, w2_ffn1, g_mix, w_in, conv_w, conv_b, conv_ln_g, conv_ln_b,
              ret_gn_g, w_out, g_ca, g_mem, w_cq, w_ck, w_cv, w_co,
              g_ffn2, w1_ffn2, w3_ffn2, w2_ffn2, g_final):
    b_p, l_p, _ = x_prompt.shape
    l_s = x_sample.shape[1]
    pos_p = jnp.arange(l_p, dtype=F32)
    pos_s = PAST_LEN + jnp.arange(l_s, dtype=F32)
    buf_p0 = jnp.zeros((b_p, CONV_K - 1, CONV_CH), x_prompt.dtype)
    ret_p0 = jnp.zeros((b_p, RET_HEADS, RET_DK, RET_DV), F32)

    xp, xs = x_prompt, x_sample
    conv_p, ret_p, memk_p, memv_p, conv_s, ret_s = [], [], [], [], [], []
    for l in range(DEPTH):
        lw = (g_ffn1[l], w1_ffn1[l], w3_ffn1[l], w2_ffn1[l], g_mix[l], w_in[l], conv_w[l], conv_b[l],
              conv_ln_g[l], conv_ln_b[l], ret_gn_g[l], w_out[l], g_ca[l], w_cq[l], w_co[l],
              g_ffn2[l], w1_ffn2[l], w3_ffn2[l], w2_ffn2[l])
        mk_p, mv_p = memory_kv(mem_prompt, g_mem[l], w_ck[l], w_cv[l])
        xp, bp, sp = trunk_layer(xp, pos_p, buf_p0, ret_p0, mk_p, mv_p, *lw)
        xs, bs, ss = trunk_layer(xs, pos_s, state_conv[l], state_ret[l],
                                 cache_mem_k[l], cache_mem_v[l], *lw)
        conv_p.append(bp); ret_p.append(sp); memk_p.append(mk_p); memv_p.append(mv_p)
        conv_s.append(bs); ret_s.append(ss)

    y_prompt = rmsnorm(xp, g_final)
    y_sample = rmsnorm(xs, g_final)
    new_conv_p = jnp.stack(conv_p)
    new_ret_p = jnp.stack(ret_p)
    new_memk_p = jnp.stack(memk_p)
    new_memv_p = jnp.stack(memv_p)
    new_conv_s = jnp.stack(conv_s)
    new_ret_s = jnp.stack(ret_s)
    return (y_prompt, y_sample, new_conv_p, new_ret_p, new_memk_p, new_memv_p, new_conv_s, new_ret_s)
```

```python
import functools

import jax
import jax.numpy as jnp
from jax import lax
from jax.experimental import pallas as pl
from jax.experimental.pallas import tpu as pltpu

F32 = jnp.float32
BF16 = jnp.bfloat16

D_MODEL = 1024
BATCH = 8
SEQ = 2048
DEPTH = 2
DEC_BATCH = 128
DEC_SEQ = 4
PAST_LEN = 16384
CONV_CH = 512
CONV_K = 31
RET_HEADS = 4
RET_DK = 128
RET_DV = 128
RET_WIDTH = 512
MIX_IN = 3072
RET_CHUNK = 128
ROPE_BASE = 10000.0
D_FF = 4096
N_MEM = 256
CA_HEADS = 4
CA_HEAD_DIM = 256
EPS = 1e-6
GN_EPS = 1e-5

T_P = BATCH * SEQ
T_S = DEC_BATCH * DEC_SEQ

TM = 256
TF = 1024
CONV_TT = 256
CONV_RC = 32
CONV_HALO = 32
S_PAD = 16
RET_S_BT = 8
ATT_S_BT = 4
CONV_S_BT = 16
VMEM_LIMIT = 56 * 1024 * 1024


def _dot(a, b):
    return jnp.dot(a, b, preferred_element_type=F32)


def _dot_nt(a, b):
    return lax.dot_general(a, b, (((1,), (1,)), ((), ())), preferred_element_type=F32)


def _rms(x, g):
    return x * lax.rsqrt(jnp.mean(x * x, axis=-1, keepdims=True) + EPS) * g


def _silu(x):
    return x * jax.nn.sigmoid(x)


def _ffn(x, g_ref, w1_ref, w3_ref, w2_ref):
    xn = _rms(x, g_ref[...]).astype(BF16)
    acc = None
    for c in range(D_FF // TF):
        sl = slice(c * TF, (c + 1) * TF)
        h1 = _dot(xn, w1_ref[:, sl])
        h3 = _dot(xn, w3_ref[:, sl])
        a = (_silu(h1) * h3).astype(BF16)
        part = _dot(a, w2_ref[sl, :])
        acc = part if acc is None else acc + part
    return x + 0.5 * acc


def _softmax_rows(s):
    m = jnp.max(s, axis=-1, keepdims=True)
    e = jnp.exp(s - m)
    return e * (1.0 / jnp.sum(e, axis=-1, keepdims=True))


def _group_norm_gate(o, gate, gn):
    mu = jnp.mean(o, axis=-1, keepdims=True)
    d = o - mu
    var = jnp.mean(d * d, axis=-1, keepdims=True)
    return _silu(gate) * (d * lax.rsqrt(var + GN_EPS) * gn)


def _rotary(t, cos2, sin2):
    return t * cos2 + pltpu.roll(t, RET_DK // 2, 1) * sin2


def _ffn_inproj_kernel(x_ref, g1_ref, w1_ref, w3_ref, w2_ref, gm_ref, win_ref,
                       x1_ref, u_ref, r_ref):
    x1 = _ffn(x_ref[...], g1_ref, w1_ref, w3_ref, w2_ref)
    x1_ref[...] = x1
    h = _rms(x1, gm_ref[...]).astype(BF16)
    proj = _dot(h, win_ref[...])
    u_ref[...] = proj[:, :CONV_CH] * jax.nn.sigmoid(proj[:, CONV_CH:2 * CONV_CH])
    r_ref[...] = proj[:, 2 * CONV_CH:]


def _resident(shape, layer):
    nd = len(shape)
    return pl.BlockSpec((None,) + tuple(shape), lambda *_: (layer,) + (0,) * nd,
                        pipeline_mode=pl.Buffered(1))


def _ffn_inproj(x, layer, g1, w1, w3, w2, gm, win):
    t = x.shape[0]
    row = lambda n: pl.BlockSpec((TM, n), lambda i: (i, 0))
    return pl.pallas_call(
        _ffn_inproj_kernel,
        grid=(t // TM,),
        in_specs=[row(D_MODEL), _resident((1, D_MODEL), layer),
                  _resident((D_MODEL, D_FF), layer), _resident((D_MODEL, D_FF), layer),
                  _resident((D_FF, D_MODEL), layer), _resident((1, D_MODEL), layer),
                  _resident((D_MODEL, MIX_IN), layer)],
        out_specs=[row(D_MODEL), row(CONV_CH), row(MIX_IN - 2 * CONV_CH)],
        out_shape=[jax.ShapeDtypeStruct((t, D_MODEL), F32),
                   jax.ShapeDtypeStruct((t, CONV_CH), F32),
                   jax.ShapeDtypeStruct((t, MIX_IN - 2 * CONV_CH), F32)],
        compiler_params=pltpu.CompilerParams(vmem_limit_bytes=VMEM_LIMIT),
        name="ffn_inproj",
    )(x, g1, w1, w3, w2, gm, win)


def _layer_norm_silu(y, lg, lb):
    mu = jnp.mean(y, axis=-1, keepdims=True)
    d = y - mu
    var = jnp.mean(d * d, axis=-1, keepdims=True)
    return _silu(d * lax.rsqrt(var + EPS) * lg + lb)


def _conv_p_kernel(u_ref, w_ref, b_ref, lg_ref, lb_ref, y_ref, nb_ref, win_ref):
    j = pl.program_id(1)

    @pl.when(j == 0)
    def _():
        win_ref[0:CONV_HALO, :] = jnp.zeros((CONV_HALO, CONV_CH), F32)

    @pl.when(j > 0)
    def _():
        win_ref[0:CONV_HALO, :] = win_ref[CONV_TT:CONV_TT + CONV_HALO, :]

    win_ref[CONV_HALO:CONV_HALO + CONV_TT, :] = u_ref[...]
    first = CONV_HALO - (CONV_K - 1)
    for c in range(CONV_TT // CONV_RC):
        base = first + c * CONV_RC
        acc = w_ref[0:1, :] * win_ref[base:base + CONV_RC, :]
        for k in range(1, CONV_K):
            acc = acc + w_ref[k:k + 1, :] * win_ref[base + k:base + k + CONV_RC, :]
        y = acc + b_ref[...]
        y_ref[c * CONV_RC:(c + 1) * CONV_RC, :] = _layer_norm_silu(y, lg_ref[...], lb_ref[...])

    @pl.when(j == pl.num_programs(1) - 1)
    def _():
        nb_ref[0] = u_ref[CONV_TT - (CONV_K - 1):CONV_TT, :]


def _conv_p(u, layer, conv_w, conv_b, ln_g, ln_b):
    nt = SEQ // CONV_TT
    vec = lambda: pl.BlockSpec((None, 1, CONV_CH), lambda b, j: (layer, 0, 0))
    return pl.pallas_call(
        _conv_p_kernel,
        grid=(BATCH, nt),
        in_specs=[pl.BlockSpec((CONV_TT, CONV_CH), lambda b, j: (b * nt + j, 0)),
                  pl.BlockSpec((None, CONV_K, CONV_CH), lambda b, j: (layer, 0, 0)),
                  vec(), vec(), vec()],
        out_specs=[pl.BlockSpec((CONV_TT, CONV_CH), lambda b, j: (b * nt + j, 0)),
                   pl.BlockSpec((1, CONV_K - 1, CONV_CH), lambda b, j: (b, 0, 0))],
        out_shape=[jax.ShapeDtypeStruct((T_P, CONV_CH), F32),
                   jax.ShapeDtypeStruct((BATCH, CONV_K - 1, CONV_CH), F32)],
        scratch_shapes=[pltpu.VMEM((CONV_HALO + CONV_TT, CONV_CH), F32)],
        name="conv_prompt",
    )(u, conv_w, conv_b, ln_g, ln_b)


def _conv_s_kernel(st_ref, u_ref, ut_ref, w_ref, wb_ref, b_ref, lg_ref, lb_ref, y_ref, nb_ref):
    buf = st_ref[0]
    for t in range(DEC_SEQ):
        y = jnp.sum(buf * wb_ref[t][None], axis=1)
        for j in range(t + 1):
            k = CONV_K - 1 - t + j
            y = y + ut_ref[j] * w_ref[k:k + 1, :]
        y = y + b_ref[...]
        y_ref[t] = _layer_norm_silu(y, lg_ref[...], lb_ref[...])
    keep = CONV_K - 1 - DEC_SEQ
    nb_ref[:, 0:keep, :] = buf[:, DEC_SEQ:, :]
    nb_ref[:, keep:, :] = u_ref[...]


def _conv_s(state_conv, u3, ut, layer, conv_w, wb, conv_b, ln_g, ln_b):
    bt = CONV_S_BT
    vec = lambda: pl.BlockSpec((None, 1, CONV_CH), lambda i: (layer, 0, 0))
    return pl.pallas_call(
        _conv_s_kernel,
        grid=(DEC_BATCH // bt,),
        in_specs=[pl.BlockSpec((1, bt, CONV_K - 1, CONV_CH), lambda i: (layer, i, 0, 0)),
                  pl.BlockSpec((bt, DEC_SEQ, CONV_CH), lambda i: (i, 0, 0)),
                  pl.BlockSpec((DEC_SEQ, bt, CONV_CH), lambda i: (0, i, 0)),
                  pl.BlockSpec((None, CONV_K, CONV_CH), lambda i: (layer, 0, 0)),
                  pl.BlockSpec((DEC_SEQ, CONV_K - 1, CONV_CH), lambda i: (0, 0, 0)),
                  vec(), vec(), vec()],
        out_specs=[pl.BlockSpec((DEC_SEQ, bt, CONV_CH), lambda i: (0, i, 0)),
                   pl.BlockSpec((bt, CONV_K - 1, CONV_CH), lambda i: (i, 0, 0))],
        out_shape=[jax.ShapeDtypeStruct((DEC_SEQ, DEC_BATCH, CONV_CH), F32),
                   jax.ShapeDtypeStruct((DEC_BATCH, CONV_K - 1, CONV_CH), F32)],
        name="conv_sample",
    )(state_conv, u3, ut, conv_w, wb, conv_b, ln_g, ln_b)


def _ret_p_kernel(q_ref, k_ref, v_ref, g_ref, cos_ref, sin_ref, dec_ref, qd_ref, kd_ref,
                  cd_ref, gn_ref, o_ref, ns_ref, s_ref):
    j = pl.program_id(1)

    @pl.when(j == 0)
    def _():
        s_ref[...] = jnp.zeros_like(s_ref)

    cos2 = cos_ref[...]
    sin2 = sin_ref[...]
    for h in range(RET_HEADS):
        sl = slice(h * RET_DK, (h + 1) * RET_DK)
        qh = _rotary(q_ref[:, sl], cos2, sin2)
        kh = _rotary(k_ref[:, sl], cos2, sin2) * (RET_DK ** -0.5)
        qb = qh.astype(BF16)
        vb = v_ref[:, sl].astype(BF16)
        s_h = s_ref[h]
        sc = _dot_nt(qb, kh.astype(BF16)) * dec_ref[h]
        o = _dot(sc.astype(BF16), vb) + _dot(qb, s_h.astype(BF16)) * qd_ref[h]
        kdec = (kh * kd_ref[h]).astype(BF16)
        upd = lax.dot_general(kdec, vb, (((0,), (0,)), ((), ())), preferred_element_type=F32)
        s_ref[h] = s_h * cd_ref[h] + upd
        o_ref[:, sl] = _group_norm_gate(o, g_ref[:, sl], gn_ref[:, sl])

    @pl.when(j == pl.num_programs(1) - 1)
    def _():
        ns_ref[0] = s_ref[...]


def _ret_p(r, layer, tabs, gn):
    cos2, sin2, dec, qd, kd, cd = tabs
    nc = SEQ // RET_CHUNK
    col = lambda c: pl.BlockSpec((RET_CHUNK, RET_WIDTH), lambda b, j: (b * nc + j, c))
    const = lambda: pl.BlockSpec((RET_HEADS, RET_CHUNK, RET_DK), lambda b, j: (0, 0, 0))
    tab = lambda: pl.BlockSpec((RET_CHUNK, RET_DK), lambda b, j: (j, 0))
    return pl.pallas_call(
        _ret_p_kernel,
        grid=(BATCH, nc),
        in_specs=[col(0), col(1), col(2), col(3), tab(), tab(), const(), const(), const(), const(),
                  pl.BlockSpec((None, 1, RET_WIDTH), lambda b, j: (layer, 0, 0))],
        out_specs=[pl.BlockSpec((RET_CHUNK, RET_WIDTH), lambda b, j: (b * nc + j, 0)),
                   pl.BlockSpec((1, RET_HEADS, RET_DK, RET_DV), lambda b, j: (b, 0, 0, 0))],
        out_shape=[jax.ShapeDtypeStruct((T_P, RET_WIDTH), F32),
                   jax.ShapeDtypeStruct((BATCH, RET_HEADS, RET_DK, RET_DV), F32)],
        scratch_shapes=[pltpu.VMEM((RET_HEADS, RET_DK, RET_DV), F32)],
        name="retention_prompt",
    )(r, r, r, r, cos2, sin2, dec, qd, kd, cd, gn)


def _ret_s_kernel(q_ref, k_ref, v_ref, g_ref, st_ref, cos_ref, sin_ref, dec_ref, qd_ref, kd_ref,
                  cd_ref, gn_ref, o_ref, ns_ref):
    cos2 = cos_ref[...]
    sin2 = sin_ref[...]
    zpad = jnp.zeros((RET_DK - S_PAD, RET_DK), F32)

    def body(b, carry):
        for h in range(RET_HEADS):
            sl = slice(h * RET_DK, (h + 1) * RET_DK)
            qh = _rotary(q_ref[b, :, sl], cos2, sin2)
            kh = _rotary(k_ref[b, :, sl], cos2, sin2) * (RET_DK ** -0.5)
            vh = v_ref[b, :, sl]
            s_h = st_ref[0, b, h]
            o = _dot(qh.astype(BF16), s_h.astype(BF16)) * qd_ref[h]
            for jj in range(DEC_SEQ):
                sj = jnp.sum(qh * kh[jj:jj + 1, :], axis=-1, keepdims=True)
                o = o + (sj * dec_ref[h, jj]) * vh[jj:jj + 1, :]
            kdec = jnp.concatenate([kh * kd_ref[h], zpad], axis=0)
            vpad = jnp.concatenate([vh, zpad], axis=0)
            upd = _dot(kdec.T.astype(BF16), vpad.astype(BF16))
            ns_ref[0, b, h] = s_h * cd_ref[h] + upd
            o_ref[b, :, sl] = _group_norm_gate(o, g_ref[b, :, sl], gn_ref[:, sl])
        return carry

    lax.fori_loop(0, RET_S_BT, body, 0)


def _ret_s(r16, state_ret, layer, tabs, gn):
    cos2, sin2, dec, qd, kd, cd = tabs
    bt = RET_S_BT
    col = lambda c: pl.BlockSpec((bt, S_PAD, RET_WIDTH), lambda i: (i, 0, c))
    full = lambda a: pl.BlockSpec(a.shape, lambda i: (0,) * a.ndim)
    st = pl.BlockSpec((1, bt, RET_HEADS, RET_DK, RET_DV), lambda i: (layer, i, 0, 0, 0))
    return pl.pallas_call(
        _ret_s_kernel,
        grid=(DEC_BATCH // bt,),
        in_specs=[col(0), col(1), col(2), col(3), st, full(cos2), full(sin2), full(dec), full(qd),
                  full(kd), full(cd), pl.BlockSpec((None, 1, RET_WIDTH), lambda i: (layer, 0, 0))],
        out_specs=[pl.BlockSpec((bt, S_PAD, RET_WIDTH), lambda i: (i, 0, 0)),
                   pl.BlockSpec((1, bt, RET_HEADS, RET_DK, RET_DV), lambda i: (0, i, 0, 0, 0))],
        out_shape=[jax.ShapeDtypeStruct((DEC_BATCH, S_PAD, RET_WIDTH), F32),
                   jax.ShapeDtypeStruct((1, DEC_BATCH, RET_HEADS, RET_DK, RET_DV), F32)],
        name="retention_sample",
    )(r16, r16, r16, r16, state_ret, cos2, sin2, dec, qd, kd, cd, gn)


def _memkv_kernel(m_ref, g_ref, wk_ref, wv_ref, k_ref, v_ref):
    mn = _rms(m_ref[...], g_ref[...]).astype(BF16)
    k_ref[...] = _dot(mn, wk_ref[...])
    v_ref[...] = _dot(mn, wv_ref[...])


def _memkv(mem, layer, g, wk, wv):
    t = mem.shape[0]
    row = lambda: pl.BlockSpec((TM, D_MODEL), lambda i: (i, 0))
    return pl.pallas_call(
        _memkv_kernel,
        grid=(t // TM,),
        in_specs=[row(), _resident((1, D_MODEL), layer), _resident((D_MODEL, D_MODEL), layer),
                  _resident((D_MODEL, D_MODEL), layer)],
        out_specs=[row(), row()],
        out_shape=[jax.ShapeDtypeStruct((t, D_MODEL), F32)] * 2,
        name="memory_kv",
    )(mem, g, wk, wv)


def _mix_out_q(x1, conv, ret, wo_ref, gca_ref, wq_ref):
    mix = jnp.concatenate([conv, ret], axis=-1).astype(BF16)
    x2 = x1 + _dot(mix, wo_ref[...])
    q = _dot(_rms(x2, gca_ref[...]).astype(BF16), wq_ref[...])
    return x2, q


def _post_p_kernel(x1_ref, conv_ref, ret_ref, mk_ref, mv_ref, wo_ref, gca_ref, wq_ref, wco_ref,
                   g2_ref, w1_ref, w3_ref, w2_ref, gf_ref, y_ref, *, final):
    x2, q = _mix_out_q(x1_ref[...], conv_ref[...], ret_ref[...], wo_ref, gca_ref, wq_ref)
    heads = []
    for h in range(CA_HEADS):
        sl = slice(h * CA_HEAD_DIM, (h + 1) * CA_HEAD_DIM)
        s = _dot_nt(q[:, sl].astype(BF16), mk_ref[:, sl].astype(BF16)) * (CA_HEAD_DIM ** -0.5)
        p = _softmax_rows(s)
        heads.append(_dot(p.astype(BF16), mv_ref[:, sl].astype(BF16)))
    o = jnp.concatenate(heads, axis=-1).astype(BF16)
    x3 = x2 + _dot(o, wco_ref[...])
    x4 = _ffn(x3, g2_ref, w1_ref, w3_ref, w2_ref)
    y_ref[...] = _rms(x4, gf_ref[...]) if final else x4


def _post_p(x1, conv, ret, mk, mv, layer, wo, gca, wq, wco, g2, w1, w3, w2, gf, final):
    row = lambda n: pl.BlockSpec((TM, n), lambda i: (i, 0))
    mem = lambda: pl.BlockSpec((N_MEM, D_MODEL), lambda i: (i // (SEQ // TM), 0))
    sq = lambda: _resident((D_MODEL, D_MODEL), layer)
    return pl.pallas_call(
        functools.partial(_post_p_kernel, final=final),
        grid=(T_P // TM,),
        in_specs=[row(D_MODEL), row(CONV_CH), row(RET_WIDTH), mem(), mem(),
                  sq(), _resident((1, D_MODEL), layer), sq(), sq(), _resident((1, D_MODEL), layer),
                  _resident((D_MODEL, D_FF), layer), _resident((D_MODEL, D_FF), layer),
                  _resident((D_FF, D_MODEL), layer),
                  pl.BlockSpec((1, D_MODEL), lambda i: (0, 0))],
        out_specs=row(D_MODEL),
        out_shape=jax.ShapeDtypeStruct((T_P, D_MODEL), F32),
        compiler_params=pltpu.CompilerParams(vmem_limit_bytes=VMEM_LIMIT),
        name="post_prompt",
    )(x1, conv, ret, mk, mv, wo, gca, wq, wco, g2, w1, w3, w2, gf)


def _post_a_s_kernel(x1_ref, conv_ref, ret_ref, wo_ref, gca_ref, wq_ref, x2_ref, q_ref):
    x2, q = _mix_out_q(x1_ref[...], conv_ref[...], ret_ref[...], wo_ref, gca_ref, wq_ref)
    x2_ref[...] = x2
    q_ref[...] = q


def _post_a_s(x1, conv, ret, layer, wo, gca, wq):
    row = lambda n: pl.BlockSpec((TM, n), lambda i: (i, 0))
    sq = lambda: _resident((D_MODEL, D_MODEL), layer)
    return pl.pallas_call(
        _post_a_s_kernel,
        grid=(T_S // TM,),
        in_specs=[row(D_MODEL), row(CONV_CH), row(RET_WIDTH), sq(), _resident((1, D_MODEL), layer), sq()],
        out_specs=[row(D_MODEL), row(D_MODEL)],
        out_shape=[jax.ShapeDtypeStruct((T_S, D_MODEL), F32)] * 2,
        name="mix_out_q_sample",
    )(x1, conv, ret, wo, gca, wq)


def _attn_s_kernel(q_ref, k_ref, v_ref, o_ref):
    def body(b, carry):
        for h in range(CA_HEADS):
            sl = slice(h * CA_HEAD_DIM, (h + 1) * CA_HEAD_DIM)
            s = _dot_nt(q_ref[b, :, sl].astype(BF16), k_ref[0, b, :, sl].astype(BF16)) * (CA_HEAD_DIM ** -0.5)
            p = _softmax_rows(s)
            o_ref[b, :, sl] = _dot(p.astype(BF16), v_ref[0, b, :, sl].astype(BF16))
        return carry

    lax.fori_loop(0, ATT_S_BT, body, 0)


def _attn_s(q16, cache_k, cache_v, layer):
    bt = ATT_S_BT
    qs = lambda: pl.BlockSpec((bt, S_PAD, D_MODEL), lambda i: (i, 0, 0))
    kv = lambda: pl.BlockSpec((1, bt, N_MEM, D_MODEL), lambda i: (layer, i, 0, 0))
    return pl.pallas_call(
        _attn_s_kernel,
        grid=(DEC_BATCH // bt,),
        in_specs=[qs(), kv(), kv()],
        out_specs=qs(),
        out_shape=jax.ShapeDtypeStruct((DEC_BATCH, S_PAD, D_MODEL), F32),
        compiler_params=pltpu.CompilerParams(vmem_limit_bytes=VMEM_LIMIT),
        name="cross_attn_sample",
    )(q16, cache_k, cache_v)


def _post_b_s_kernel(x2_ref, o_ref, wco_ref, g2_ref, w1_ref, w3_ref, w2_ref, gf_ref, y_ref, *, final):
    x3 = x2_ref[...] + _dot(o_ref[...].astype(BF16), wco_ref[...])
    x4 = _ffn(x3, g2_ref, w1_ref, w3_ref, w2_ref)
    y_ref[...] = _rms(x4, gf_ref[...]) if final else x4


def _post_b_s(x2, o, layer, wco, g2, w1, w3, w2, gf, final):
    row = lambda: pl.BlockSpec((TM, D_MODEL), lambda i: (i, 0))
    return pl.pallas_call(
        functools.partial(_post_b_s_kernel, final=final),
        grid=(T_S // TM,),
        in_specs=[row(), row(), _resident((D_MODEL, D_MODEL), layer), _resident((1, D_MODEL), layer),
                  _resident((D_MODEL, D_FF), layer), _resident((D_MODEL, D_FF), layer),
                  _resident((D_FF, D_MODEL), layer), pl.BlockSpec((1, D_MODEL), lambda i: (0, 0))],
        out_specs=row(),
        out_shape=jax.ShapeDtypeStruct((T_S, D_MODEL), F32),
        compiler_params=pltpu.CompilerParams(vmem_limit_bytes=VMEM_LIMIT),
        name="attn_out_ffn_sample",
    )(x2, o, wco, g2, w1, w3, w2, gf)


def _rope_tables(pos):
    inv_freq = ROPE_BASE ** (-jnp.arange(0, RET_DK, 2, dtype=F32) / RET_DK)
    ang = pos[:, None] * inv_freq[None, :]
    cos, sin = jnp.cos(ang), jnp.sin(ang)
    return jnp.concatenate([cos, cos], axis=-1), jnp.concatenate([-sin, sin], axis=-1)


def _decay_tables(c):
    log_gamma = jnp.log1p(-jnp.exp2(-5.0 - jnp.arange(RET_HEADS, dtype=F32)))
    idx = jnp.arange(c, dtype=F32)
    rel = idx[:, None] - idx[None, :]
    decay = jnp.where(rel[None] >= 0,
                      jnp.exp(log_gamma[:, None, None] * jnp.maximum(rel, 0.0)[None]), 0.0)
    q_dec = jnp.exp(log_gamma[:, None] * (idx[None, :] + 1.0))
    k_dec = jnp.exp(log_gamma[:, None] * (c - 1.0 - idx[None, :]))
    chunk_dec = jnp.exp(log_gamma * c)
    return decay, q_dec, k_dec, chunk_dec


def _prompt_tables():
    cos2, sin2 = _rope_tables(jnp.arange(SEQ, dtype=F32))
    decay, q_dec, k_dec, chunk_dec = _decay_tables(RET_CHUNK)
    lanes = (RET_HEADS, RET_CHUNK, RET_DK)
    return (cos2, sin2, decay,
            jnp.broadcast_to(q_dec[:, :, None], lanes),
            jnp.broadcast_to(k_dec[:, :, None], lanes),
            jnp.broadcast_to(chunk_dec[:, None, None], lanes))


def _sample_tables():
    pad_rows = lambda a: jnp.pad(a, [(0, 0)] * (a.ndim - 2) + [(0, S_PAD - DEC_SEQ), (0, 0)])
    cos2, sin2 = _rope_tables(PAST_LEN + jnp.arange(DEC_SEQ, dtype=F32))
    decay, q_dec, k_dec, chunk_dec = _decay_tables(DEC_SEQ)
    rows = (RET_HEADS, DEC_SEQ, RET_DK)
    dec = jnp.broadcast_to(jnp.swapaxes(decay, 1, 2)[:, :, :, None], (RET_HEADS, DEC_SEQ, DEC_SEQ, RET_DK))
    return (pad_rows(cos2), pad_rows(sin2), pad_rows(dec),
            pad_rows(jnp.broadcast_to(q_dec[:, :, None], rows)),
            pad_rows(jnp.broadcast_to(k_dec[:, :, None], rows)),
            jnp.broadcast_to(chunk_dec[:, None, None], (RET_HEADS, RET_DK, RET_DV)))


def _pad_tokens(a):
    a = a.reshape(DEC_BATCH, DEC_SEQ, a.shape[-1])
    return jnp.pad(a, ((0, 0), (0, S_PAD - DEC_SEQ), (0, 0)))


def _unpad_tokens(a):
    return a[:, :DEC_SEQ, :].reshape(T_S, a.shape[-1])


def kernel(x_prompt, x_sample, state_conv, state_ret, cache_mem_k, cache_mem_v, mem_prompt, g_ffn1, w1_ffn1, w3_ffn1, w2_ffn1, g_mix, w_in, conv_w, conv_b, conv_ln_g, conv_ln_b, ret_gn_g, w_out, g_ca, g_mem, w_cq, w_ck, w_cv, w_co, g_ffn2, w1_ffn2, w3_ffn2, w2_ffn2, g_final):
    bf = lambda w: w.astype(BF16)
    vec = lambda g: g.reshape(DEPTH, 1, -1)
    w1a, w3a, w2a, w1b, w3b, w2b = map(bf, (w1_ffn1, w3_ffn1, w2_ffn1, w1_ffn2, w3_ffn2, w2_ffn2))
    win, wout, wcq, wck, wcv, wco = map(bf, (w_in, w_out, w_cq, w_ck, w_cv, w_co))
    g1, gm, gca, gmem, g2 = map(vec, (g_ffn1, g_mix, g_ca, g_mem, g_ffn2))
    cb, clg, clb, gn = map(vec, (conv_b, conv_ln_g, conv_ln_b, ret_gn_g))
    gf = g_final.reshape(1, D_MODEL)
    cache_k = cache_mem_k.reshape(DEPTH, DEC_BATCH, N_MEM, D_MODEL)
    cache_v = cache_mem_v.reshape(DEPTH, DEC_BATCH, N_MEM, D_MODEL)
    mem = mem_prompt.reshape(BATCH * N_MEM, D_MODEL)
    tabs_p = _prompt_tables()
    tabs_s = _sample_tables()

    xp = x_prompt.reshape(T_P, D_MODEL)
    xs = x_sample.reshape(T_S, D_MODEL)
    conv_p, ret_p, memk_p, memv_p, conv_s, ret_s = [], [], [], [], [], []
    for l in range(DEPTH):
        final = l == DEPTH - 1
        wb = jnp.stack([jnp.pad(conv_w[l, :CONV_K - 1 - t], ((t, 0), (0, 0))) for t in range(DEC_SEQ)])

        mk, mv = _memkv(mem, l, gmem, wck, wcv)
        x1, u, r = _ffn_inproj(xp, l, g1, w1a, w3a, w2a, gm, win)
        cv, nbuf = _conv_p(u, l, conv_w, cb, clg, clb)
        rt, ns = _ret_p(r, l, tabs_p, gn)
        xp = _post_p(x1, cv, rt, mk, mv, l, wout, gca, wcq, wco, g2, w1b, w3b, w2b, gf, final)
        conv_p.append(nbuf); ret_p.append(ns)
        memk_p.append(mk.reshape(BATCH, N_MEM, CA_HEADS, CA_HEAD_DIM))
        memv_p.append(mv.reshape(BATCH, N_MEM, CA_HEADS, CA_HEAD_DIM))

        x1, u, r = _ffn_inproj(xs, l, g1, w1a, w3a, w2a, gm, win)
        u3 = u.reshape(DEC_BATCH, DEC_SEQ, CONV_CH)
        cv, nbuf = _conv_s(state_conv, u3, jnp.swapaxes(u3, 0, 1), l, conv_w, wb, cb, clg, clb)
        cv = jnp.swapaxes(cv, 0, 1).reshape(T_S, CONV_CH)
        rt, ns = _ret_s(_pad_tokens(r), state_ret, l, tabs_s, gn)
        x2, q = _post_a_s(x1, cv, _unpad_tokens(rt), l, wout, gca, wcq)
        o = _attn_s(_pad_tokens(q), cache_k, cache_v, l)
        xs = _post_b_s(x2, _unpad_tokens(o), l, wco, g2, w1b, w3b, w2b, gf, final)
        conv_s.append(nbuf); ret_s.append(ns[0])

    return (xp.reshape(BATCH, SEQ, D_MODEL), xs.reshape(DEC_BATCH, DEC_SEQ, D_MODEL),
            jnp.stack(conv_p), jnp.stack(ret_p), jnp.stack(memk_p), jnp.stack(memv_p),
            jnp.stack(conv_s), jnp.stack(ret_s))
```

```python
import functools

import jax
import jax.numpy as jnp
from jax import lax
from jax.experimental import pallas as pl
from jax.experimental.pallas import tpu as pltpu

F32 = jnp.float32
BF16 = jnp.bfloat16

D_MODEL = 1024
BATCH = 8
SEQ = 2048
DEPTH = 2
DEC_BATCH = 128
DEC_SEQ = 4
PAST_LEN = 16384
CONV_CH = 512
CONV_K = 31
RET_HEADS = 4
RET_DK = 128
RET_DV = 128
RET_WIDTH = 512
MIX_IN = 3072
RET_CHUNK = 128
ROPE_BASE = 10000.0
D_FF = 4096
N_MEM = 256
CA_HEADS = 4
CA_HEAD_DIM = 256
EPS = 1e-6
GN_EPS = 1e-5

T_P = BATCH * SEQ
T_S = DEC_BATCH * DEC_SEQ

TM = 256
TF = 1024
CONV_TT = 512
CONV_RC = 32
CONV_HALO = 32
SUBLANES = 8
LANES = 128
CONV_ROWS = CONV_HALO + CONV_TT
RET_CPS = 4
S_PAD = 16
Q_PAD = 8
RET_S_BT = 8
ATT_S_BT = 4
KV_ROWS = N_MEM * 2 * CA_HEADS
NEG_BIG = -1e30
CONV_S_BT = 16
VMEM_LIMIT = 56 * 1024 * 1024


def _dot(a, b):
    return jnp.dot(a, b, preferred_element_type=F32)


def _dot_nt(a, b):
    return lax.dot_general(a, b, (((1,), (1,)), ((), ())), preferred_element_type=F32)


def _rms(x, g):
    return x * lax.rsqrt(jnp.mean(x * x, axis=-1, keepdims=True) + EPS) * g


def _silu(x):
    return x * jax.nn.sigmoid(x)


def _ffn(x, g_ref, w1_ref, w3_ref, w2_ref):
    xn = _rms(x, g_ref[...]).astype(BF16)
    acc = None
    for c in range(D_FF // TF):
        sl = slice(c * TF, (c + 1) * TF)
        h1 = _dot(xn, w1_ref[:, sl])
        h3 = _dot(xn, w3_ref[:, sl])
        a = (_silu(h1) * h3).astype(BF16)
        part = _dot(a, w2_ref[sl, :])
        acc = part if acc is None else acc + part
    return x + 0.5 * acc


def _softmax_rows(s):
    m = jnp.max(s, axis=-1, keepdims=True)
    e = jnp.exp(s - m)
    return e * (1.0 / jnp.sum(e, axis=-1, keepdims=True))


def _group_norm_gate(o, gate, gn):
    mu = jnp.mean(o, axis=-1, keepdims=True)
    d = o - mu
    var = jnp.mean(d * d, axis=-1, keepdims=True)
    return _silu(gate) * (d * lax.rsqrt(var + GN_EPS) * gn)


def _rotary(t, cos2, sin2):
    return t * cos2 + pltpu.roll(t, RET_DK // 2, 1) * sin2


def _ffn_inproj_kernel(x_ref, g1_ref, w1_ref, w3_ref, w2_ref, gm_ref, win_ref,
                       x1_ref, u_ref, r_ref):
    x1 = _ffn(x_ref[...], g1_ref, w1_ref, w3_ref, w2_ref)
    x1_ref[...] = x1
    h = _rms(x1, gm_ref[...]).astype(BF16)
    proj = _dot(h, win_ref[...])
    u_ref[...] = proj[:, :CONV_CH] * jax.nn.sigmoid(proj[:, CONV_CH:2 * CONV_CH])
    r_ref[...] = proj[:, 2 * CONV_CH:]


def _resident(shape, layer):
    nd = len(shape)
    return pl.BlockSpec((None,) + tuple(shape), lambda *_: (layer,) + (0,) * nd,
                        pipeline_mode=pl.Buffered(1))


def _ffn_inproj(x, layer, g1, w1, w3, w2, gm, win):
    t = x.shape[0]
    row = lambda n: pl.BlockSpec((TM, n), lambda i: (i, 0))
    return pl.pallas_call(
        _ffn_inproj_kernel,
        grid=(t // TM,),
        in_specs=[row(D_MODEL), _resident((1, D_MODEL), layer),
                  _resident((D_MODEL, D_FF), layer), _resident((D_MODEL, D_FF), layer),
                  _resident((D_FF, D_MODEL), layer), _resident((1, D_MODEL), layer),
                  _resident((D_MODEL, MIX_IN), layer)],
        out_specs=[row(D_MODEL), row(CONV_CH), row(MIX_IN - 2 * CONV_CH)],
        out_shape=[jax.ShapeDtypeStruct((t, D_MODEL), F32),
                   jax.ShapeDtypeStruct((t, CONV_CH), F32),
                   jax.ShapeDtypeStruct((t, MIX_IN - 2 * CONV_CH), F32)],
        compiler_params=pltpu.CompilerParams(vmem_limit_bytes=VMEM_LIMIT),
        name="ffn_inproj",
    )(x, g1, w1, w3, w2, gm, win)


def _layer_norm_silu(y, lg, lb):
    mu = jnp.mean(y, axis=-1, keepdims=True)
    d = y - mu
    var = jnp.mean(d * d, axis=-1, keepdims=True)
    return _silu(d * lax.rsqrt(var + EPS) * lg + lb)


def _conv_p_kernel(u_ref, w_ref, b_ref, lg_ref, lb_ref, y_ref, nb_ref, win_ref):
    j = pl.program_id(1)

    @pl.when(j == 0)
    def _():
        win_ref[0, 0:CONV_HALO, :] = jnp.zeros((CONV_HALO, CONV_CH), F32)

    @pl.when(j > 0)
    def _():
        win_ref[0, 0:CONV_HALO, :] = win_ref[0, CONV_TT:CONV_TT + CONV_HALO, :]

    win_ref[0, CONV_HALO:CONV_ROWS, :] = u_ref[...]
    win_ref[0, CONV_ROWS:CONV_ROWS + SUBLANES, :] = jnp.zeros((SUBLANES, CONV_CH), F32)
    for sh in range(1, SUBLANES):
        win_ref[sh, 0:CONV_ROWS, :] = win_ref[0, sh:sh + CONV_ROWS, :]

    first = CONV_HALO - (CONV_K - 1)

    def chunk(c, carry):
        r0 = pl.multiple_of(c * CONV_RC, CONV_RC)
        acc = None
        for k in range(CONV_K):
            sh = (first + k) % SUBLANES
            term = w_ref[k:k + 1, :] * win_ref[sh, pl.ds(r0 + (first + k - sh), CONV_RC), :]
            acc = term if acc is None else acc + term
        y = acc + b_ref[...]
        y_ref[pl.ds(r0, CONV_RC), :] = _layer_norm_silu(y, lg_ref[...], lb_ref[...])
        return carry

    lax.fori_loop(0, CONV_TT // CONV_RC, chunk, 0)

    @pl.when(j == pl.num_programs(1) - 1)
    def _():
        nb_ref[0] = u_ref[CONV_TT - (CONV_K - 1):CONV_TT, :]


def _conv_p(u, layer, conv_w, conv_b, ln_g, ln_b):
    nt = SEQ // CONV_TT
    vec = lambda: pl.BlockSpec((None, 1, CONV_CH), lambda b, j: (layer, 0, 0))
    return pl.pallas_call(
        _conv_p_kernel,
        grid=(BATCH, nt),
        in_specs=[pl.BlockSpec((CONV_TT, CONV_CH), lambda b, j: (b * nt + j, 0)),
                  pl.BlockSpec((None, CONV_K, CONV_CH), lambda b, j: (layer, 0, 0)),
                  vec(), vec(), vec()],
        out_specs=[pl.BlockSpec((CONV_TT, CONV_CH), lambda b, j: (b * nt + j, 0)),
                   pl.BlockSpec((1, CONV_K - 1, CONV_CH), lambda b, j: (b, 0, 0))],
        out_shape=[jax.ShapeDtypeStruct((T_P, CONV_CH), F32),
                   jax.ShapeDtypeStruct((BATCH, CONV_K - 1, CONV_CH), F32)],
        scratch_shapes=[pltpu.VMEM((SUBLANES, CONV_ROWS + SUBLANES, CONV_CH), F32)],
        name="conv_prompt",
    )(u, conv_w, conv_b, ln_g, ln_b)


def _conv_s_kernel(st_ref, u_ref, ut_ref, w_ref, wb_ref, b_ref, lg_ref, lb_ref, y_ref, nb_ref):
    buf = st_ref[0]
    for t in range(DEC_SEQ):
        y = jnp.sum(buf * wb_ref[t][None], axis=1)
        for j in range(t + 1):
            k = CONV_K - 1 - t + j
            y = y + ut_ref[j] * w_ref[k:k + 1, :]
        y = y + b_ref[...]
        y_ref[t] = _layer_norm_silu(y, lg_ref[...], lb_ref[...])
    keep = CONV_K - 1 - DEC_SEQ
    nb_ref[:, 0:keep, :] = buf[:, DEC_SEQ:, :]
    nb_ref[:, keep:, :] = u_ref[...]


def _conv_s(state_conv, u3, ut, layer, conv_w, wb, conv_b, ln_g, ln_b):
    bt = CONV_S_BT
    vec = lambda: pl.BlockSpec((None, 1, CONV_CH), lambda i: (layer, 0, 0))
    return pl.pallas_call(
        _conv_s_kernel,
        grid=(DEC_BATCH // bt,),
        in_specs=[pl.BlockSpec((1, bt, CONV_K - 1, CONV_CH), lambda i: (layer, i, 0, 0)),
                  pl.BlockSpec((bt, DEC_SEQ, CONV_CH), lambda i: (i, 0, 0)),
                  pl.BlockSpec((DEC_SEQ, bt, CONV_CH), lambda i: (0, i, 0)),
                  pl.BlockSpec((None, CONV_K, CONV_CH), lambda i: (layer, 0, 0)),
                  pl.BlockSpec((DEC_SEQ, CONV_K - 1, CONV_CH), lambda i: (0, 0, 0)),
                  vec(), vec(), vec()],
        out_specs=[pl.BlockSpec((DEC_SEQ, bt, CONV_CH), lambda i: (0, i, 0)),
                   pl.BlockSpec((bt, CONV_K - 1, CONV_CH), lambda i: (i, 0, 0))],
        out_shape=[jax.ShapeDtypeStruct((DEC_SEQ, DEC_BATCH, CONV_CH), F32),
                   jax.ShapeDtypeStruct((DEC_BATCH, CONV_K - 1, CONV_CH), F32)],
        name="conv_sample",
    )(state_conv, u3, ut, conv_w, wb, conv_b, ln_g, ln_b)


def _ret_p_kernel(q_ref, k_ref, v_ref, g_ref, cos_ref, sin_ref, dec_ref, qd_ref, kd_ref,
                  cd_ref, gn_ref, o_ref, ns_ref, s_ref):
    j = pl.program_id(1)

    @pl.when(j == 0)
    def _():
        s_ref[...] = jnp.zeros_like(s_ref)

    for c in range(RET_CPS):
        rows = slice(c * RET_CHUNK, (c + 1) * RET_CHUNK)
        cos2 = cos_ref[rows, :]
        sin2 = sin_ref[rows, :]
        for h in range(RET_HEADS):
            sl = slice(h * RET_DK, (h + 1) * RET_DK)
            qh = _rotary(q_ref[rows, sl], cos2, sin2)
            kh = _rotary(k_ref[rows, sl], cos2, sin2) * (RET_DK ** -0.5)
            qb = qh.astype(BF16)
            vb = v_ref[rows, sl].astype(BF16)
            s_h = s_ref[h]
            sc = _dot_nt(qb, kh.astype(BF16)) * dec_ref[h]
            o = _dot(sc.astype(BF16), vb) + _dot(qb, s_h.astype(BF16)) * qd_ref[h]
            kdec = (kh * kd_ref[h]).astype(BF16)
            upd = lax.dot_general(kdec, vb, (((0,), (0,)), ((), ())), preferred_element_type=F32)
            s_ref[h] = s_h * cd_ref[h] + upd
            o_ref[rows, sl] = _group_norm_gate(o, g_ref[rows, sl], gn_ref[:, sl])

    @pl.when(j == pl.num_programs(1) - 1)
    def _():
        ns_ref[0] = s_ref[...]


def _ret_p(r, layer, tabs, gn):
    cos2, sin2, dec, qd, kd, cd = tabs
    rows = RET_CPS * RET_CHUNK
    nc = SEQ // rows
    col = lambda c: pl.BlockSpec((rows, RET_WIDTH), lambda b, j: (b * nc + j, c))
    const = lambda: pl.BlockSpec((RET_HEADS, RET_CHUNK, RET_DK), lambda b, j: (0, 0, 0))
    tab = lambda: pl.BlockSpec((rows, RET_DK), lambda b, j: (j, 0))
    return pl.pallas_call(
        _ret_p_kernel,
        grid=(BATCH, nc),
        in_specs=[col(0), col(1), col(2), col(3), tab(), tab(), const(), const(), const(), const(),
                  pl.BlockSpec((None, 1, RET_WIDTH), lambda b, j: (layer, 0, 0))],
        out_specs=[pl.BlockSpec((rows, RET_WIDTH), lambda b, j: (b * nc + j, 0)),
                   pl.BlockSpec((1, RET_HEADS, RET_DK, RET_DV), lambda b, j: (b, 0, 0, 0))],
        out_shape=[jax.ShapeDtypeStruct((T_P, RET_WIDTH), F32),
                   jax.ShapeDtypeStruct((BATCH, RET_HEADS, RET_DK, RET_DV), F32)],
        scratch_shapes=[pltpu.VMEM((RET_HEADS, RET_DK, RET_DV), F32)],
        name="retention_prompt",
    )(r, r, r, r, cos2, sin2, dec, qd, kd, cd, gn)


def _ret_s_kernel(q_ref, k_ref, v_ref, g_ref, st_ref, cos_ref, sin_ref, dec_ref, qd_ref, kd_ref,
                  cd_ref, gn_ref, o_ref, ns_ref):
    cos2 = cos_ref[...]
    sin2 = sin_ref[...]
    zpad = jnp.zeros((RET_DK - S_PAD, RET_DK), F32)

    def body(b, carry):
        for h in range(RET_HEADS):
            sl = slice(h * RET_DK, (h + 1) * RET_DK)
            qh = _rotary(q_ref[b, :, sl], cos2, sin2)
            kh = _rotary(k_ref[b, :, sl], cos2, sin2) * (RET_DK ** -0.5)
            vh = v_ref[b, :, sl]
            s_h = st_ref[0, b, h]
            o = _dot(qh.astype(BF16), s_h.astype(BF16)) * qd_ref[h]
            for jj in range(DEC_SEQ):
                sj = jnp.sum(qh * kh[jj:jj + 1, :], axis=-1, keepdims=True)
                o = o + (sj * dec_ref[h, jj]) * vh[jj:jj + 1, :]
            kdec = jnp.concatenate([kh * kd_ref[h], zpad], axis=0)
            vpad = jnp.concatenate([vh, zpad], axis=0)
            upd = _dot(kdec.T.astype(BF16), vpad.astype(BF16))
            ns_ref[0, b, h] = s_h * cd_ref[h] + upd
            o_ref[b, :, sl] = _group_norm_gate(o, g_ref[b, :, sl], gn_ref[:, sl])
        return carry

    lax.fori_loop(0, RET_S_BT, body, 0)


def _ret_s(r16, state_ret, layer, tabs, gn):
    cos2, sin2, dec, qd, kd, cd = tabs
    bt = RET_S_BT
    col = lambda c: pl.BlockSpec((bt, S_PAD, RET_WIDTH), lambda i: (i, 0, c))
    full = lambda a: pl.BlockSpec(a.shape, lambda i: (0,) * a.ndim)
    st = pl.BlockSpec((1, bt, RET_HEADS, RET_DK, RET_DV), lambda i: (layer, i, 0, 0, 0))
    return pl.pallas_call(
        _ret_s_kernel,
        grid=(DEC_BATCH // bt,),
        in_specs=[col(0), col(1), col(2), col(3), st, full(cos2), full(sin2), full(dec), full(qd),
                  full(kd), full(cd), pl.BlockSpec((None, 1, RET_WIDTH), lambda i: (layer, 0, 0))],
        out_specs=[pl.BlockSpec((bt, S_PAD, RET_WIDTH), lambda i: (i, 0, 0)),
                   pl.BlockSpec((1, bt, RET_HEADS, RET_DK, RET_DV), lambda i: (0, i, 0, 0, 0))],
        out_shape=[jax.ShapeDtypeStruct((DEC_BATCH, S_PAD, RET_WIDTH), F32),
                   jax.ShapeDtypeStruct((1, DEC_BATCH, RET_HEADS, RET_DK, RET_DV), F32)],
        name="retention_sample",
    )(r16, r16, r16, r16, state_ret, cos2, sin2, dec, qd, kd, cd, gn)


def _memkv_kernel(m_ref, g_ref, wk_ref, wv_ref, k_ref, v_ref):
    mn = _rms(m_ref[...], g_ref[...]).astype(BF16)
    k_ref[...] = _dot(mn, wk_ref[...])
    v_ref[...] = _dot(mn, wv_ref[...])


def _memkv(mem, layer, g, wk, wv):
    t = mem.shape[0]
    row = lambda: pl.BlockSpec((TM, D_MODEL), lambda i: (i, 0))
    return pl.pallas_call(
        _memkv_kernel,
        grid=(t // TM,),
        in_specs=[row(), _resident((1, D_MODEL), layer), _resident((D_MODEL, D_MODEL), layer),
                  _resident((D_MODEL, D_MODEL), layer)],
        out_specs=[row(), row()],
        out_shape=[jax.ShapeDtypeStruct((t, D_MODEL), F32)] * 2,
        name="memory_kv",
    )(mem, g, wk, wv)


def _mix_out_q(x1, conv, ret, wo_ref, gca_ref, wq_ref):
    mix = jnp.concatenate([conv, ret], axis=-1).astype(BF16)
    x2 = x1 + _dot(mix, wo_ref[...])
    q = _dot(_rms(x2, gca_ref[...]).astype(BF16), wq_ref[...])
    return x2, q


def _post_p_kernel(x1_ref, conv_ref, ret_ref, mk_ref, mv_ref, wo_ref, gca_ref, wq_ref, wco_ref,
                   g2_ref, w1_ref, w3_ref, w2_ref, gf_ref, y_ref, *, final):
    x2, q = _mix_out_q(x1_ref[...], conv_ref[...], ret_ref[...], wo_ref, gca_ref, wq_ref)
    heads = []
    for h in range(CA_HEADS):
        sl = slice(h * CA_HEAD_DIM, (h + 1) * CA_HEAD_DIM)
        s = _dot_nt(q[:, sl].astype(BF16), mk_ref[:, sl].astype(BF16)) * (CA_HEAD_DIM ** -0.5)
        p = _softmax_rows(s)
        heads.append(_dot(p.astype(BF16), mv_ref[:, sl].astype(BF16)))
    o = jnp.concatenate(heads, axis=-1).astype(BF16)
    x3 = x2 + _dot(o, wco_ref[...])
    x4 = _ffn(x3, g2_ref, w1_ref, w3_ref, w2_ref)
    y_ref[...] = _rms(x4, gf_ref[...]) if final else x4


def _post_p(x1, conv, ret, mk, mv, layer, wo, gca, wq, wco, g2, w1, w3, w2, gf, final):
    row = lambda n: pl.BlockSpec((TM, n), lambda i: (i, 0))
    mem = lambda: pl.BlockSpec((N_MEM, D_MODEL), lambda i: (i // (SEQ // TM), 0))
    sq = lambda: _resident((D_MODEL, D_MODEL), layer)
    return pl.pallas_call(
        functools.partial(_post_p_kernel, final=final),
        grid=(T_P // TM,),
        in_specs=[row(D_MODEL), row(CONV_CH), row(RET_WIDTH), mem(), mem(),
                  sq(), _resident((1, D_MODEL), layer), sq(), sq(), _resident((1, D_MODEL), layer),
                  _resident((D_MODEL, D_FF), layer), _resident((D_MODEL, D_FF), layer),
                  _resident((D_FF, D_MODEL), layer),
                  pl.BlockSpec((1, D_MODEL), lambda i: (0, 0))],
        out_specs=row(D_MODEL),
        out_shape=jax.ShapeDtypeStruct((T_P, D_MODEL), F32),
        compiler_params=pltpu.CompilerParams(vmem_limit_bytes=VMEM_LIMIT),
        name="post_prompt",
    )(x1, conv, ret, mk, mv, wo, gca, wq, wco, g2, w1, w3, w2, gf)


def _post_a_s_kernel(x1_ref, conv_ref, ret_ref, wo_ref, gca_ref, wq_ref, x2_ref, q_ref):
    x2, q = _mix_out_q(x1_ref[...], conv_ref[...], ret_ref[...], wo_ref, gca_ref, wq_ref)
    x2_ref[...] = x2
    q_ref[...] = q


def _post_a_s(x1, conv, ret, layer, wo, gca, wq):
    row = lambda n: pl.BlockSpec((TM, n), lambda i: (i, 0))
    sq = lambda: _resident((D_MODEL, D_MODEL), layer)
    return pl.pallas_call(
        _post_a_s_kernel,
        grid=(T_S // TM,),
        in_specs=[row(D_MODEL), row(CONV_CH), row(RET_WIDTH), sq(), _resident((1, D_MODEL), layer), sq()],
        out_specs=[row(D_MODEL), row(D_MODEL)],
        out_shape=[jax.ShapeDtypeStruct((T_S, D_MODEL), F32)] * 2,
        name="mix_out_q_sample",
    )(x1, conv, ret, wo, gca, wq)


def _attn_s_kernel(q_ref, k_ref, v_ref, o_ref):
    nh = CA_HEADS * Q_PAD
    lane = lax.broadcasted_iota(jnp.int32, (nh, KV_ROWS), 1)
    row = lax.broadcasted_iota(jnp.int32, (nh, KV_ROWS), 0)
    valid = (lane % (2 * CA_HEADS)) == (row // Q_PAD)
    for b in range(ATT_S_BT):
        q = q_ref[b]
        blocks = [q[:, (h * 2 + half) * LANES:(h * 2 + half + 1) * LANES]
                  for half in range(2) for h in range(CA_HEADS)]
        qt = jnp.concatenate(blocks, axis=0).astype(BF16)
        st = _dot_nt(qt, k_ref[0, b].astype(BF16))
        s = (st[:nh] + pltpu.roll(st[nh:], KV_ROWS - CA_HEADS, 1)) * (CA_HEAD_DIM ** -0.5)
        p = _softmax_rows(jnp.where(valid, s, NEG_BIG))
        w = jnp.concatenate([p, pltpu.roll(p, CA_HEADS, 1)], axis=0).astype(BF16)
        o = _dot(w, v_ref[0, b].astype(BF16))
        for half in range(2):
            for h in range(CA_HEADS):
                r0 = (half * CA_HEADS + h) * Q_PAD
                c0 = (h * 2 + half) * LANES
                o_ref[b, :, c0:c0 + LANES] = o[r0:r0 + Q_PAD]


def _flat_cache(c):
    c = c.reshape(DEPTH, DEC_BATCH, N_MEM, CA_HEADS, 2, LANES)
    return c.transpose(0, 1, 2, 4, 3, 5).reshape(DEPTH, DEC_BATCH, KV_ROWS, LANES)


def _attn_s(q8, cache_k, cache_v, layer):
    bt = ATT_S_BT
    qs = lambda: pl.BlockSpec((bt, Q_PAD, D_MODEL), lambda i: (i, 0, 0))
    kv = lambda: pl.BlockSpec((1, bt, KV_ROWS, LANES), lambda i: (layer, i, 0, 0))
    return pl.pallas_call(
        _attn_s_kernel,
        grid=(DEC_BATCH // bt,),
        in_specs=[qs(), kv(), kv()],
        out_specs=qs(),
        out_shape=jax.ShapeDtypeStruct((DEC_BATCH, Q_PAD, D_MODEL), F32),
        compiler_params=pltpu.CompilerParams(vmem_limit_bytes=VMEM_LIMIT),
        name="cross_attn_sample",
    )(q8, cache_k, cache_v)


def _post_b_s_kernel(x2_ref, o_ref, wco_ref, g2_ref, w1_ref, w3_ref, w2_ref, gf_ref, y_ref, *, final):
    x3 = x2_ref[...] + _dot(o_ref[...].astype(BF16), wco_ref[...])
    x4 = _ffn(x3, g2_ref, w1_ref, w3_ref, w2_ref)
    y_ref[...] = _rms(x4, gf_ref[...]) if final else x4


def _post_b_s(x2, o, layer, wco, g2, w1, w3, w2, gf, final):
    row = lambda: pl.BlockSpec((TM, D_MODEL), lambda i: (i, 0))
    return pl.pallas_call(
        functools.partial(_post_b_s_kernel, final=final),
        grid=(T_S // TM,),
        in_specs=[row(), row(), _resident((D_MODEL, D_MODEL), layer), _resident((1, D_MODEL), layer),
                  _resident((D_MODEL, D_FF), layer), _resident((D_MODEL, D_FF), layer),
                  _resident((D_FF, D_MODEL), layer), pl.BlockSpec((1, D_MODEL), lambda i: (0, 0))],
        out_specs=row(),
        out_shape=jax.ShapeDtypeStruct((T_S, D_MODEL), F32),
        compiler_params=pltpu.CompilerParams(vmem_limit_bytes=VMEM_LIMIT),
        name="attn_out_ffn_sample",
    )(x2, o, wco, g2, w1, w3, w2, gf)


def _rope_tables(pos):
    inv_freq = ROPE_BASE ** (-jnp.arange(0, RET_DK, 2, dtype=F32) / RET_DK)
    ang = pos[:, None] * inv_freq[None, :]
    cos, sin = jnp.cos(ang), jnp.sin(ang)
    return jnp.concatenate([cos, cos], axis=-1), jnp.concatenate([-sin, sin], axis=-1)


def _decay_tables(c):
    log_gamma = jnp.log1p(-jnp.exp2(-5.0 - jnp.arange(RET_HEADS, dtype=F32)))
    idx = jnp.arange(c, dtype=F32)
    rel = idx[:, None] - idx[None, :]
    decay = jnp.where(rel[None] >= 0,
                      jnp.exp(log_gamma[:, None, None] * jnp.maximum(rel, 0.0)[None]), 0.0)
    q_dec = jnp.exp(log_gamma[:, None] * (idx[None, :] + 1.0))
    k_dec = jnp.exp(log_gamma[:, None] * (c - 1.0 - idx[None, :]))
    chunk_dec = jnp.exp(log_gamma * c)
    return decay, q_dec, k_dec, chunk_dec


def _prompt_tables():
    cos2, sin2 = _rope_tables(jnp.arange(SEQ, dtype=F32))
    decay, q_dec, k_dec, chunk_dec = _decay_tables(RET_CHUNK)
    lanes = (RET_HEADS, RET_CHUNK, RET_DK)
    return (cos2, sin2, decay,
            jnp.broadcast_to(q_dec[:, :, None], lanes),
            jnp.broadcast_to(k_dec[:, :, None], lanes),
            jnp.broadcast_to(chunk_dec[:, None, None], lanes))


def _sample_tables():
    pad_rows = lambda a: jnp.pad(a, [(0, 0)] * (a.ndim - 2) + [(0, S_PAD - DEC_SEQ), (0, 0)])
    cos2, sin2 = _rope_tables(PAST_LEN + jnp.arange(DEC_SEQ, dtype=F32))
    decay, q_dec, k_dec, chunk_dec = _decay_tables(DEC_SEQ)
    rows = (RET_HEADS, DEC_SEQ, RET_DK)
    dec = jnp.broadcast_to(jnp.swapaxes(decay, 1, 2)[:, :, :, None], (RET_HEADS, DEC_SEQ, DEC_SEQ, RET_DK))
    return (pad_rows(cos2), pad_rows(sin2), pad_rows(dec),
            pad_rows(jnp.broadcast_to(q_dec[:, :, None], rows)),
            pad_rows(jnp.broadcast_to(k_dec[:, :, None], rows)),
            jnp.broadcast_to(chunk_dec[:, None, None], (RET_HEADS, RET_DK, RET_DV)))


def _pad_tokens(a, rows):
    a = a.reshape(DEC_BATCH, DEC_SEQ, a.shape[-1])
    return jnp.pad(a, ((0, 0), (0, rows - DEC_SEQ), (0, 0)))


def _unpad_tokens(a):
    return a[:, :DEC_SEQ, :].reshape(T_S, a.shape[-1])


def kernel(x_prompt, x_sample, state_conv, state_ret, cache_mem_k, cache_mem_v, mem_prompt, g_ffn1, w1_ffn1, w3_ffn1, w2_ffn1, g_mix, w_in, conv_w, conv_b, conv_ln_g, conv_ln_b, ret_gn_g, w_out, g_ca, g_mem, w_cq, w_ck, w_cv, w_co, g_ffn2, w1_ffn2, w3_ffn2, w2_ffn2, g_final):
    bf = lambda w: w.astype(BF16)
    vec = lambda g: g.reshape(DEPTH, 1, -1)
    w1a, w3a, w2a, w1b, w3b, w2b = map(bf, (w1_ffn1, w3_ffn1, w2_ffn1, w1_ffn2, w3_ffn2, w2_ffn2))
    win, wout, wcq, wck, wcv, wco = map(bf, (w_in, w_out, w_cq, w_ck, w_cv, w_co))
    g1, gm, gca, gmem, g2 = map(vec, (g_ffn1, g_mix, g_ca, g_mem, g_ffn2))
    cb, clg, clb, gn = map(vec, (conv_b, conv_ln_g, conv_ln_b, ret_gn_g))
    gf = g_final.reshape(1, D_MODEL)
    cache_k = _flat_cache(cache_mem_k)
    cache_v = _flat_cache(cache_mem_v)
    mem = mem_prompt.reshape(BATCH * N_MEM, D_MODEL)
    tabs_p = _prompt_tables()
    tabs_s = _sample_tables()

    xp = x_prompt.reshape(T_P, D_MODEL)
    xs = x_sample.reshape(T_S, D_MODEL)
    conv_p, ret_p, memk_p, memv_p, conv_s, ret_s = [], [], [], [], [], []
    for l in range(DEPTH):
        final = l == DEPTH - 1
        wb = jnp.stack([jnp.pad(conv_w[l, :CONV_K - 1 - t], ((t, 0), (0, 0))) for t in range(DEC_SEQ)])

        mk, mv = _memkv(mem, l, gmem, wck, wcv)
        x1, u, r = _ffn_inproj(xp, l, g1, w1a, w3a, w2a, gm, win)
        cv, nbuf = _conv_p(u, l, conv_w, cb, clg, clb)
        rt, ns = _ret_p(r, l, tabs_p, gn)
        xp = _post_p(x1, cv, rt, mk, mv, l, wout, gca, wcq, wco, g2, w1b, w3b, w2b, gf, final)
        conv_p.append(nbuf); ret_p.append(ns)
        memk_p.append(mk.reshape(BATCH, N_MEM, CA_HEADS, CA_HEAD_DIM))
        memv_p.append(mv.reshape(BATCH, N_MEM, CA_HEADS, CA_HEAD_DIM))

        x1, u, r = _ffn_inproj(xs, l, g1, w1a, w3a, w2a, gm, win)
        u3 = u.reshape(DEC_BATCH, DEC_SEQ, CONV_CH)
        cv, nbuf = _conv_s(state_conv, u3, jnp.swapaxes(u3, 0, 1), l, conv_w, wb, cb, clg, clb)
        cv = jnp.swapaxes(cv, 0, 1).reshape(T_S, CONV_CH)
        rt, ns = _ret_s(_pad_tokens(r, S_PAD), state_ret, l, tabs_s, gn)
        x2, q = _post_a_s(x1, cv, _unpad_tokens(rt), l, wout, gca, wcq)
        o = _attn_s(_pad_tokens(q, Q_PAD), cache_k, cache_v, l)
        xs = _post_b_s(x2, _unpad_tokens(o), l, wco, g2, w1b, w3b, w2b, gf, final)
        conv_s.append(nbuf); ret_s.append(ns[0])

    return (xp.reshape(BATCH, SEQ, D_MODEL), xs.reshape(DEC_BATCH, DEC_SEQ, D_MODEL),
            jnp.stack(conv_p), jnp.stack(ret_p), jnp.stack(memk_p), jnp.stack(memv_p),
            jnp.stack(conv_s), jnp.stack(ret_s))
```

```python
import functools

import jax
import jax.numpy as jnp
from jax import lax
from jax.experimental import pallas as pl
from jax.experimental.pallas import tpu as pltpu

F32 = jnp.float32
BF16 = jnp.bfloat16

D_MODEL = 1024
BATCH = 8
SEQ = 2048
DEPTH = 2
DEC_BATCH = 128
DEC_SEQ = 4
PAST_LEN = 16384
CONV_CH = 512
CONV_K = 31
RET_HEADS = 4
RET_DK = 128
RET_DV = 128
RET_WIDTH = 512
MIX_IN = 3072
RET_CHUNK = 128
ROPE_BASE = 10000.0
D_FF = 4096
N_MEM = 256
CA_HEADS = 4
CA_HEAD_DIM = 256
EPS = 1e-6
GN_EPS = 1e-5

T_P = BATCH * SEQ
T_S = DEC_BATCH * DEC_SEQ

TM = 256
TF = 1024
CONV_TT = 512
CONV_RC = 32
CONV_HALO = 32
SUBLANES = 8
LANES = 128
CONV_ROWS = CONV_HALO + CONV_TT
CONV_UNROLL = 2
RET_CPS = 4
RET_PB = 2
PAIR = SUBLANES // DEC_SEQ
RET_S_BT = 8
ATT_S_BT = 4
KV_ROWS = N_MEM * 2 * CA_HEADS
NEG_BIG = -1e30
CONV_S_BT = 16
VMEM_LIMIT = 56 * 1024 * 1024


def _dot(a, b):
    return jnp.dot(a, b, preferred_element_type=F32)


def _dot_nt(a, b):
    return lax.dot_general(a, b, (((1,), (1,)), ((), ())), preferred_element_type=F32)


def _rms(x, g):
    return x * lax.rsqrt(jnp.mean(x * x, axis=-1, keepdims=True) + EPS) * g


def _silu(x):
    return x * jax.nn.sigmoid(x)


def _ffn(x, g_ref, w1_ref, w3_ref, w2_ref):
    xn = _rms(x, g_ref[...]).astype(BF16)
    acc = None
    for c in range(D_FF // TF):
        sl = slice(c * TF, (c + 1) * TF)
        h1 = _dot(xn, w1_ref[:, sl])
        h3 = _dot(xn, w3_ref[:, sl])
        a = (_silu(h1) * h3).astype(BF16)
        part = _dot(a, w2_ref[sl, :])
        acc = part if acc is None else acc + part
    return x + 0.5 * acc


def _softmax_rows(s):
    m = jnp.max(s, axis=-1, keepdims=True)
    e = jnp.exp(s - m)
    return e * (1.0 / jnp.sum(e, axis=-1, keepdims=True))


def _group_norm_gate(o, gate, gn):
    mu = jnp.mean(o, axis=-1, keepdims=True)
    d = o - mu
    var = jnp.mean(d * d, axis=-1, keepdims=True)
    return _silu(gate) * (d * lax.rsqrt(var + GN_EPS) * gn)


def _rotary(t, cos2, sin2):
    return t * cos2 + pltpu.roll(t, RET_DK // 2, 1) * sin2


def _ffn_inproj_kernel(x_ref, g1_ref, w1_ref, w3_ref, w2_ref, gm_ref, win_ref,
                       x1_ref, u_ref, r_ref):
    x1 = _ffn(x_ref[...], g1_ref, w1_ref, w3_ref, w2_ref)
    x1_ref[...] = x1
    h = _rms(x1, gm_ref[...]).astype(BF16)
    proj = _dot(h, win_ref[...])
    u_ref[...] = proj[:, :CONV_CH] * jax.nn.sigmoid(proj[:, CONV_CH:2 * CONV_CH])
    r_ref[...] = proj[:, 2 * CONV_CH:]


def _resident(shape, layer):
    nd = len(shape)
    return pl.BlockSpec((None,) + tuple(shape), lambda *_: (layer,) + (0,) * nd,
                        pipeline_mode=pl.Buffered(1))


def _ffn_inproj(x, layer, g1, w1, w3, w2, gm, win):
    t = x.shape[0]
    row = lambda n: pl.BlockSpec((TM, n), lambda i: (i, 0))
    return pl.pallas_call(
        _ffn_inproj_kernel,
        grid=(t // TM,),
        in_specs=[row(D_MODEL), _resident((1, D_MODEL), layer),
                  _resident((D_MODEL, D_FF), layer), _resident((D_MODEL, D_FF), layer),
                  _resident((D_FF, D_MODEL), layer), _resident((1, D_MODEL), layer),
                  _resident((D_MODEL, MIX_IN), layer)],
        out_specs=[row(D_MODEL), row(CONV_CH), row(MIX_IN - 2 * CONV_CH)],
        out_shape=[jax.ShapeDtypeStruct((t, D_MODEL), F32),
                   jax.ShapeDtypeStruct((t, CONV_CH), F32),
                   jax.ShapeDtypeStruct((t, MIX_IN - 2 * CONV_CH), F32)],
        compiler_params=pltpu.CompilerParams(vmem_limit_bytes=VMEM_LIMIT),
        name="ffn_inproj",
    )(x, g1, w1, w3, w2, gm, win)


def _layer_norm_silu(y, lg, lb):
    mu = jnp.mean(y, axis=-1, keepdims=True)
    d = y - mu
    var = jnp.mean(d * d, axis=-1, keepdims=True)
    return _silu(d * lax.rsqrt(var + EPS) * lg + lb)


def _conv_p_kernel(u_ref, w_ref, b_ref, lg_ref, lb_ref, y_ref, nb_ref, win_ref):
    j = pl.program_id(1)

    @pl.when(j == 0)
    def _():
        win_ref[0, 0:CONV_HALO, :] = jnp.zeros((CONV_HALO, CONV_CH), F32)

    @pl.when(j > 0)
    def _():
        win_ref[0, 0:CONV_HALO, :] = win_ref[0, CONV_TT:CONV_TT + CONV_HALO, :]

    win_ref[0, CONV_HALO:CONV_ROWS, :] = u_ref[...]
    win_ref[0, CONV_ROWS:CONV_ROWS + SUBLANES, :] = jnp.zeros((SUBLANES, CONV_CH), F32)
    for sh in range(1, SUBLANES):
        win_ref[sh, 0:CONV_ROWS, :] = win_ref[0, sh:sh + CONV_ROWS, :]

    first = CONV_HALO - (CONV_K - 1)

    def chunk(c, carry):
        for e in range(CONV_UNROLL):
            r0 = pl.multiple_of((c * CONV_UNROLL + e) * CONV_RC, CONV_RC)
            acc = None
            for k in range(CONV_K):
                sh = (first + k) % SUBLANES
                wk = jnp.concatenate([w_ref[k]] * (CONV_RC // SUBLANES), axis=0)
                term = wk * win_ref[sh, pl.ds(r0 + (first + k - sh), CONV_RC), :]
                acc = term if acc is None else acc + term
            y = acc + b_ref[...]
            y_ref[pl.ds(r0, CONV_RC), :] = _layer_norm_silu(y, lg_ref[...], lb_ref[...])
        return carry

    lax.fori_loop(0, CONV_TT // (CONV_RC * CONV_UNROLL), chunk, 0)

    @pl.when(j == pl.num_programs(1) - 1)
    def _():
        nb_ref[0] = u_ref[CONV_TT - (CONV_K - 1):CONV_TT, :]


def _conv_p(u, layer, conv_w, conv_b, ln_g, ln_b):
    nt = SEQ // CONV_TT
    vec = lambda: pl.BlockSpec((None, 1, CONV_CH), lambda b, j: (layer, 0, 0))
    return pl.pallas_call(
        _conv_p_kernel,
        grid=(BATCH, nt),
        in_specs=[pl.BlockSpec((CONV_TT, CONV_CH), lambda b, j: (b * nt + j, 0)),
                  pl.BlockSpec((None, CONV_K, SUBLANES, CONV_CH), lambda b, j: (layer, 0, 0, 0)),
                  vec(), vec(), vec()],
        out_specs=[pl.BlockSpec((CONV_TT, CONV_CH), lambda b, j: (b * nt + j, 0)),
                   pl.BlockSpec((1, CONV_K - 1, CONV_CH), lambda b, j: (b, 0, 0))],
        out_shape=[jax.ShapeDtypeStruct((T_P, CONV_CH), F32),
                   jax.ShapeDtypeStruct((BATCH, CONV_K - 1, CONV_CH), F32)],
        scratch_shapes=[pltpu.VMEM((SUBLANES, CONV_ROWS + SUBLANES, CONV_CH), F32)],
        name="conv_prompt",
    )(u, conv_w, conv_b, ln_g, ln_b)


def _conv_s_kernel(st_ref, u_ref, ut_ref, w_ref, wb_ref, b_ref, lg_ref, lb_ref, y_ref, nb_ref):
    buf = st_ref[0]
    for t in range(DEC_SEQ):
        y = jnp.sum(buf * wb_ref[t][None], axis=1)
        for j in range(t + 1):
            k = CONV_K - 1 - t + j
            y = y + ut_ref[j] * w_ref[k:k + 1, :]
        y = y + b_ref[...]
        y_ref[t] = _layer_norm_silu(y, lg_ref[...], lb_ref[...])
    keep = CONV_K - 1 - DEC_SEQ
    nb_ref[:, 0:keep, :] = buf[:, DEC_SEQ:, :]
    nb_ref[:, keep:, :] = u_ref[...]


def _conv_s(state_conv, u3, ut, layer, conv_w, wb, conv_b, ln_g, ln_b):
    bt = CONV_S_BT
    vec = lambda: pl.BlockSpec((None, 1, CONV_CH), lambda i: (layer, 0, 0))
    return pl.pallas_call(
        _conv_s_kernel,
        grid=(DEC_BATCH // bt,),
        in_specs=[pl.BlockSpec((1, bt, CONV_K - 1, CONV_CH), lambda i: (layer, i, 0, 0)),
                  pl.BlockSpec((bt, DEC_SEQ, CONV_CH), lambda i: (i, 0, 0)),
                  pl.BlockSpec((DEC_SEQ, bt, CONV_CH), lambda i: (0, i, 0)),
                  pl.BlockSpec((None, CONV_K, CONV_CH), lambda i: (layer, 0, 0)),
                  pl.BlockSpec((DEC_SEQ, CONV_K - 1, CONV_CH), lambda i: (0, 0, 0)),
                  vec(), vec(), vec()],
        out_specs=[pl.BlockSpec((DEC_SEQ, bt, CONV_CH), lambda i: (0, i, 0)),
                   pl.BlockSpec((bt, CONV_K - 1, CONV_CH), lambda i: (i, 0, 0))],
        out_shape=[jax.ShapeDtypeStruct((DEC_SEQ, DEC_BATCH, CONV_CH), F32),
                   jax.ShapeDtypeStruct((DEC_BATCH, CONV_K - 1, CONV_CH), F32)],
        name="conv_sample",
    )(state_conv, u3, ut, conv_w, wb, conv_b, ln_g, ln_b)


def _ret_p_kernel(q_ref, k_ref, v_ref, g_ref, cos_ref, sin_ref, dec_ref, qd_ref, kd_ref,
                  cd_ref, gn_ref, o_ref, ns_ref, s_ref):
    j = pl.program_id(1)

    @pl.when(j == 0)
    def _():
        s_ref[...] = jnp.zeros_like(s_ref)

    for c in range(RET_CPS):
        rows = slice(c * RET_CHUNK, (c + 1) * RET_CHUNK)
        cos2 = cos_ref[rows, :]
        sin2 = sin_ref[rows, :]
        for h in range(RET_HEADS):
            sl = slice(h * RET_DK, (h + 1) * RET_DK)
            for e in range(RET_PB):
                qh = _rotary(q_ref[e, rows, sl], cos2, sin2)
                kh = _rotary(k_ref[e, rows, sl], cos2, sin2) * (RET_DK ** -0.5)
                qb = qh.astype(BF16)
                vb = v_ref[e, rows, sl].astype(BF16)
                s_h = s_ref[e, h]
                sc = _dot_nt(qb, kh.astype(BF16)) * dec_ref[h]
                o = _dot(sc.astype(BF16), vb) + _dot(qb, s_h.astype(BF16)) * qd_ref[h]
                kdec = (kh * kd_ref[h]).astype(BF16)
                upd = lax.dot_general(kdec, vb, (((0,), (0,)), ((), ())), preferred_element_type=F32)
                s_ref[e, h] = s_h * cd_ref[h] + upd
                o_ref[e, rows, sl] = _group_norm_gate(o, g_ref[e, rows, sl], gn_ref[:, sl])

    @pl.when(j == pl.num_programs(1) - 1)
    def _():
        ns_ref[...] = s_ref[...]


def _ret_p(r, layer, tabs, gn):
    cos2, sin2, dec, qd, kd, cd = tabs
    rows = RET_CPS * RET_CHUNK
    r3 = r.reshape(BATCH, SEQ, r.shape[-1])
    col = lambda c: pl.BlockSpec((RET_PB, rows, RET_WIDTH), lambda b, j: (b, j, c))
    const = lambda: pl.BlockSpec((RET_HEADS, RET_CHUNK, RET_DK), lambda b, j: (0, 0, 0))
    tab = lambda: pl.BlockSpec((rows, RET_DK), lambda b, j: (j, 0))
    out, ns = pl.pallas_call(
        _ret_p_kernel,
        grid=(BATCH // RET_PB, SEQ // rows),
        in_specs=[col(0), col(1), col(2), col(3), tab(), tab(), const(), const(), const(), const(),
                  pl.BlockSpec((None, 1, RET_WIDTH), lambda b, j: (layer, 0, 0))],
        out_specs=[pl.BlockSpec((RET_PB, rows, RET_WIDTH), lambda b, j: (b, j, 0)),
                   pl.BlockSpec((RET_PB, RET_HEADS, RET_DK, RET_DV), lambda b, j: (b, 0, 0, 0))],
        out_shape=[jax.ShapeDtypeStruct((BATCH, SEQ, RET_WIDTH), F32),
                   jax.ShapeDtypeStruct((BATCH, RET_HEADS, RET_DK, RET_DV), F32)],
        scratch_shapes=[pltpu.VMEM((RET_PB, RET_HEADS, RET_DK, RET_DV), F32)],
        name="retention_prompt",
    )(r3, r3, r3, r3, cos2, sin2, dec, qd, kd, cd, gn)
    return out.reshape(T_P, RET_WIDTH), ns


def _ret_s_kernel(q_ref, k_ref, v_ref, g_ref, st_ref, cos_ref, sin_ref, dec_ref, qd_ref, kd_ref,
                  cd_ref, gn_ref, o_ref, ns_ref):
    cos2 = cos_ref[...]
    sin2 = sin_ref[...]
    zpad = jnp.zeros((RET_DK - SUBLANES, RET_DK), F32)
    second = lax.broadcasted_iota(jnp.int32, (SUBLANES, RET_DK), 0) >= DEC_SEQ

    def pick(a, jj):
        return jnp.where(second, a[DEC_SEQ + jj:DEC_SEQ + jj + 1, :], a[jj:jj + 1, :])

    def body(p, carry):
        rows = pl.ds(pl.multiple_of(p * SUBLANES, SUBLANES), SUBLANES)
        for h in range(RET_HEADS):
            sl = slice(h * RET_DK, (h + 1) * RET_DK)
            qh = _rotary(q_ref[rows, sl], cos2, sin2)
            kh = _rotary(k_ref[rows, sl], cos2, sin2) * (RET_DK ** -0.5)
            vh = v_ref[rows, sl]
            qb = qh.astype(BF16)
            kdec = kh * kd_ref[h]
            vpad = jnp.concatenate([vh, zpad], axis=0).astype(BF16)
            from_state = []
            for x in range(PAIR):
                s_x = st_ref[0, p * PAIR + x, h]
                from_state.append(_dot(qb, s_x.astype(BF16)))
                mine = second if x else jnp.logical_not(second)
                kx = jnp.concatenate([jnp.where(mine, kdec, 0.0), zpad], axis=0)
                upd = _dot(kx.T.astype(BF16), vpad)
                ns_ref[0, p * PAIR + x, h] = s_x * cd_ref[h] + upd
            o = jnp.where(second, from_state[1], from_state[0]) * qd_ref[h]
            for jj in range(DEC_SEQ):
                sj = jnp.sum(qh * pick(kh, jj), axis=-1, keepdims=True)
                o = o + (sj * dec_ref[h, jj]) * pick(vh, jj)
            o_ref[rows, sl] = _group_norm_gate(o, g_ref[rows, sl], gn_ref[:, sl])
        return carry

    lax.fori_loop(0, RET_S_BT // PAIR, body, 0, unroll=2)


def _ret_s(r, state_ret, layer, tabs, gn):
    cos2, sin2, dec, qd, kd, cd = tabs
    bt = RET_S_BT
    col = lambda c: pl.BlockSpec((bt * DEC_SEQ, RET_WIDTH), lambda i: (i, c))
    full = lambda a: pl.BlockSpec(a.shape, lambda i: (0,) * a.ndim)
    st = pl.BlockSpec((1, bt, RET_HEADS, RET_DK, RET_DV), lambda i: (layer, i, 0, 0, 0))
    return pl.pallas_call(
        _ret_s_kernel,
        grid=(DEC_BATCH // bt,),
        in_specs=[col(0), col(1), col(2), col(3), st, full(cos2), full(sin2), full(dec), full(qd),
                  full(kd), full(cd), pl.BlockSpec((None, 1, RET_WIDTH), lambda i: (layer, 0, 0))],
        out_specs=[pl.BlockSpec((bt * DEC_SEQ, RET_WIDTH), lambda i: (i, 0)),
                   pl.BlockSpec((1, bt, RET_HEADS, RET_DK, RET_DV), lambda i: (0, i, 0, 0, 0))],
        out_shape=[jax.ShapeDtypeStruct((T_S, RET_WIDTH), F32),
                   jax.ShapeDtypeStruct((1, DEC_BATCH, RET_HEADS, RET_DK, RET_DV), F32)],
        name="retention_sample",
    )(r, r, r, r, state_ret, cos2, sin2, dec, qd, kd, cd, gn)


def _memkv_kernel(m_ref, g_ref, wk_ref, wv_ref, k_ref, v_ref):
    mn = _rms(m_ref[...], g_ref[...]).astype(BF16)
    k_ref[...] = _dot(mn, wk_ref[...])
    v_ref[...] = _dot(mn, wv_ref[...])


def _memkv(mem, layer, g, wk, wv):
    t = mem.shape[0]
    row = lambda: pl.BlockSpec((TM, D_MODEL), lambda i: (i, 0))
    return pl.pallas_call(
        _memkv_kernel,
        grid=(t // TM,),
        in_specs=[row(), _resident((1, D_MODEL), layer), _resident((D_MODEL, D_MODEL), layer),
                  _resident((D_MODEL, D_MODEL), layer)],
        out_specs=[row(), row()],
        out_shape=[jax.ShapeDtypeStruct((t, D_MODEL), F32)] * 2,
        name="memory_kv",
    )(mem, g, wk, wv)


def _mix_out_q(x1, conv, ret, wo_ref, gca_ref, wq_ref):
    mix = jnp.concatenate([conv, ret], axis=-1).astype(BF16)
    x2 = x1 + _dot(mix, wo_ref[...])
    q = _dot(_rms(x2, gca_ref[...]).astype(BF16), wq_ref[...])
    return x2, q


def _post_p_kernel(x1_ref, conv_ref, ret_ref, mk_ref, mv_ref, wo_ref, gca_ref, wq_ref, wco_ref,
                   g2_ref, w1_ref, w3_ref, w2_ref, gf_ref, y_ref, *, final):
    x2, q = _mix_out_q(x1_ref[...], conv_ref[...], ret_ref[...], wo_ref, gca_ref, wq_ref)
    heads = []
    for h in range(CA_HEADS):
        sl = slice(h * CA_HEAD_DIM, (h + 1) * CA_HEAD_DIM)
        s = _dot_nt(q[:, sl].astype(BF16), mk_ref[:, sl].astype(BF16)) * (CA_HEAD_DIM ** -0.5)
        p = _softmax_rows(s)
        heads.append(_dot(p.astype(BF16), mv_ref[:, sl].astype(BF16)))
    o = jnp.concatenate(heads, axis=-1).astype(BF16)
    x3 = x2 + _dot(o, wco_ref[...])
    x4 = _ffn(x3, g2_ref, w1_ref, w3_ref, w2_ref)
    y_ref[...] = _rms(x4, gf_ref[...]) if final else x4


def _post_p(x1, conv, ret, mk, mv, layer, wo, gca, wq, wco, g2, w1, w3, w2, gf, final):
    row = lambda n: pl.BlockSpec((TM, n), lambda i: (i, 0))
    mem = lambda: pl.BlockSpec((N_MEM, D_MODEL), lambda i: (i // (SEQ // TM), 0))
    sq = lambda: _resident((D_MODEL, D_MODEL), layer)
    return pl.pallas_call(
        functools.partial(_post_p_kernel, final=final),
        grid=(T_P // TM,),
        in_specs=[row(D_MODEL), row(CONV_CH), row(RET_WIDTH), mem(), mem(),
                  sq(), _resident((1, D_MODEL), layer), sq(), sq(), _resident((1, D_MODEL), layer),
                  _resident((D_MODEL, D_FF), layer), _resident((D_MODEL, D_FF), layer),
                  _resident((D_FF, D_MODEL), layer),
                  pl.BlockSpec((1, D_MODEL), lambda i: (0, 0))],
        out_specs=row(D_MODEL),
        out_shape=jax.ShapeDtypeStruct((T_P, D_MODEL), F32),
        compiler_params=pltpu.CompilerParams(vmem_limit_bytes=VMEM_LIMIT),
        name="post_prompt",
    )(x1, conv, ret, mk, mv, wo, gca, wq, wco, g2, w1, w3, w2, gf)


def _post_a_s_kernel(x1_ref, conv_ref, ret_ref, wo_ref, gca_ref, wq_ref, x2_ref, q_ref):
    x2, q = _mix_out_q(x1_ref[...], conv_ref[...], ret_ref[...], wo_ref, gca_ref, wq_ref)
    x2_ref[...] = x2
    q_ref[...] = q


def _post_a_s(x1, conv, ret, layer, wo, gca, wq):
    row = lambda n: pl.BlockSpec((TM, n), lambda i: (i, 0))
    sq = lambda: _resident((D_MODEL, D_MODEL), layer)
    return pl.pallas_call(
        _post_a_s_kernel,
        grid=(T_S // TM,),
        in_specs=[row(D_MODEL), row(CONV_CH), row(RET_WIDTH), sq(), _resident((1, D_MODEL), layer), sq()],
        out_specs=[row(D_MODEL), row(D_MODEL)],
        out_shape=[jax.ShapeDtypeStruct((T_S, D_MODEL), F32)] * 2,
        name="mix_out_q_sample",
    )(x1, conv, ret, wo, gca, wq)


def _attn_s_kernel(q_ref, k_ref, v_ref, o_ref):
    nh = CA_HEADS * SUBLANES
    lane = lax.broadcasted_iota(jnp.int32, (nh, KV_ROWS), 1)
    row = lax.broadcasted_iota(jnp.int32, (nh, KV_ROWS), 0)
    valid = (lane % (2 * CA_HEADS)) == (row // SUBLANES)
    second = lax.broadcasted_iota(jnp.int32, (2 * nh, LANES), 0) % SUBLANES >= DEC_SEQ
    for pr in range(ATT_S_BT // PAIR):
        q = q_ref[pr * SUBLANES:(pr + 1) * SUBLANES, :]
        blocks = [q[:, (h * 2 + half) * LANES:(h * 2 + half + 1) * LANES]
                  for half in range(2) for h in range(CA_HEADS)]
        qt = jnp.concatenate(blocks, axis=0).astype(BF16)
        outs = []
        for x in range(PAIR):
            b = pr * PAIR + x
            st = _dot_nt(qt, k_ref[0, b].astype(BF16))
            s = (st[:nh] + pltpu.roll(st[nh:], KV_ROWS - CA_HEADS, 1)) * (CA_HEAD_DIM ** -0.5)
            p = _softmax_rows(jnp.where(valid, s, NEG_BIG))
            w = jnp.concatenate([p, pltpu.roll(p, CA_HEADS, 1)], axis=0).astype(BF16)
            outs.append(_dot(w, v_ref[0, b].astype(BF16)))
        o = jnp.where(second, outs[1], outs[0])
        for half in range(2):
            for h in range(CA_HEADS):
                r0 = (half * CA_HEADS + h) * SUBLANES
                c0 = (h * 2 + half) * LANES
                o_ref[pr * SUBLANES:(pr + 1) * SUBLANES, c0:c0 + LANES] = o[r0:r0 + SUBLANES]


def _flat_cache(c):
    c = c.reshape(DEPTH, DEC_BATCH, N_MEM, CA_HEADS, 2, LANES)
    return c.transpose(0, 1, 2, 4, 3, 5).reshape(DEPTH, DEC_BATCH, KV_ROWS, LANES)


def _attn_s(q, cache_k, cache_v, layer):
    bt = ATT_S_BT
    qs = lambda: pl.BlockSpec((bt * DEC_SEQ, D_MODEL), lambda i: (i, 0))
    kv = lambda: pl.BlockSpec((1, bt, KV_ROWS, LANES), lambda i: (layer, i, 0, 0))
    return pl.pallas_call(
        _attn_s_kernel,
        grid=(DEC_BATCH // bt,),
        in_specs=[qs(), kv(), kv()],
        out_specs=qs(),
        out_shape=jax.ShapeDtypeStruct((T_S, D_MODEL), F32),
        compiler_params=pltpu.CompilerParams(vmem_limit_bytes=VMEM_LIMIT),
        name="cross_attn_sample",
    )(q, cache_k, cache_v)


def _post_b_s_kernel(x2_ref, o_ref, wco_ref, g2_ref, w1_ref, w3_ref, w2_ref, gf_ref, y_ref, *, final):
    x3 = x2_ref[...] + _dot(o_ref[...].astype(BF16), wco_ref[...])
    x4 = _ffn(x3, g2_ref, w1_ref, w3_ref, w2_ref)
    y_ref[...] = _rms(x4, gf_ref[...]) if final else x4


def _post_b_s(x2, o, layer, wco, g2, w1, w3, w2, gf, final):
    row = lambda: pl.BlockSpec((TM, D_MODEL), lambda i: (i, 0))
    return pl.pallas_call(
        functools.partial(_post_b_s_kernel, final=final),
        grid=(T_S // TM,),
        in_specs=[row(), row(), _resident((D_MODEL, D_MODEL), layer), _resident((1, D_MODEL), layer),
                  _resident((D_MODEL, D_FF), layer), _resident((D_MODEL, D_FF), layer),
                  _resident((D_FF, D_MODEL), layer), pl.BlockSpec((1, D_MODEL), lambda i: (0, 0))],
        out_specs=row(),
        out_shape=jax.ShapeDtypeStruct((T_S, D_MODEL), F32),
        compiler_params=pltpu.CompilerParams(vmem_limit_bytes=VMEM_LIMIT),
        name="attn_out_ffn_sample",
    )(x2, o, wco, g2, w1, w3, w2, gf)


def _rope_tables(pos):
    inv_freq = ROPE_BASE ** (-jnp.arange(0, RET_DK, 2, dtype=F32) / RET_DK)
    ang = pos[:, None] * inv_freq[None, :]
    cos, sin = jnp.cos(ang), jnp.sin(ang)
    return jnp.concatenate([cos, cos], axis=-1), jnp.concatenate([-sin, sin], axis=-1)


def _decay_tables(c):
    log_gamma = jnp.log1p(-jnp.exp2(-5.0 - jnp.arange(RET_HEADS, dtype=F32)))
    idx = jnp.arange(c, dtype=F32)
    rel = idx[:, None] - idx[None, :]
    decay = jnp.where(rel[None] >= 0,
                      jnp.exp(log_gamma[:, None, None] * jnp.maximum(rel, 0.0)[None]), 0.0)
    q_dec = jnp.exp(log_gamma[:, None] * (idx[None, :] + 1.0))
    k_dec = jnp.exp(log_gamma[:, None] * (c - 1.0 - idx[None, :]))
    chunk_dec = jnp.exp(log_gamma * c)
    return decay, q_dec, k_dec, chunk_dec


def _prompt_tables():
    cos2, sin2 = _rope_tables(jnp.arange(SEQ, dtype=F32))
    decay, q_dec, k_dec, chunk_dec = _decay_tables(RET_CHUNK)
    lanes = (RET_HEADS, RET_CHUNK, RET_DK)
    return (cos2, sin2, decay,
            jnp.broadcast_to(q_dec[:, :, None], lanes),
            jnp.broadcast_to(k_dec[:, :, None], lanes),
            jnp.broadcast_to(chunk_dec[:, None, None], lanes))


def _sample_tables():
    slab = lambda a: jnp.concatenate([a] * PAIR, axis=-2)
    cos2, sin2 = _rope_tables(PAST_LEN + jnp.arange(DEC_SEQ, dtype=F32))
    decay, q_dec, k_dec, chunk_dec = _decay_tables(DEC_SEQ)
    rows = (RET_HEADS, DEC_SEQ, RET_DK)
    dec = jnp.broadcast_to(jnp.swapaxes(decay, 1, 2)[:, :, :, None], (RET_HEADS, DEC_SEQ, DEC_SEQ, RET_DK))
    return (slab(cos2), slab(sin2), slab(dec),
            slab(jnp.broadcast_to(q_dec[:, :, None], rows)),
            slab(jnp.broadcast_to(k_dec[:, :, None], rows)),
            jnp.broadcast_to(chunk_dec[:, None, None], (RET_HEADS, RET_DK, RET_DV)))


def kernel(x_prompt, x_sample, state_conv, state_ret, cache_mem_k, cache_mem_v, mem_prompt, g_ffn1, w1_ffn1, w3_ffn1, w2_ffn1, g_mix, w_in, conv_w, conv_b, conv_ln_g, conv_ln_b, ret_gn_g, w_out, g_ca, g_mem, w_cq, w_ck, w_cv, w_co, g_ffn2, w1_ffn2, w3_ffn2, w2_ffn2, g_final):
    bf = lambda w: w.astype(BF16)
    vec = lambda g: g.reshape(DEPTH, 1, -1)
    w1a, w3a, w2a, w1b, w3b, w2b = map(bf, (w1_ffn1, w3_ffn1, w2_ffn1, w1_ffn2, w3_ffn2, w2_ffn2))
    win, wout, wcq, wck, wcv, wco = map(bf, (w_in, w_out, w_cq, w_ck, w_cv, w_co))
    g1, gm, gca, gmem, g2 = map(vec, (g_ffn1, g_mix, g_ca, g_mem, g_ffn2))
    cb, clg, clb, gn = map(vec, (conv_b, conv_ln_g, conv_ln_b, ret_gn_g))
    gf = g_final.reshape(1, D_MODEL)
    cache_k = _flat_cache(cache_mem_k)
    cache_v = _flat_cache(cache_mem_v)
    mem = mem_prompt.reshape(BATCH * N_MEM, D_MODEL)
    tabs_p = _prompt_tables()
    tabs_s = _sample_tables()
    cw8 = jnp.broadcast_to(conv_w[:, :, None, :], (DEPTH, CONV_K, SUBLANES, CONV_CH))

    xp = x_prompt.reshape(T_P, D_MODEL)
    xs = x_sample.reshape(T_S, D_MODEL)
    conv_p, ret_p, memk_p, memv_p, conv_s, ret_s = [], [], [], [], [], []
    for l in range(DEPTH):
        final = l == DEPTH - 1
        wb = jnp.stack([jnp.pad(conv_w[l, :CONV_K - 1 - t], ((t, 0), (0, 0))) for t in range(DEC_SEQ)])

        mk, mv = _memkv(mem, l, gmem, wck, wcv)
        x1, u, r = _ffn_inproj(xp, l, g1, w1a, w3a, w2a, gm, win)
        cv, nbuf = _conv_p(u, l, cw8, cb, clg, clb)
        rt, ns = _ret_p(r, l, tabs_p, gn)
        xp = _post_p(x1, cv, rt, mk, mv, l, wout, gca, wcq, wco, g2, w1b, w3b, w2b, gf, final)
        conv_p.append(nbuf); ret_p.append(ns)
        memk_p.append(mk.reshape(BATCH, N_MEM, CA_HEADS, CA_HEAD_DIM))
        memv_p.append(mv.reshape(BATCH, N_MEM, CA_HEADS, CA_HEAD_DIM))

        x1, u, r = _ffn_inproj(xs, l, g1, w1a, w3a, w2a, gm, win)
        u3 = u.reshape(DEC_BATCH, DEC_SEQ, CONV_CH)
        cv, nbuf = _conv_s(state_conv, u3, jnp.swapaxes(u3, 0, 1), l, conv_w, wb, cb, clg, clb)
        cv = jnp.swapaxes(cv, 0, 1).reshape(T_S, CONV_CH)
        rt, ns = _ret_s(r, state_ret, l, tabs_s, gn)
        x2, q = _post_a_s(x1, cv, rt, l, wout, gca, wcq)
        o = _attn_s(q, cache_k, cache_v, l)
        xs = _post_b_s(x2, o, l, wco, g2, w1b, w3b, w2b, gf, final)
        conv_s.append(nbuf); ret_s.append(ns[0])

    return (xp.reshape(BATCH, SEQ, D_MODEL), xs.reshape(DEC_BATCH, DEC_SEQ, D_MODEL),
            jnp.stack(conv_p), jnp.stack(ret_p), jnp.stack(memk_p), jnp.stack(memv_p),
            jnp.stack(conv_s), jnp.stack(ret_s))
```

```python
import functools

import jax
import jax.numpy as jnp
from jax import lax
from jax.experimental import pallas as pl
from jax.experimental.pallas import tpu as pltpu

F32 = jnp.float32
BF16 = jnp.bfloat16

D_MODEL = 1024
BATCH = 8
SEQ = 2048
DEPTH = 2
DEC_BATCH = 128
DEC_SEQ = 4
PAST_LEN = 16384
CONV_CH = 512
CONV_K = 31
RET_HEADS = 4
RET_DK = 128
RET_DV = 128
RET_WIDTH = 512
MIX_IN = 3072
RET_CHUNK = 128
ROPE_BASE = 10000.0
D_FF = 4096
N_MEM = 256
CA_HEADS = 4
CA_HEAD_DIM = 256
EPS = 1e-6
GN_EPS = 1e-5

T_P = BATCH * SEQ
T_S = DEC_BATCH * DEC_SEQ

TM = 256
TF = 1024
CONV_TT = 512
CONV_RC = 64
CONV_HALO = 32
SUBLANES = 8
LANES = 128
CONV_ROWS = CONV_HALO + CONV_TT
CONV_LT = CONV_CH // LANES
CONV_NORM_UNROLL = 4
RET_CPS = 4
RET_PB = 2
PAIR = SUBLANES // DEC_SEQ
RET_S_BT = 8
ATT_S_BT = 4
KV_ROWS = N_MEM * 2 * CA_HEADS
NEG_BIG = -1e30
CONV_S_BT = 32
VMEM_LIMIT = 56 * 1024 * 1024


def _dot(a, b):
    return jnp.dot(a, b, preferred_element_type=F32)


def _dot_nt(a, b):
    return lax.dot_general(a, b, (((1,), (1,)), ((), ())), preferred_element_type=F32)


def _rms(x, g):
    return x * lax.rsqrt(jnp.mean(x * x, axis=-1, keepdims=True) + EPS) * g


def _silu(x):
    return x * jax.nn.sigmoid(x)


def _ffn(x, g_ref, w1_ref, w3_ref, w2_ref):
    xn = _rms(x, g_ref[...]).astype(BF16)
    acc = None
    for c in range(D_FF // TF):
        sl = slice(c * TF, (c + 1) * TF)
        h1 = _dot(xn, w1_ref[:, sl])
        h3 = _dot(xn, w3_ref[:, sl])
        a = (_silu(h1) * h3).astype(BF16)
        part = _dot(a, w2_ref[sl, :])
        acc = part if acc is None else acc + part
    return x + 0.5 * acc


def _softmax_rows(s):
    m = jnp.max(s, axis=-1, keepdims=True)
    e = jnp.exp(s - m)
    return e * (1.0 / jnp.sum(e, axis=-1, keepdims=True))


def _group_norm_gate(o, gate, gn):
    mu = jnp.mean(o, axis=-1, keepdims=True)
    d = o - mu
    var = jnp.mean(d * d, axis=-1, keepdims=True)
    return _silu(gate) * (d * lax.rsqrt(var + GN_EPS) * gn)


def _rotary(t, cos2, sin2):
    return t * cos2 + pltpu.roll(t, RET_DK // 2, 1) * sin2


def _ffn_inproj_kernel(x_ref, g1_ref, w1_ref, w3_ref, w2_ref, gm_ref, win_ref,
                       x1_ref, u_ref, r_ref):
    x1 = _ffn(x_ref[...], g1_ref, w1_ref, w3_ref, w2_ref)
    x1_ref[...] = x1
    h = _rms(x1, gm_ref[...]).astype(BF16)
    proj = _dot(h, win_ref[...])
    u_ref[...] = proj[:, :CONV_CH] * jax.nn.sigmoid(proj[:, CONV_CH:2 * CONV_CH])
    r_ref[...] = proj[:, 2 * CONV_CH:]


def _resident(shape, layer):
    nd = len(shape)
    return pl.BlockSpec((None,) + tuple(shape), lambda *_: (layer,) + (0,) * nd,
                        pipeline_mode=pl.Buffered(1))


def _layered_call(body, n_in, prev, **kw):
    idx = sorted(prev)
    kw["in_specs"] = list(kw["in_specs"]) + [pl.BlockSpec(memory_space=pl.ANY)] * len(idx)

    def with_aliased(*refs):
        return body(*refs[:n_in], *refs[n_in + len(idx):])

    call = pl.pallas_call(with_aliased, input_output_aliases={n_in + k: o for k, o in enumerate(idx)}, **kw)
    return lambda *args: call(*args, *[prev[o] for o in idx])


def _ffn_inproj(x, layer, g1, w1, w3, w2, gm, win):
    t = x.shape[0]
    row = lambda n: pl.BlockSpec((TM, n), lambda i: (i, 0))
    return pl.pallas_call(
        _ffn_inproj_kernel,
        grid=(t // TM,),
        in_specs=[row(D_MODEL), _resident((1, D_MODEL), layer),
                  _resident((D_MODEL, D_FF), layer), _resident((D_MODEL, D_FF), layer),
                  _resident((D_FF, D_MODEL), layer), _resident((1, D_MODEL), layer),
                  _resident((D_MODEL, MIX_IN), layer)],
        out_specs=[row(D_MODEL), row(CONV_CH), row(MIX_IN - 2 * CONV_CH)],
        out_shape=[jax.ShapeDtypeStruct((t, D_MODEL), F32),
                   jax.ShapeDtypeStruct((t, CONV_CH), F32),
                   jax.ShapeDtypeStruct((t, MIX_IN - 2 * CONV_CH), F32)],
        compiler_params=pltpu.CompilerParams(vmem_limit_bytes=VMEM_LIMIT),
        name="ffn_inproj",
    )(x, g1, w1, w3, w2, gm, win)


def _layer_norm_silu(y, lg, lb):
    mu = jnp.mean(y, axis=-1, keepdims=True)
    d = y - mu
    var = jnp.mean(d * d, axis=-1, keepdims=True)
    return _silu(d * lax.rsqrt(var + EPS) * lg + lb)


def _conv_p_kernel(u_ref, w_ref, b_ref, lg_ref, lb_ref, y_ref, nb_ref, win_ref, acc_ref):
    j = pl.program_id(1)
    for t in range(CONV_LT):
        @pl.when(j == 0)
        def _():
            win_ref[0, t, 0:CONV_HALO, :] = jnp.zeros((CONV_HALO, LANES), F32)

        @pl.when(j > 0)
        def _():
            win_ref[0, t, 0:CONV_HALO, :] = win_ref[0, t, CONV_TT:CONV_TT + CONV_HALO, :]

        win_ref[0, t, CONV_HALO:CONV_ROWS, :] = u_ref[:, t * LANES:(t + 1) * LANES]
        win_ref[0, t, CONV_ROWS:CONV_ROWS + SUBLANES, :] = jnp.zeros((SUBLANES, LANES), F32)
        for sh in range(1, SUBLANES):
            win_ref[sh, t, 0:CONV_ROWS, :] = win_ref[0, t, sh:sh + CONV_ROWS, :]

    first = CONV_HALO - (CONV_K - 1)
    n_rc = CONV_TT // CONV_RC

    def taps(idx, carry):
        t = idx // n_rc
        r0 = pl.multiple_of((idx % n_rc) * CONV_RC, CONV_RC)
        accs = [None] * (CONV_RC // SUBLANES)
        for k in range(CONV_K):
            sh = (first + k) % SUBLANES
            wk = w_ref[t, k]
            for i in range(CONV_RC // SUBLANES):
                a0 = pl.multiple_of(r0 + (first + k - sh) + i * SUBLANES, SUBLANES)
                term = wk * win_ref[sh, t, pl.ds(a0, SUBLANES), :]
                accs[i] = term if accs[i] is None else accs[i] + term
        acc_ref[t, pl.ds(r0, CONV_RC), :] = jnp.concatenate(accs, axis=0)
        return carry

    lax.fori_loop(0, CONV_LT * n_rc, taps, 0)

    def norm(c, carry):
        r0 = pl.multiple_of(c * CONV_RC, CONV_RC)
        y = jnp.concatenate([acc_ref[t, pl.ds(r0, CONV_RC), :] for t in range(CONV_LT)], axis=-1) + b_ref[...]
        y_ref[pl.ds(r0, CONV_RC), :] = _layer_norm_silu(y, lg_ref[...], lb_ref[...])
        return carry

    lax.fori_loop(0, n_rc, norm, 0, unroll=CONV_NORM_UNROLL)

    @pl.when(j == pl.num_programs(1) - 1)
    def _():
        nb_ref[0, 0] = u_ref[CONV_TT - (CONV_K - 1):CONV_TT, :]


def _conv_p(u, layer, conv_w, conv_b, ln_g, ln_b, prev_nb):
    nt = SEQ // CONV_TT
    vec = lambda: pl.BlockSpec((None, 1, CONV_CH), lambda b, j: (layer, 0, 0))
    return _layered_call(
        _conv_p_kernel, 5, {1: prev_nb},
        grid=(BATCH, nt),
        in_specs=[pl.BlockSpec((CONV_TT, CONV_CH), lambda b, j: (b * nt + j, 0)),
                  pl.BlockSpec((None, CONV_LT, CONV_K, SUBLANES, LANES), lambda b, j: (layer, 0, 0, 0, 0)),
                  vec(), vec(), vec()],
        out_specs=[pl.BlockSpec((CONV_TT, CONV_CH), lambda b, j: (b * nt + j, 0)),
                   pl.BlockSpec((1, 1, CONV_K - 1, CONV_CH), lambda b, j: (layer, b, 0, 0))],
        out_shape=[jax.ShapeDtypeStruct((T_P, CONV_CH), F32),
                   jax.ShapeDtypeStruct((DEPTH, BATCH, CONV_K - 1, CONV_CH), F32)],
        scratch_shapes=[pltpu.VMEM((SUBLANES, CONV_LT, CONV_ROWS + SUBLANES, LANES), F32),
                        pltpu.VMEM((CONV_LT, CONV_TT, LANES), F32)],
        name="conv_prompt",
    )(u, conv_w, conv_b, ln_g, ln_b)


def _conv_s_kernel(st_ref, ut_ref, w_ref, b_ref, lg_ref, lb_ref, y_ref, nb_ref):
    keep = CONV_K - 1 - DEC_SEQ
    for t in range(DEC_SEQ):
        y = None
        for j in range(t, CONV_K - 1):
            term = st_ref[0, j] * w_ref[j - t:j - t + 1, :]
            y = term if y is None else y + term
        for j in range(t + 1):
            k = CONV_K - 1 - t + j
            y = y + ut_ref[j] * w_ref[k:k + 1, :]
        y_ref[t] = _layer_norm_silu(y + b_ref[...], lg_ref[...], lb_ref[...])
    for j in range(keep):
        nb_ref[0, j] = st_ref[0, j + DEC_SEQ]
    for t in range(DEC_SEQ):
        nb_ref[0, keep + t] = ut_ref[t]


def _conv_s(state_t, ut, layer, conv_w, conv_b, ln_g, ln_b, prev_nb):
    bt = CONV_S_BT
    vec = lambda: pl.BlockSpec((None, 1, CONV_CH), lambda i: (layer, 0, 0))
    return _layered_call(
        _conv_s_kernel, 6, {1: prev_nb},
        grid=(DEC_BATCH // bt,),
        in_specs=[pl.BlockSpec((1, CONV_K - 1, bt, CONV_CH), lambda i: (layer, 0, i, 0)),
                  pl.BlockSpec((DEC_SEQ, bt, CONV_CH), lambda i: (0, i, 0)),
                  pl.BlockSpec((None, CONV_K, CONV_CH), lambda i: (layer, 0, 0)),
                  vec(), vec(), vec()],
        out_specs=[pl.BlockSpec((DEC_SEQ, bt, CONV_CH), lambda i: (0, i, 0)),
                   pl.BlockSpec((1, CONV_K - 1, bt, CONV_CH), lambda i: (layer, 0, i, 0))],
        out_shape=[jax.ShapeDtypeStruct((DEC_SEQ, DEC_BATCH, CONV_CH), F32),
                   jax.ShapeDtypeStruct((DEPTH, CONV_K - 1, DEC_BATCH, CONV_CH), F32)],
        name="conv_sample",
    )(state_t, ut, conv_w, conv_b, ln_g, ln_b)


def _ret_p_kernel(q_ref, k_ref, v_ref, g_ref, cos_ref, sin_ref, dec_ref, qd_ref, kd_ref,
                  cd_ref, gn_ref, o_ref, ns_ref, s_ref):
    j = pl.program_id(1)

    @pl.when(j == 0)
    def _():
        s_ref[...] = jnp.zeros_like(s_ref)

    for c in range(RET_CPS):
        rows = slice(c * RET_CHUNK, (c + 1) * RET_CHUNK)
        cos2 = cos_ref[rows, :]
        sin2 = sin_ref[rows, :]
        for h in range(RET_HEADS):
            sl = slice(h * RET_DK, (h + 1) * RET_DK)
            for e in range(RET_PB):
                qh = _rotary(q_ref[e, rows, sl], cos2, sin2)
                kh = _rotary(k_ref[e, rows, sl], cos2, sin2) * (RET_DK ** -0.5)
                qb = qh.astype(BF16)
                vb = v_ref[e, rows, sl].astype(BF16)
                s_h = s_ref[e, h]
                sc = _dot_nt(qb, kh.astype(BF16)) * dec_ref[h]
                o = _dot(sc.astype(BF16), vb) + _dot(qb, s_h.astype(BF16)) * qd_ref[h]
                kdec = (kh * kd_ref[h]).astype(BF16)
                upd = lax.dot_general(kdec, vb, (((0,), (0,)), ((), ())), preferred_element_type=F32)
                s_ref[e, h] = s_h * cd_ref[h] + upd
                o_ref[e, rows, sl] = _group_norm_gate(o, g_ref[e, rows, sl], gn_ref[:, sl])

    @pl.when(j == pl.num_programs(1) - 1)
    def _():
        ns_ref[0] = s_ref[...]


def _ret_p(r, layer, tabs, gn, prev_ns):
    cos2, sin2, dec, qd, kd, cd = tabs
    rows = RET_CPS * RET_CHUNK
    r3 = r.reshape(BATCH, SEQ, r.shape[-1])
    col = lambda c: pl.BlockSpec((RET_PB, rows, RET_WIDTH), lambda b, j: (b, j, c))
    const = lambda: pl.BlockSpec((RET_HEADS, RET_CHUNK, RET_DK), lambda b, j: (0, 0, 0))
    tab = lambda: pl.BlockSpec((rows, RET_DK), lambda b, j: (j, 0))
    out, ns = _layered_call(
        _ret_p_kernel, 11, {1: prev_ns},
        grid=(BATCH // RET_PB, SEQ // rows),
        in_specs=[col(0), col(1), col(2), col(3), tab(), tab(), const(), const(), const(), const(),
                  pl.BlockSpec((None, 1, RET_WIDTH), lambda b, j: (layer, 0, 0))],
        out_specs=[pl.BlockSpec((RET_PB, rows, RET_WIDTH), lambda b, j: (b, j, 0)),
                   pl.BlockSpec((1, RET_PB, RET_HEADS, RET_DK, RET_DV), lambda b, j: (layer, b, 0, 0, 0))],
        out_shape=[jax.ShapeDtypeStruct((BATCH, SEQ, RET_WIDTH), F32),
                   jax.ShapeDtypeStruct((DEPTH, BATCH, RET_HEADS, RET_DK, RET_DV), F32)],
        scratch_shapes=[pltpu.VMEM((RET_PB, RET_HEADS, RET_DK, RET_DV), F32)],
        name="retention_prompt",
    )(r3, r3, r3, r3, cos2, sin2, dec, qd, kd, cd, gn)
    return out.reshape(T_P, RET_WIDTH), ns


def _ret_s_kernel(q_ref, k_ref, v_ref, g_ref, st_ref, cos_ref, sin_ref, dec_ref, qd_ref, kd_ref,
                  cd_ref, gn_ref, o_ref, ns_ref):
    cos2 = cos_ref[...]
    sin2 = sin_ref[...]
    zpad = jnp.zeros((RET_DK - SUBLANES, RET_DK), F32)
    second = lax.broadcasted_iota(jnp.int32, (SUBLANES, RET_DK), 0) >= DEC_SEQ

    def pick(a, jj):
        return jnp.where(second, a[DEC_SEQ + jj:DEC_SEQ + jj + 1, :], a[jj:jj + 1, :])

    def body(p, carry):
        rows = pl.ds(pl.multiple_of(p * SUBLANES, SUBLANES), SUBLANES)
        for h in range(RET_HEADS):
            sl = slice(h * RET_DK, (h + 1) * RET_DK)
            qh = _rotary(q_ref[rows, sl], cos2, sin2)
            kh = _rotary(k_ref[rows, sl], cos2, sin2) * (RET_DK ** -0.5)
            vh = v_ref[rows, sl]
            qb = qh.astype(BF16)
            kdec = kh * kd_ref[h]
            vpad = jnp.concatenate([vh, zpad], axis=0).astype(BF16)
            from_state = []
            for x in range(PAIR):
                s_x = st_ref[0, p * PAIR + x, h]
                from_state.append(_dot(qb, s_x.astype(BF16)))
                mine = second if x else jnp.logical_not(second)
                kx = jnp.concatenate([jnp.where(mine, kdec, 0.0), zpad], axis=0)
                upd = _dot(kx.T.astype(BF16), vpad)
                ns_ref[0, p * PAIR + x, h] = s_x * cd_ref[h] + upd
            o = jnp.where(second, from_state[1], from_state[0]) * qd_ref[h]
            for jj in range(DEC_SEQ):
                sj = jnp.sum(qh * pick(kh, jj), axis=-1, keepdims=True)
                o = o + (sj * dec_ref[h, jj]) * pick(vh, jj)
            o_ref[rows, sl] = _group_norm_gate(o, g_ref[rows, sl], gn_ref[:, sl])
        return carry

    lax.fori_loop(0, RET_S_BT // PAIR, body, 0, unroll=2)


def _ret_s(r, state_ret, layer, tabs, gn, prev_ns):
    cos2, sin2, dec, qd, kd, cd = tabs
    bt = RET_S_BT
    col = lambda c: pl.BlockSpec((bt * DEC_SEQ, RET_WIDTH), lambda i: (i, c))
    full = lambda a: pl.BlockSpec(a.shape, lambda i: (0,) * a.ndim)
    st = lambda: pl.BlockSpec((1, bt, RET_HEADS, RET_DK, RET_DV), lambda i: (layer, i, 0, 0, 0))
    return _layered_call(
        _ret_s_kernel, 12, {1: prev_ns},
        grid=(DEC_BATCH // bt,),
        in_specs=[col(0), col(1), col(2), col(3), st(), full(cos2), full(sin2), full(dec), full(qd),
                  full(kd), full(cd), pl.BlockSpec((None, 1, RET_WIDTH), lambda i: (layer, 0, 0))],
        out_specs=[pl.BlockSpec((bt * DEC_SEQ, RET_WIDTH), lambda i: (i, 0)), st()],
        out_shape=[jax.ShapeDtypeStruct((T_S, RET_WIDTH), F32),
                   jax.ShapeDtypeStruct((DEPTH, DEC_BATCH, RET_HEADS, RET_DK, RET_DV), F32)],
        name="retention_sample",
    )(r, r, r, r, state_ret, cos2, sin2, dec, qd, kd, cd, gn)


def _memkv_kernel(m_ref, g_ref, wk_ref, wv_ref, kf_ref, vf_ref, kb_ref, vb_ref):
    mn = _rms(m_ref[...], g_ref[...]).astype(BF16)
    for w_ref, flat_ref, b_ref in ((wk_ref, kf_ref, kb_ref), (wv_ref, vf_ref, vb_ref)):
        y = _dot(mn, w_ref[...])
        b_ref[...] = y.astype(BF16)
        for h in range(CA_HEADS):
            for half in range(2):
                c0 = (h * 2 + half) * LANES
                flat_ref[0, pl.ds(half * CA_HEADS + h, N_MEM, stride=2 * CA_HEADS), :] = y[:, c0:c0 + LANES]


def _memkv(mem, layer, g, wk, wv, prev):
    t = mem.shape[0]
    row = lambda: pl.BlockSpec((N_MEM, D_MODEL), lambda i: (i, 0))
    flat = lambda: pl.BlockSpec((1, KV_ROWS, LANES), lambda i: (layer, i, 0))
    flat_shape = jax.ShapeDtypeStruct((DEPTH, BATCH * KV_ROWS, LANES), F32)
    return _layered_call(
        _memkv_kernel, 4, {0: prev[0], 1: prev[1]},
        grid=(t // N_MEM,),
        in_specs=[row(), _resident((1, D_MODEL), layer), _resident((D_MODEL, D_MODEL), layer),
                  _resident((D_MODEL, D_MODEL), layer)],
        out_specs=[flat(), flat(), row(), row()],
        out_shape=[flat_shape, flat_shape, jax.ShapeDtypeStruct((t, D_MODEL), BF16),
                   jax.ShapeDtypeStruct((t, D_MODEL), BF16)],
        name="memory_kv",
    )(mem, g, wk, wv)


def _mix_out_q(x1, conv, ret, wo_ref, gca_ref, wq_ref):
    mix = jnp.concatenate([conv, ret], axis=-1).astype(BF16)
    x2 = x1 + _dot(mix, wo_ref[...])
    q = _dot(_rms(x2, gca_ref[...]).astype(BF16), wq_ref[...])
    return x2, q


def _post_p_kernel(x1_ref, conv_ref, ret_ref, mk_ref, mv_ref, wo_ref, gca_ref, wq_ref, wco_ref,
                   g2_ref, w1_ref, w3_ref, w2_ref, gf_ref, y_ref, *, final):
    x2, q = _mix_out_q(x1_ref[...], conv_ref[...], ret_ref[...], wo_ref, gca_ref, wq_ref)
    heads = []
    for h in range(CA_HEADS):
        sl = slice(h * CA_HEAD_DIM, (h + 1) * CA_HEAD_DIM)
        s = _dot_nt(q[:, sl].astype(BF16), mk_ref[:, sl]) * (CA_HEAD_DIM ** -0.5)
        p = _softmax_rows(s)
        heads.append(_dot(p.astype(BF16), mv_ref[:, sl]))
    o = jnp.concatenate(heads, axis=-1).astype(BF16)
    x3 = x2 + _dot(o, wco_ref[...])
    x4 = _ffn(x3, g2_ref, w1_ref, w3_ref, w2_ref)
    y_ref[...] = _rms(x4, gf_ref[...]) if final else x4


def _post_p(x1, conv, ret, mk, mv, layer, wo, gca, wq, wco, g2, w1, w3, w2, gf, final):
    row = lambda n: pl.BlockSpec((TM, n), lambda i: (i, 0))
    mem = lambda: pl.BlockSpec((N_MEM, D_MODEL), lambda i: (i // (SEQ // TM), 0))
    sq = lambda: _resident((D_MODEL, D_MODEL), layer)
    return pl.pallas_call(
        functools.partial(_post_p_kernel, final=final),
        grid=(T_P // TM,),
        in_specs=[row(D_MODEL), row(CONV_CH), row(RET_WIDTH), mem(), mem(),
                  sq(), _resident((1, D_MODEL), layer), sq(), sq(), _resident((1, D_MODEL), layer),
                  _resident((D_MODEL, D_FF), layer), _resident((D_MODEL, D_FF), layer),
                  _resident((D_FF, D_MODEL), layer),
                  pl.BlockSpec((1, D_MODEL), lambda i: (0, 0))],
        out_specs=row(D_MODEL),
        out_shape=jax.ShapeDtypeStruct((T_P, D_MODEL), F32),
        compiler_params=pltpu.CompilerParams(vmem_limit_bytes=VMEM_LIMIT),
        name="post_prompt",
    )(x1, conv, ret, mk, mv, wo, gca, wq, wco, g2, w1, w3, w2, gf)


def _post_a_s_kernel(x1_ref, conv_ref, ret_ref, wo_ref, gca_ref, wq_ref, x2_ref, q_ref):
    x2, q = _mix_out_q(x1_ref[...], conv_ref[...], ret_ref[...], wo_ref, gca_ref, wq_ref)
    x2_ref[...] = x2
    q_ref[...] = q


def _post_a_s(x1, conv, ret, layer, wo, gca, wq):
    row = lambda n: pl.BlockSpec((TM, n), lambda i: (i, 0))
    sq = lambda: _resident((D_MODEL, D_MODEL), layer)
    return pl.pallas_call(
        _post_a_s_kernel,
        grid=(T_S // TM,),
        in_specs=[row(D_MODEL), row(CONV_CH), row(RET_WIDTH), sq(), _resident((1, D_MODEL), layer), sq()],
        out_specs=[row(D_MODEL), row(D_MODEL)],
        out_shape=[jax.ShapeDtypeStruct((T_S, D_MODEL), F32)] * 2,
        name="mix_out_q_sample",
    )(x1, conv, ret, wo, gca, wq)


def _attn_s_kernel(q_ref, k_ref, v_ref, o_ref):
    nh = CA_HEADS * SUBLANES
    lane = lax.broadcasted_iota(jnp.int32, (nh, KV_ROWS), 1)
    row = lax.broadcasted_iota(jnp.int32, (nh, KV_ROWS), 0)
    valid = (lane % (2 * CA_HEADS)) == (row // SUBLANES)
    second = lax.broadcasted_iota(jnp.int32, (2 * nh, LANES), 0) % SUBLANES >= DEC_SEQ
    for pr in range(ATT_S_BT // PAIR):
        q = q_ref[pr * SUBLANES:(pr + 1) * SUBLANES, :]
        blocks = [q[:, (h * 2 + half) * LANES:(h * 2 + half + 1) * LANES]
                  for half in range(2) for h in range(CA_HEADS)]
        qt = jnp.concatenate(blocks, axis=0).astype(BF16)
        outs = []
        for x in range(PAIR):
            b = pr * PAIR + x
            st = _dot_nt(qt, k_ref[0, b].astype(BF16))
            s = (st[:nh] + pltpu.roll(st[nh:], KV_ROWS - CA_HEADS, 1)) * (CA_HEAD_DIM ** -0.5)
            p = _softmax_rows(jnp.where(valid, s, NEG_BIG))
            w = jnp.concatenate([p, pltpu.roll(p, CA_HEADS, 1)], axis=0).astype(BF16)
            outs.append(_dot(w, v_ref[0, b].astype(BF16)))
        o = jnp.where(second, outs[1], outs[0])
        for half in range(2):
            for h in range(CA_HEADS):
                r0 = (half * CA_HEADS + h) * SUBLANES
                c0 = (h * 2 + half) * LANES
                o_ref[pr * SUBLANES:(pr + 1) * SUBLANES, c0:c0 + LANES] = o[r0:r0 + SUBLANES]


def _flat_cache(c):
    c = c.reshape(DEPTH, DEC_BATCH, N_MEM, CA_HEADS, 2, LANES)
    return c.transpose(0, 1, 2, 4, 3, 5).reshape(DEPTH, DEC_BATCH, KV_ROWS, LANES)


def _unflat_cache(f):
    f = f.reshape(DEPTH, BATCH, N_MEM, 2, CA_HEADS, LANES)
    return f.transpose(0, 1, 2, 4, 3, 5).reshape(DEPTH, BATCH, N_MEM, CA_HEADS, CA_HEAD_DIM)


def _attn_s(q, cache_k, cache_v, layer):
    bt = ATT_S_BT
    qs = lambda: pl.BlockSpec((bt * DEC_SEQ, D_MODEL), lambda i: (i, 0))
    kv = lambda: pl.BlockSpec((1, bt, KV_ROWS, LANES), lambda i: (layer, i, 0, 0))
    return pl.pallas_call(
        _attn_s_kernel,
        grid=(DEC_BATCH // bt,),
        in_specs=[qs(), kv(), kv()],
        out_specs=qs(),
        out_shape=jax.ShapeDtypeStruct((T_S, D_MODEL), F32),
        compiler_params=pltpu.CompilerParams(vmem_limit_bytes=VMEM_LIMIT),
        name="cross_attn_sample",
    )(q, cache_k, cache_v)


def _post_b_s_kernel(x2_ref, o_ref, wco_ref, g2_ref, w1_ref, w3_ref, w2_ref, gf_ref, y_ref, *, final):
    x3 = x2_ref[...] + _dot(o_ref[...].astype(BF16), wco_ref[...])
    x4 = _ffn(x3, g2_ref, w1_ref, w3_ref, w2_ref)
    y_ref[...] = _rms(x4, gf_ref[...]) if final else x4


def _post_b_s(x2, o, layer, wco, g2, w1, w3, w2, gf, final):
    row = lambda: pl.BlockSpec((TM, D_MODEL), lambda i: (i, 0))
    return pl.pallas_call(
        functools.partial(_post_b_s_kernel, final=final),
        grid=(T_S // TM,),
        in_specs=[row(), row(), _resident((D_MODEL, D_MODEL), layer), _resident((1, D_MODEL), layer),
                  _resident((D_MODEL, D_FF), layer), _resident((D_MODEL, D_FF), layer),
                  _resident((D_FF, D_MODEL), layer), pl.BlockSpec((1, D_MODEL), lambda i: (0, 0))],
        out_specs=row(),
        out_shape=jax.ShapeDtypeStruct((T_S, D_MODEL), F32),
        compiler_params=pltpu.CompilerParams(vmem_limit_bytes=VMEM_LIMIT),
        name="attn_out_ffn_sample",
    )(x2, o, wco, g2, w1, w3, w2, gf)


def _rope_tables(pos):
    inv_freq = ROPE_BASE ** (-jnp.arange(0, RET_DK, 2, dtype=F32) / RET_DK)
    ang = pos[:, None] * inv_freq[None, :]
    cos, sin = jnp.cos(ang), jnp.sin(ang)
    return jnp.concatenate([cos, cos], axis=-1), jnp.concatenate([-sin, sin], axis=-1)


def _decay_tables(c):
    log_gamma = jnp.log1p(-jnp.exp2(-5.0 - jnp.arange(RET_HEADS, dtype=F32)))
    idx = jnp.arange(c, dtype=F32)
    rel = idx[:, None] - idx[None, :]
    decay = jnp.where(rel[None] >= 0,
                      jnp.exp(log_gamma[:, None, None] * jnp.maximum(rel, 0.0)[None]), 0.0)
    q_dec = jnp.exp(log_gamma[:, None] * (idx[None, :] + 1.0))
    k_dec = jnp.exp(log_gamma[:, None] * (c - 1.0 - idx[None, :]))
    chunk_dec = jnp.exp(log_gamma * c)
    return decay, q_dec, k_dec, chunk_dec


def _prompt_tables():
    cos2, sin2 = _rope_tables(jnp.arange(SEQ, dtype=F32))
    decay, q_dec, k_dec, chunk_dec = _decay_tables(RET_CHUNK)
    lanes = (RET_HEADS, RET_CHUNK, RET_DK)
    return (cos2, sin2, decay,
            jnp.broadcast_to(q_dec[:, :, None], lanes),
            jnp.broadcast_to(k_dec[:, :, None], lanes),
            jnp.broadcast_to(chunk_dec[:, None, None], lanes))


def _sample_tables():
    slab = lambda a: jnp.concatenate([a] * PAIR, axis=-2)
    cos2, sin2 = _rope_tables(PAST_LEN + jnp.arange(DEC_SEQ, dtype=F32))
    decay, q_dec, k_dec, chunk_dec = _decay_tables(DEC_SEQ)
    rows = (RET_HEADS, DEC_SEQ, RET_DK)
    dec = jnp.broadcast_to(jnp.swapaxes(decay, 1, 2)[:, :, :, None], (RET_HEADS, DEC_SEQ, DEC_SEQ, RET_DK))
    return (slab(cos2), slab(sin2), slab(dec),
            slab(jnp.broadcast_to(q_dec[:, :, None], rows)),
            slab(jnp.broadcast_to(k_dec[:, :, None], rows)),
            jnp.broadcast_to(chunk_dec[:, None, None], (RET_HEADS, RET_DK, RET_DV)))


def kernel(x_prompt, x_sample, state_conv, state_ret, cache_mem_k, cache_mem_v, mem_prompt, g_ffn1, w1_ffn1, w3_ffn1, w2_ffn1, g_mix, w_in, conv_w, conv_b, conv_ln_g, conv_ln_b, ret_gn_g, w_out, g_ca, g_mem, w_cq, w_ck, w_cv, w_co, g_ffn2, w1_ffn2, w3_ffn2, w2_ffn2, g_final):
    bf = lambda w: w.astype(BF16)
    vec = lambda g: g.reshape(DEPTH, 1, -1)
    w1a, w3a, w2a, w1b, w3b, w2b = map(bf, (w1_ffn1, w3_ffn1, w2_ffn1, w1_ffn2, w3_ffn2, w2_ffn2))
    win, wout, wcq, wck, wcv, wco = map(bf, (w_in, w_out, w_cq, w_ck, w_cv, w_co))
    g1, gm, gca, gmem, g2 = map(vec, (g_ffn1, g_mix, g_ca, g_mem, g_ffn2))
    cb, clg, clb, gn = map(vec, (conv_b, conv_ln_g, conv_ln_b, ret_gn_g))
    gf = g_final.reshape(1, D_MODEL)
    cache_k = _flat_cache(cache_mem_k)
    cache_v = _flat_cache(cache_mem_v)
    mem = mem_prompt.reshape(BATCH * N_MEM, D_MODEL)
    tabs_p = _prompt_tables()
    tabs_s = _sample_tables()
    cw8 = conv_w.reshape(DEPTH, CONV_K, CONV_LT, LANES).transpose(0, 2, 1, 3)
    cw8 = jnp.broadcast_to(cw8[:, :, :, None, :], (DEPTH, CONV_LT, CONV_K, SUBLANES, LANES))

    state_t = jnp.swapaxes(state_conv, 1, 2)

    xp = x_prompt.reshape(T_P, D_MODEL)
    xs = x_sample.reshape(T_S, D_MODEL)
    conv_p = jnp.zeros((DEPTH, BATCH, CONV_K - 1, CONV_CH), F32)
    ret_p = jnp.zeros((DEPTH, BATCH, RET_HEADS, RET_DK, RET_DV), F32)
    mem_p = tuple(jnp.zeros((DEPTH, BATCH * KV_ROWS, LANES), F32) for _ in range(2))
    conv_s = jnp.zeros((DEPTH, CONV_K - 1, DEC_BATCH, CONV_CH), F32)
    ret_s = jnp.zeros((DEPTH, DEC_BATCH, RET_HEADS, RET_DK, RET_DV), F32)
    for l in range(DEPTH):
        final = l == DEPTH - 1
        memk_f, memv_f, mk, mv = _memkv(mem, l, gmem, wck, wcv, mem_p)
        mem_p = (memk_f, memv_f)
        x1, u, r = _ffn_inproj(xp, l, g1, w1a, w3a, w2a, gm, win)
        cv, conv_p = _conv_p(u, l, cw8, cb, clg, clb, conv_p)
        rt, ret_p = _ret_p(r, l, tabs_p, gn, ret_p)
        xp = _post_p(x1, cv, rt, mk, mv, l, wout, gca, wcq, wco, g2, w1b, w3b, w2b, gf, final)

        x1, u, r = _ffn_inproj(xs, l, g1, w1a, w3a, w2a, gm, win)
        ut = jnp.swapaxes(u.reshape(DEC_BATCH, DEC_SEQ, CONV_CH), 0, 1)
        cv, conv_s = _conv_s(state_t, ut, l, conv_w, cb, clg, clb, conv_s)
        cv = jnp.swapaxes(cv, 0, 1).reshape(T_S, CONV_CH)
        rt, ret_s = _ret_s(r, state_ret, l, tabs_s, gn, ret_s)
        x2, q = _post_a_s(x1, cv, rt, l, wout, gca, wcq)
        o = _attn_s(q, cache_k, cache_v, l)
        xs = _post_b_s(x2, o, l, wco, g2, w1b, w3b, w2b, gf, final)

    return (xp.reshape(BATCH, SEQ, D_MODEL), xs.reshape(DEC_BATCH, DEC_SEQ, D_MODEL),
            conv_p, ret_p, _unflat_cache(mem_p[0]), _unflat_cache(mem_p[1]),
            jnp.swapaxes(conv_s, 1, 2), ret_s)
```

```python
import functools

import jax
import jax.numpy as jnp
from jax import lax
from jax.experimental import pallas as pl
from jax.experimental.pallas import tpu as pltpu

F32 = jnp.float32
BF16 = jnp.bfloat16

D_MODEL = 1024
BATCH = 8
SEQ = 2048
DEPTH = 2
DEC_BATCH = 128
DEC_SEQ = 4
PAST_LEN = 16384
CONV_CH = 512
CONV_K = 31
RET_HEADS = 4
RET_DK = 128
RET_DV = 128
RET_WIDTH = 512
MIX_IN = 3072
RET_CHUNK = 128
ROPE_BASE = 10000.0
D_FF = 4096
N_MEM = 256
CA_HEADS = 4
CA_HEAD_DIM = 256
EPS = 1e-6
GN_EPS = 1e-5

T_P = BATCH * SEQ
T_S = DEC_BATCH * DEC_SEQ

TM = 512
TF = 1024
CONV_TT = 512
CONV_RC = 64
CONV_HALO = 32
SUBLANES = 8
LANES = 128
CONV_ROWS = CONV_HALO + CONV_TT
CONV_LT = CONV_CH // LANES
CONV_NORM_UNROLL = 4
RET_CPS = 4
RET_PB = 2
PAIR = SUBLANES // DEC_SEQ
RET_S_BT = 8
ATT_S_BT = 4
KV_ROWS = N_MEM * 2 * CA_HEADS
NEG_BIG = -1e30
CONV_S_BT = 32
STAGE_BYTES = 1024 * 1024
VMEM_LIMIT = 60 * 1024 * 1024


def _dot(a, b):
    return jnp.dot(a, b, preferred_element_type=F32)


def _dot_nt(a, b):
    return lax.dot_general(a, b, (((1,), (1,)), ((), ())), preferred_element_type=F32)


def _rms(x, g):
    return x * lax.rsqrt(jnp.mean(x * x, axis=-1, keepdims=True) + EPS) * g


def _silu(x):
    return x * jax.nn.sigmoid(x)


def _ffn(x, g_ref, w1_ref, w3_ref, w2_ref):
    xn = _rms(x, g_ref[...]).astype(BF16)
    acc = None
    for c in range(D_FF // TF):
        sl = slice(c * TF, (c + 1) * TF)
        h1 = _dot(xn, w1_ref[:, sl])
        h3 = _dot(xn, w3_ref[:, sl])
        a = (_silu(h1) * h3).astype(BF16)
        part = _dot(a, w2_ref[sl, :])
        acc = part if acc is None else acc + part
    return x + 0.5 * acc


def _softmax_rows(s):
    m = jnp.max(s, axis=-1, keepdims=True)
    e = jnp.exp(s - m)
    return e * (1.0 / jnp.sum(e, axis=-1, keepdims=True))


def _group_norm_gate(o, gate, gn):
    mu = jnp.mean(o, axis=-1, keepdims=True)
    d = o - mu
    var = jnp.mean(d * d, axis=-1, keepdims=True)
    return _silu(gate) * (d * lax.rsqrt(var + GN_EPS) * gn)


def _rotary(t, cos2, sin2):
    return t * cos2 + pltpu.roll(t, RET_DK // 2, 1) * sin2


def _ffn_inproj_kernel(x_ref, g1_ref, w1_ref, w3_ref, w2_ref, gm_ref, win_ref,
                       x1_ref, u_ref, r_ref):
    x1 = _ffn(x_ref[...], g1_ref, w1_ref, w3_ref, w2_ref)
    x1_ref[...] = x1
    h = _rms(x1, gm_ref[...]).astype(BF16)
    proj = _dot(h, win_ref[...])
    u_ref[...] = proj[:, :CONV_CH] * jax.nn.sigmoid(proj[:, CONV_CH:2 * CONV_CH])
    r_ref[...] = proj[:, 2 * CONV_CH:]


def _resident(shape, layer):
    nd = len(shape)
    return pl.BlockSpec((None,) + tuple(shape), lambda *_: (layer,) + (0,) * nd,
                        pipeline_mode=pl.Buffered(1))


def _stream_cast(w_hbm, layer, dst_ref):
    rows, cols = dst_ref.shape
    ch = 1 << ((STAGE_BYTES // (4 * cols)).bit_length() - 1)
    assert ch % SUBLANES == 0 and rows % ch == 0
    n = rows // ch

    def run(stage, sem):
        def chunk_copy(k):
            slot = lax.rem(k, 2)
            return pltpu.make_async_copy(w_hbm.at[layer, pl.ds(k * ch, ch), :], stage.at[slot], sem.at[slot])

        chunk_copy(0).start()

        def body(k, carry):
            @pl.when(k + 1 < n)
            def _():
                chunk_copy(k + 1).start()

            chunk_copy(k).wait()
            dst_ref[pl.ds(pl.multiple_of(k * ch, ch), ch), :] = stage[lax.rem(k, 2)].astype(BF16)
            return carry

        lax.fori_loop(0, n, body, 0)

    pl.run_scoped(run, pltpu.VMEM((2, ch, cols), F32), pltpu.SemaphoreType.DMA((2,)))


def _weights_call(body, layer, weights, **kw):
    idx = sorted(weights)
    in_specs = list(kw.pop("in_specs"))
    for i in idx:
        in_specs[i] = pl.BlockSpec(memory_space=pl.ANY)
    scratch = list(kw.pop("scratch_shapes", ())) + [pltpu.VMEM(weights[i], BF16) for i in idx]

    def with_weights(*refs):
        refs, slabs = list(refs[:len(refs) - len(idx)]), refs[len(refs) - len(idx):]

        @pl.when(pl.program_id(0) == 0)
        def _():
            for i, slab in zip(idx, slabs):
                _stream_cast(refs[i], layer, slab)

        for i, slab in zip(idx, slabs):
            refs[i] = slab
        return body(*refs)

    return pl.pallas_call(with_weights, in_specs=in_specs, scratch_shapes=scratch, **kw)


def _layered_call(body, n_in, prev, make_call=pl.pallas_call, **kw):
    idx = sorted(prev)
    kw["in_specs"] = list(kw["in_specs"]) + [pl.BlockSpec(memory_space=pl.ANY)] * len(idx)

    def with_aliased(*refs):
        return body(*refs[:n_in], *refs[n_in + len(idx):])

    call = make_call(with_aliased, input_output_aliases={n_in + k: o for k, o in enumerate(idx)}, **kw)
    return lambda *args: call(*args, *[prev[o] for o in idx])


def _ffn_inproj(x, layer, g1, w1, w3, w2, gm, win):
    t = x.shape[0]
    row = lambda n: pl.BlockSpec((TM, n), lambda i: (i, 0))
    return _weights_call(
        _ffn_inproj_kernel, layer,
        {2: (D_MODEL, D_FF), 3: (D_MODEL, D_FF), 4: (D_FF, D_MODEL), 6: (D_MODEL, MIX_IN)},
        grid=(t // TM,),
        in_specs=[row(D_MODEL), _resident((1, D_MODEL), layer), None, None, None,
                  _resident((1, D_MODEL), layer), None],
        out_specs=[row(D_MODEL), row(CONV_CH), row(MIX_IN - 2 * CONV_CH)],
        out_shape=[jax.ShapeDtypeStruct((t, D_MODEL), F32),
                   jax.ShapeDtypeStruct((t, CONV_CH), F32),
                   jax.ShapeDtypeStruct((t, MIX_IN - 2 * CONV_CH), F32)],
        compiler_params=pltpu.CompilerParams(vmem_limit_bytes=VMEM_LIMIT),
        name="ffn_inproj",
    )(x, g1, w1, w3, w2, gm, win)


def _layer_norm_silu(y, lg, lb):
    mu = jnp.mean(y, axis=-1, keepdims=True)
    d = y - mu
    var = jnp.mean(d * d, axis=-1, keepdims=True)
    return _silu(d * lax.rsqrt(var + EPS) * lg + lb)


def _conv_p_kernel(u_ref, w_ref, b_ref, lg_ref, lb_ref, y_ref, nb_ref, win_ref, acc_ref):
    j = pl.program_id(1)
    for t in range(CONV_LT):
        @pl.when(j == 0)
        def _():
            win_ref[0, t, 0:CONV_HALO, :] = jnp.zeros((CONV_HALO, LANES), F32)

        @pl.when(j > 0)
        def _():
            win_ref[0, t, 0:CONV_HALO, :] = win_ref[0, t, CONV_TT:CONV_TT + CONV_HALO, :]

        win_ref[0, t, CONV_HALO:CONV_ROWS, :] = u_ref[:, t * LANES:(t + 1) * LANES]
        win_ref[0, t, CONV_ROWS:CONV_ROWS + SUBLANES, :] = jnp.zeros((SUBLANES, LANES), F32)
        for sh in range(1, SUBLANES):
            win_ref[sh, t, 0:CONV_ROWS, :] = win_ref[0, t, sh:sh + CONV_ROWS, :]

    first = CONV_HALO - (CONV_K - 1)
    n_rc = CONV_TT // CONV_RC

    def taps(idx, carry):
        t = idx // n_rc
        r0 = pl.multiple_of((idx % n_rc) * CONV_RC, CONV_RC)
        accs = [None] * (CONV_RC // SUBLANES)
        for k in range(CONV_K):
            sh = (first + k) % SUBLANES
            wk = w_ref[t, k]
            for i in range(CONV_RC // SUBLANES):
                a0 = pl.multiple_of(r0 + (first + k - sh) + i * SUBLANES, SUBLANES)
                term = wk * win_ref[sh, t, pl.ds(a0, SUBLANES), :]
                accs[i] = term if accs[i] is None else accs[i] + term
        acc_ref[t, pl.ds(r0, CONV_RC), :] = jnp.concatenate(accs, axis=0)
        return carry

    lax.fori_loop(0, CONV_LT * n_rc, taps, 0)

    def norm(c, carry):
        r0 = pl.multiple_of(c * CONV_RC, CONV_RC)
        y = jnp.concatenate([acc_ref[t, pl.ds(r0, CONV_RC), :] for t in range(CONV_LT)], axis=-1) + b_ref[...]
        y_ref[pl.ds(r0, CONV_RC), :] = _layer_norm_silu(y, lg_ref[...], lb_ref[...])
        return carry

    lax.fori_loop(0, n_rc, norm, 0, unroll=CONV_NORM_UNROLL)

    @pl.when(j == pl.num_programs(1) - 1)
    def _():
        nb_ref[0, 0] = u_ref[CONV_TT - (CONV_K - 1):CONV_TT, :]


def _conv_p(u, layer, conv_w, conv_b, ln_g, ln_b, prev_nb):
    nt = SEQ // CONV_TT
    vec = lambda: pl.BlockSpec((None, 1, CONV_CH), lambda b, j: (layer, 0, 0))
    return _layered_call(
        _conv_p_kernel, 5, {1: prev_nb},
        grid=(BATCH, nt),
        in_specs=[pl.BlockSpec((CONV_TT, CONV_CH), lambda b, j: (b * nt + j, 0)),
                  pl.BlockSpec((None, CONV_LT, CONV_K, SUBLANES, LANES), lambda b, j: (layer, 0, 0, 0, 0)),
                  vec(), vec(), vec()],
        out_specs=[pl.BlockSpec((CONV_TT, CONV_CH), lambda b, j: (b * nt + j, 0)),
                   pl.BlockSpec((1, 1, CONV_K - 1, CONV_CH), lambda b, j: (layer, b, 0, 0))],
        out_shape=[jax.ShapeDtypeStruct((T_P, CONV_CH), F32),
                   jax.ShapeDtypeStruct((DEPTH, BATCH, CONV_K - 1, CONV_CH), F32)],
        scratch_shapes=[pltpu.VMEM((SUBLANES, CONV_LT, CONV_ROWS + SUBLANES, LANES), F32),
                        pltpu.VMEM((CONV_LT, CONV_TT, LANES), F32)],
        name="conv_prompt",
    )(u, conv_w, conv_b, ln_g, ln_b)


def _conv_s_kernel(st_ref, ut_ref, w_ref, b_ref, lg_ref, lb_ref, y_ref, nb_ref):
    keep = CONV_K - 1 - DEC_SEQ
    for t in range(DEC_SEQ):
        y = None
        for j in range(t, CONV_K - 1):
            term = st_ref[0, j] * w_ref[j - t:j - t + 1, :]
            y = term if y is None else y + term
        for j in range(t + 1):
            k = CONV_K - 1 - t + j
            y = y + ut_ref[j] * w_ref[k:k + 1, :]
        y_ref[t] = _layer_norm_silu(y + b_ref[...], lg_ref[...], lb_ref[...])
    for j in range(keep):
        nb_ref[0, j] = st_ref[0, j + DEC_SEQ]
    for t in range(DEC_SEQ):
        nb_ref[0, keep + t] = ut_ref[t]


def _conv_s(state_t, ut, layer, conv_w, conv_b, ln_g, ln_b, prev_nb):
    bt = CONV_S_BT
    vec = lambda: pl.BlockSpec((None, 1, CONV_CH), lambda i: (layer, 0, 0))
    return _layered_call(
        _conv_s_kernel, 6, {1: prev_nb},
        grid=(DEC_BATCH // bt,),
        in_specs=[pl.BlockSpec((1, CONV_K - 1, bt, CONV_CH), lambda i: (layer, 0, i, 0)),
                  pl.BlockSpec((DEC_SEQ, bt, CONV_CH), lambda i: (0, i, 0)),
                  pl.BlockSpec((None, CONV_K, CONV_CH), lambda i: (layer, 0, 0)),
                  vec(), vec(), vec()],
        out_specs=[pl.BlockSpec((DEC_SEQ, bt, CONV_CH), lambda i: (0, i, 0)),
                   pl.BlockSpec((1, CONV_K - 1, bt, CONV_CH), lambda i: (layer, 0, i, 0))],
        out_shape=[jax.ShapeDtypeStruct((DEC_SEQ, DEC_BATCH, CONV_CH), F32),
                   jax.ShapeDtypeStruct((DEPTH, CONV_K - 1, DEC_BATCH, CONV_CH), F32)],
        name="conv_sample",
    )(state_t, ut, conv_w, conv_b, ln_g, ln_b)


def _ret_p_kernel(q_ref, k_ref, v_ref, g_ref, cos_ref, sin_ref, dec_ref, qd_ref, kd_ref,
                  cd_ref, gn_ref, o_ref, ns_ref, s_ref):
    j = pl.program_id(1)

    @pl.when(j == 0)
    def _():
        s_ref[...] = jnp.zeros_like(s_ref)

    for c in range(RET_CPS):
        rows = slice(c * RET_CHUNK, (c + 1) * RET_CHUNK)
        cos2 = cos_ref[rows, :]
        sin2 = sin_ref[rows, :]
        for h in range(RET_HEADS):
            sl = slice(h * RET_DK, (h + 1) * RET_DK)
            for e in range(RET_PB):
                qh = _rotary(q_ref[e, rows, sl], cos2, sin2)
                kh = _rotary(k_ref[e, rows, sl], cos2, sin2) * (RET_DK ** -0.5)
                qb = qh.astype(BF16)
                vb = v_ref[e, rows, sl].astype(BF16)
                s_h = s_ref[e, h]
                sc = _dot_nt(qb, kh.astype(BF16)) * dec_ref[h]
                o = _dot(sc.astype(BF16), vb) + _dot(qb, s_h.astype(BF16)) * qd_ref[h]
                kdec = (kh * kd_ref[h]).astype(BF16)
                upd = lax.dot_general(kdec, vb, (((0,), (0,)), ((), ())), preferred_element_type=F32)
                s_ref[e, h] = s_h * cd_ref[h] + upd
                o_ref[e, rows, sl] = _group_norm_gate(o, g_ref[e, rows, sl], gn_ref[:, sl])

    @pl.when(j == pl.num_programs(1) - 1)
    def _():
        ns_ref[0] = s_ref[...]


def _ret_p(r, layer, tabs, gn, prev_ns):
    cos2, sin2, dec, qd, kd, cd = tabs
    rows = RET_CPS * RET_CHUNK
    r3 = r.reshape(BATCH, SEQ, r.shape[-1])
    col = lambda c: pl.BlockSpec((RET_PB, rows, RET_WIDTH), lambda b, j: (b, j, c))
    const = lambda: pl.BlockSpec((RET_HEADS, RET_CHUNK, RET_DK), lambda b, j: (0, 0, 0))
    tab = lambda: pl.BlockSpec((rows, RET_DK), lambda b, j: (j, 0))
    out, ns = _layered_call(
        _ret_p_kernel, 11, {1: prev_ns},
        grid=(BATCH // RET_PB, SEQ // rows),
        in_specs=[col(0), col(1), col(2), col(3), tab(), tab(), const(), const(), const(), const(),
                  pl.BlockSpec((None, 1, RET_WIDTH), lambda b, j: (layer, 0, 0))],
        out_specs=[pl.BlockSpec((RET_PB, rows, RET_WIDTH), lambda b, j: (b, j, 0)),
                   pl.BlockSpec((1, RET_PB, RET_HEADS, RET_DK, RET_DV), lambda b, j: (layer, b, 0, 0, 0))],
        out_shape=[jax.ShapeDtypeStruct((BATCH, SEQ, RET_WIDTH), F32),
                   jax.ShapeDtypeStruct((DEPTH, BATCH, RET_HEADS, RET_DK, RET_DV), F32)],
        scratch_shapes=[pltpu.VMEM((RET_PB, RET_HEADS, RET_DK, RET_DV), F32)],
        name="retention_prompt",
    )(r3, r3, r3, r3, cos2, sin2, dec, qd, kd, cd, gn)
    return out.reshape(T_P, RET_WIDTH), ns


def _ret_s_kernel(q_ref, k_ref, v_ref, g_ref, st_ref, cos_ref, sin_ref, dec_ref, qd_ref, kd_ref,
                  cd_ref, gn_ref, o_ref, ns_ref):
    cos2 = cos_ref[...]
    sin2 = sin_ref[...]
    zpad = jnp.zeros((RET_DK - SUBLANES, RET_DK), F32)
    second = lax.broadcasted_iota(jnp.int32, (SUBLANES, RET_DK), 0) >= DEC_SEQ

    def pick(a, jj):
        return jnp.where(second, a[DEC_SEQ + jj:DEC_SEQ + jj + 1, :], a[jj:jj + 1, :])

    def body(p, carry):
        rows = pl.ds(pl.multiple_of(p * SUBLANES, SUBLANES), SUBLANES)
        for h in range(RET_HEADS):
            sl = slice(h * RET_DK, (h + 1) * RET_DK)
            qh = _rotary(q_ref[rows, sl], cos2, sin2)
            kh = _rotary(k_ref[rows, sl], cos2, sin2) * (RET_DK ** -0.5)
            vh = v_ref[rows, sl]
            qb = qh.astype(BF16)
            kdec = kh * kd_ref[h]
            vpad = jnp.concatenate([vh, zpad], axis=0).astype(BF16)
            from_state = []
            for x in range(PAIR):
                s_x = st_ref[0, p * PAIR + x, h]
                from_state.append(_dot(qb, s_x.astype(BF16)))
                mine = second if x else jnp.logical_not(second)
                kx = jnp.concatenate([jnp.where(mine, kdec, 0.0), zpad], axis=0)
                upd = _dot(kx.T.astype(BF16), vpad)
                ns_ref[0, p * PAIR + x, h] = s_x * cd_ref[h] + upd
            o = jnp.where(second, from_state[1], from_state[0]) * qd_ref[h]
            for jj in range(DEC_SEQ):
                sj = jnp.sum(qh * pick(kh, jj), axis=-1, keepdims=True)
                o = o + (sj * dec_ref[h, jj]) * pick(vh, jj)
            o_ref[rows, sl] = _group_norm_gate(o, g_ref[rows, sl], gn_ref[:, sl])
        return carry

    lax.fori_loop(0, RET_S_BT // PAIR, body, 0, unroll=2)


def _ret_s(r, state_ret, layer, tabs, gn, prev_ns):
    cos2, sin2, dec, qd, kd, cd = tabs
    bt = RET_S_BT
    col = lambda c: pl.BlockSpec((bt * DEC_SEQ, RET_WIDTH), lambda i: (i, c))
    full = lambda a: pl.BlockSpec(a.shape, lambda i: (0,) * a.ndim)
    st = lambda: pl.BlockSpec((1, bt, RET_HEADS, RET_DK, RET_DV), lambda i: (layer, i, 0, 0, 0))
    return _layered_call(
        _ret_s_kernel, 12, {1: prev_ns},
        grid=(DEC_BATCH // bt,),
        in_specs=[col(0), col(1), col(2), col(3), st(), full(cos2), full(sin2), full(dec), full(qd),
                  full(kd), full(cd), pl.BlockSpec((None, 1, RET_WIDTH), lambda i: (layer, 0, 0))],
        out_specs=[pl.BlockSpec((bt * DEC_SEQ, RET_WIDTH), lambda i: (i, 0)), st()],
        out_shape=[jax.ShapeDtypeStruct((T_S, RET_WIDTH), F32),
                   jax.ShapeDtypeStruct((DEPTH, DEC_BATCH, RET_HEADS, RET_DK, RET_DV), F32)],
        name="retention_sample",
    )(r, r, r, r, state_ret, cos2, sin2, dec, qd, kd, cd, gn)


def _memkv_kernel(m_ref, g_ref, wk_ref, wv_ref, kf_ref, vf_ref, kb_ref, vb_ref):
    mn = _rms(m_ref[...], g_ref[...]).astype(BF16)
    for w_ref, flat_ref, b_ref in ((wk_ref, kf_ref, kb_ref), (wv_ref, vf_ref, vb_ref)):
        y = _dot(mn, w_ref[...])
        b_ref[...] = y.astype(BF16)
        for h in range(CA_HEADS):
            for half in range(2):
                c0 = (h * 2 + half) * LANES
                flat_ref[0, pl.ds(half * CA_HEADS + h, N_MEM, stride=2 * CA_HEADS), :] = y[:, c0:c0 + LANES]


def _memkv(mem, layer, g, wk, wv, prev):
    t = mem.shape[0]
    row = lambda: pl.BlockSpec((N_MEM, D_MODEL), lambda i: (i, 0))
    flat = lambda: pl.BlockSpec((1, KV_ROWS, LANES), lambda i: (layer, i, 0))
    flat_shape = jax.ShapeDtypeStruct((DEPTH, BATCH * KV_ROWS, LANES), F32)
    square = (D_MODEL, D_MODEL)
    return _layered_call(
        _memkv_kernel, 4, {0: prev[0], 1: prev[1]},
        make_call=functools.partial(_weights_call, layer=layer, weights={2: square, 3: square}),
        grid=(t // N_MEM,),
        in_specs=[row(), _resident((1, D_MODEL), layer), None, None],
        out_specs=[flat(), flat(), row(), row()],
        out_shape=[flat_shape, flat_shape, jax.ShapeDtypeStruct((t, D_MODEL), BF16),
                   jax.ShapeDtypeStruct((t, D_MODEL), BF16)],
        name="memory_kv",
    )(mem, g, wk, wv)


def _mix_out_q(x1, conv, ret, wo_ref, gca_ref, wq_ref):
    mix = jnp.concatenate([conv, ret], axis=-1).astype(BF16)
    x2 = x1 + _dot(mix, wo_ref[...])
    q = _dot(_rms(x2, gca_ref[...]).astype(BF16), wq_ref[...])
    return x2, q


def _post_p_kernel(x1_ref, conv_ref, ret_ref, mk_ref, mv_ref, wo_ref, gca_ref, wq_ref, wco_ref,
                   g2_ref, w1_ref, w3_ref, w2_ref, gf_ref, y_ref, *, final):
    x2, q = _mix_out_q(x1_ref[...], conv_ref[...], ret_ref[...], wo_ref, gca_ref, wq_ref)
    heads = []
    for h in range(CA_HEADS):
        sl = slice(h * CA_HEAD_DIM, (h + 1) * CA_HEAD_DIM)
        s = _dot_nt(q[:, sl].astype(BF16), mk_ref[:, sl]) * (CA_HEAD_DIM ** -0.5)
        p = _softmax_rows(s)
        heads.append(_dot(p.astype(BF16), mv_ref[:, sl]))
    o = jnp.concatenate(heads, axis=-1).astype(BF16)
    x3 = x2 + _dot(o, wco_ref[...])
    x4 = _ffn(x3, g2_ref, w1_ref, w3_ref, w2_ref)
    y_ref[...] = _rms(x4, gf_ref[...]) if final else x4


def _post_p(x1, conv, ret, mk, mv, layer, wo, gca, wq, wco, g2, w1, w3, w2, gf, final):
    row = lambda n: pl.BlockSpec((TM, n), lambda i: (i, 0))
    mem = lambda: pl.BlockSpec((N_MEM, D_MODEL), lambda i: (i // (SEQ // TM), 0))
    square = (D_MODEL, D_MODEL)
    return _weights_call(
        functools.partial(_post_p_kernel, final=final), layer,
        {5: square, 7: square, 8: square, 10: (D_MODEL, D_FF), 11: (D_MODEL, D_FF), 12: (D_FF, D_MODEL)},
        grid=(T_P // TM,),
        in_specs=[row(D_MODEL), row(CONV_CH), row(RET_WIDTH), mem(), mem(),
                  None, _resident((1, D_MODEL), layer), None, None, _resident((1, D_MODEL), layer),
                  None, None, None,
                  pl.BlockSpec((1, D_MODEL), lambda i: (0, 0))],
        out_specs=row(D_MODEL),
        out_shape=jax.ShapeDtypeStruct((T_P, D_MODEL), F32),
        compiler_params=pltpu.CompilerParams(vmem_limit_bytes=VMEM_LIMIT),
        name="post_prompt",
    )(x1, conv, ret, mk, mv, wo, gca, wq, wco, g2, w1, w3, w2, gf)


def _post_a_s_kernel(x1_ref, conv_ref, ret_ref, wo_ref, gca_ref, wq_ref, x2_ref, q_ref):
    x2, q = _mix_out_q(x1_ref[...], conv_ref[...], ret_ref[...], wo_ref, gca_ref, wq_ref)
    x2_ref[...] = x2
    q_ref[...] = q


def _post_a_s(x1, conv, ret, layer, wo, gca, wq):
    row = lambda n: pl.BlockSpec((TM, n), lambda i: (i, 0))
    square = (D_MODEL, D_MODEL)
    return _weights_call(
        _post_a_s_kernel, layer, {3: square, 5: square},
        grid=(T_S // TM,),
        in_specs=[row(D_MODEL), row(CONV_CH), row(RET_WIDTH), None, _resident((1, D_MODEL), layer), None],
        out_specs=[row(D_MODEL), row(D_MODEL)],
        out_shape=[jax.ShapeDtypeStruct((T_S, D_MODEL), F32)] * 2,
        name="mix_out_q_sample",
    )(x1, conv, ret, wo, gca, wq)


def _attn_s_kernel(q_ref, k_ref, v_ref, o_ref):
    nh = CA_HEADS * SUBLANES
    lane = lax.broadcasted_iota(jnp.int32, (nh, KV_ROWS), 1)
    row = lax.broadcasted_iota(jnp.int32, (nh, KV_ROWS), 0)
    valid = (lane % (2 * CA_HEADS)) == (row // SUBLANES)
    second = lax.broadcasted_iota(jnp.int32, (2 * nh, LANES), 0) % SUBLANES >= DEC_SEQ
    for pr in range(ATT_S_BT // PAIR):
        q = q_ref[pr * SUBLANES:(pr + 1) * SUBLANES, :]
        blocks = [q[:, (h * 2 + half) * LANES:(h * 2 + half + 1) * LANES]
                  for half in range(2) for h in range(CA_HEADS)]
        qt = jnp.concatenate(blocks, axis=0).astype(BF16)
        outs = []
        for x in range(PAIR):
            b = pr * PAIR + x
            st = _dot_nt(qt, k_ref[0, b].astype(BF16))
            s = (st[:nh] + pltpu.roll(st[nh:], KV_ROWS - CA_HEADS, 1)) * (CA_HEAD_DIM ** -0.5)
            p = _softmax_rows(jnp.where(valid, s, NEG_BIG))
            w = jnp.concatenate([p, pltpu.roll(p, CA_HEADS, 1)], axis=0).astype(BF16)
            outs.append(_dot(w, v_ref[0, b].astype(BF16)))
        o = jnp.where(second, outs[1], outs[0])
        for half in range(2):
            for h in range(CA_HEADS):
                r0 = (half * CA_HEADS + h) * SUBLANES
                c0 = (h * 2 + half) * LANES
                o_ref[pr * SUBLANES:(pr + 1) * SUBLANES, c0:c0 + LANES] = o[r0:r0 + SUBLANES]


def _flat_cache(c):
    c = c.reshape(DEPTH, DEC_BATCH, N_MEM, CA_HEADS, 2, LANES)
    return c.transpose(0, 1, 2, 4, 3, 5).reshape(DEPTH, DEC_BATCH, KV_ROWS, LANES)


def _unflat_cache(f):
    f = f.reshape(DEPTH, BATCH, N_MEM, 2, CA_HEADS, LANES)
    return f.transpose(0, 1, 2, 4, 3, 5).reshape(DEPTH, BATCH, N_MEM, CA_HEADS, CA_HEAD_DIM)


def _attn_s(q, cache_k, cache_v, layer):
    bt = ATT_S_BT
    qs = lambda: pl.BlockSpec((bt * DEC_SEQ, D_MODEL), lambda i: (i, 0))
    kv = lambda: pl.BlockSpec((1, bt, KV_ROWS, LANES), lambda i: (layer, i, 0, 0))
    return pl.pallas_call(
        _attn_s_kernel,
        grid=(DEC_BATCH // bt,),
        in_specs=[qs(), kv(), kv()],
        out_specs=qs(),
        out_shape=jax.ShapeDtypeStruct((T_S, D_MODEL), F32),
        compiler_params=pltpu.CompilerParams(vmem_limit_bytes=VMEM_LIMIT),
        name="cross_attn_sample",
    )(q, cache_k, cache_v)


def _post_b_s_kernel(x2_ref, o_ref, wco_ref, g2_ref, w1_ref, w3_ref, w2_ref, gf_ref, y_ref, *, final):
    x3 = x2_ref[...] + _dot(o_ref[...].astype(BF16), wco_ref[...])
    x4 = _ffn(x3, g2_ref, w1_ref, w3_ref, w2_ref)
    y_ref[...] = _rms(x4, gf_ref[...]) if final else x4


def _post_b_s(x2, o, layer, wco, g2, w1, w3, w2, gf, final):
    row = lambda: pl.BlockSpec((TM, D_MODEL), lambda i: (i, 0))
    return _weights_call(
        functools.partial(_post_b_s_kernel, final=final), layer,
        {2: (D_MODEL, D_MODEL), 4: (D_MODEL, D_FF), 5: (D_MODEL, D_FF), 6: (D_FF, D_MODEL)},
        grid=(T_S // TM,),
        in_specs=[row(), row(), None, _resident((1, D_MODEL), layer), None, None, None,
                  pl.BlockSpec((1, D_MODEL), lambda i: (0, 0))],
        out_specs=row(),
        out_shape=jax.ShapeDtypeStruct((T_S, D_MODEL), F32),
        compiler_params=pltpu.CompilerParams(vmem_limit_bytes=VMEM_LIMIT),
        name="attn_out_ffn_sample",
    )(x2, o, wco, g2, w1, w3, w2, gf)


def _rope_tables(pos):
    inv_freq = ROPE_BASE ** (-jnp.arange(0, RET_DK, 2, dtype=F32) / RET_DK)
    ang = pos[:, None] * inv_freq[None, :]
    cos, sin = jnp.cos(ang), jnp.sin(ang)
    return jnp.concatenate([cos, cos], axis=-1), jnp.concatenate([-sin, sin], axis=-1)


def _decay_tables(c):
    log_gamma = jnp.log1p(-jnp.exp2(-5.0 - jnp.arange(RET_HEADS, dtype=F32)))
    idx = jnp.arange(c, dtype=F32)
    rel = idx[:, None] - idx[None, :]
    decay = jnp.where(rel[None] >= 0,
                      jnp.exp(log_gamma[:, None, None] * jnp.maximum(rel, 0.0)[None]), 0.0)
    q_dec = jnp.exp(log_gamma[:, None] * (idx[None, :] + 1.0))
    k_dec = jnp.exp(log_gamma[:, None] * (c - 1.0 - idx[None, :]))
    chunk_dec = jnp.exp(log_gamma * c)
    return decay, q_dec, k_dec, chunk_dec


def _prompt_tables():
    cos2, sin2 = _rope_tables(jnp.arange(SEQ, dtype=F32))
    decay, q_dec, k_dec, chunk_dec = _decay_tables(RET_CHUNK)
    lanes = (RET_HEADS, RET_CHUNK, RET_DK)
    return (cos2, sin2, decay,
            jnp.broadcast_to(q_dec[:, :, None], lanes),
            jnp.broadcast_to(k_dec[:, :, None], lanes),
            jnp.broadcast_to(chunk_dec[:, None, None], lanes))


def _sample_tables():
    slab = lambda a: jnp.concatenate([a] * PAIR, axis=-2)
    cos2, sin2 = _rope_tables(PAST_LEN + jnp.arange(DEC_SEQ, dtype=F32))
    decay, q_dec, k_dec, chunk_dec = _decay_tables(DEC_SEQ)
    rows = (RET_HEADS, DEC_SEQ, RET_DK)
    dec = jnp.broadcast_to(jnp.swapaxes(decay, 1, 2)[:, :, :, None], (RET_HEADS, DEC_SEQ, DEC_SEQ, RET_DK))
    return (slab(cos2), slab(sin2), slab(dec),
            slab(jnp.broadcast_to(q_dec[:, :, None], rows)),
            slab(jnp.broadcast_to(k_dec[:, :, None], rows)),
            jnp.broadcast_to(chunk_dec[:, None, None], (RET_HEADS, RET_DK, RET_DV)))


def kernel(x_prompt, x_sample, state_conv, state_ret, cache_mem_k, cache_mem_v, mem_prompt, g_ffn1, w1_ffn1, w3_ffn1, w2_ffn1, g_mix, w_in, conv_w, conv_b, conv_ln_g, conv_ln_b, ret_gn_g, w_out, g_ca, g_mem, w_cq, w_ck, w_cv, w_co, g_ffn2, w1_ffn2, w3_ffn2, w2_ffn2, g_final):
    vec = lambda g: g.reshape(DEPTH, 1, -1)
    w1a, w3a, w2a, w1b, w3b, w2b = w1_ffn1, w3_ffn1, w2_ffn1, w1_ffn2, w3_ffn2, w2_ffn2
    win, wout, wcq, wck, wcv, wco = w_in, w_out, w_cq, w_ck, w_cv, w_co
    g1, gm, gca, gmem, g2 = map(vec, (g_ffn1, g_mix, g_ca, g_mem, g_ffn2))
    cb, clg, clb, gn = map(vec, (conv_b, conv_ln_g, conv_ln_b, ret_gn_g))
    gf = g_final.reshape(1, D_MODEL)
    cache_k = _flat_cache(cache_mem_k)
    cache_v = _flat_cache(cache_mem_v)
    mem = mem_prompt.reshape(BATCH * N_MEM, D_MODEL)
    tabs_p = _prompt_tables()
    tabs_s = _sample_tables()
    cw8 = conv_w.reshape(DEPTH, CONV_K, CONV_LT, LANES).transpose(0, 2, 1, 3)
    cw8 = jnp.broadcast_to(cw8[:, :, :, None, :], (DEPTH, CONV_LT, CONV_K, SUBLANES, LANES))

    state_t = jnp.swapaxes(state_conv, 1, 2)

    xp = x_prompt.reshape(T_P, D_MODEL)
    xs = x_sample.reshape(T_S, D_MODEL)
    conv_p = jnp.zeros((DEPTH, BATCH, CONV_K - 1, CONV_CH), F32)
    ret_p = jnp.zeros((DEPTH, BATCH, RET_HEADS, RET_DK, RET_DV), F32)
    mem_p = tuple(jnp.zeros((DEPTH, BATCH * KV_ROWS, LANES), F32) for _ in range(2))
    conv_s = jnp.zeros((DEPTH, CONV_K - 1, DEC_BATCH, CONV_CH), F32)
    ret_s = jnp.zeros((DEPTH, DEC_BATCH, RET_HEADS, RET_DK, RET_DV), F32)
    for l in range(DEPTH):
        final = l == DEPTH - 1
        memk_f, memv_f, mk, mv = _memkv(mem, l, gmem, wck, wcv, mem_p)
        mem_p = (memk_f, memv_f)
        x1, u, r = _ffn_inproj(xp, l, g1, w1a, w3a, w2a, gm, win)
        cv, conv_p = _conv_p(u, l, cw8, cb, clg, clb, conv_p)
        rt, ret_p = _ret_p(r, l, tabs_p, gn, ret_p)
        xp = _post_p(x1, cv, rt, mk, mv, l, wout, gca, wcq, wco, g2, w1b, w3b, w2b, gf, final)

        x1, u, r = _ffn_inproj(xs, l, g1, w1a, w3a, w2a, gm, win)
        ut = jnp.swapaxes(u.reshape(DEC_BATCH, DEC_SEQ, CONV_CH), 0, 1)
        cv, conv_s = _conv_s(state_t, ut, l, conv_w, cb, clg, clb, conv_s)
        cv = jnp.swapaxes(cv, 0, 1).reshape(T_S, CONV_CH)
        rt, ret_s = _ret_s(r, state_ret, l, tabs_s, gn, ret_s)
        x2, q = _post_a_s(x1, cv, rt, l, wout, gca, wcq)
        o = _attn_s(q, cache_k, cache_v, l)
        xs = _post_b_s(x2, o, l, wco, g2, w1b, w3b, w2b, gf, final)

    return (xp.reshape(BATCH, SEQ, D_MODEL), xs.reshape(DEC_BATCH, DEC_SEQ, D_MODEL),
            conv_p, ret_p, _unflat_cache(mem_p[0]), _unflat_cache(mem_p[1]),
            jnp.swapaxes(conv_s, 1, 2), ret_s)
```

```python
import functools

import jax
import jax.numpy as jnp
from jax import lax
from jax.experimental import pallas as pl
from jax.experimental.pallas import tpu as pltpu

F32 = jnp.float32
BF16 = jnp.bfloat16

D_MODEL = 1024
BATCH = 8
SEQ = 2048
DEPTH = 2
DEC_BATCH = 128
DEC_SEQ = 4
PAST_LEN = 16384
CONV_CH = 512
CONV_K = 31
RET_HEADS = 4
RET_DK = 128
RET_DV = 128
RET_WIDTH = 512
MIX_IN = 3072
RET_CHUNK = 128
ROPE_BASE = 10000.0
D_FF = 4096
N_MEM = 256
CA_HEADS = 4
CA_HEAD_DIM = 256
EPS = 1e-6
GN_EPS = 1e-5

T_P = BATCH * SEQ
T_S = DEC_BATCH * DEC_SEQ

TM = 512
TF = 1024
CONV_TT = 512
CONV_RC = 64
CONV_HALO = 32
SUBLANES = 8
LANES = 128
CONV_ROWS = CONV_HALO + CONV_TT
CONV_LT = CONV_CH // LANES
CONV_NORM_UNROLL = 4
RET_CPS = 4
RET_PB = 2
PAIR = SUBLANES // DEC_SEQ
RET_S_BT = 8
ATT_S_BT = 4
KV_ROWS = N_MEM * 2 * CA_HEADS
NEG_BIG = -1e30
CONV_S_BT = 32
STAGE_BYTES = 512 * 1024
STAGE_SLOTS = 8
VMEM_LIMIT = 60 * 1024 * 1024


def _dot(a, b):
    return jnp.dot(a, b, preferred_element_type=F32)


def _dot_nt(a, b):
    return lax.dot_general(a, b, (((1,), (1,)), ((), ())), preferred_element_type=F32)


def _rms(x, g):
    return x * lax.rsqrt(jnp.mean(x * x, axis=-1, keepdims=True) + EPS) * g


def _silu(x):
    return x * jax.nn.sigmoid(x)


def _ffn(x, g_ref, w1_ref, w3_ref, w2_ref):
    xn = _rms(x, g_ref[...]).astype(BF16)
    acc = None
    for c in range(D_FF // TF):
        sl = slice(c * TF, (c + 1) * TF)
        h1 = _dot(xn, w1_ref[:, sl])
        h3 = _dot(xn, w3_ref[:, sl])
        a = (_silu(h1) * h3).astype(BF16)
        part = _dot(a, w2_ref[sl, :])
        acc = part if acc is None else acc + part
    return x + 0.5 * acc


def _softmax_rows(s):
    m = jnp.max(s, axis=-1, keepdims=True)
    e = jnp.exp(s - m)
    return e * (1.0 / jnp.sum(e, axis=-1, keepdims=True))


def _group_norm_gate(o, gate, gn):
    mu = jnp.mean(o, axis=-1, keepdims=True)
    d = o - mu
    var = jnp.mean(d * d, axis=-1, keepdims=True)
    return _silu(gate) * (d * lax.rsqrt(var + GN_EPS) * gn)


def _rotary(t, cos2, sin2):
    return t * cos2 + pltpu.roll(t, RET_DK // 2, 1) * sin2


def _ffn_inproj_kernel(x_ref, g1_ref, w1_ref, w3_ref, w2_ref, gm_ref, win_ref,
                       x1_ref, u_ref, r_ref):
    x1 = _ffn(x_ref[...], g1_ref, w1_ref, w3_ref, w2_ref)
    x1_ref[...] = x1
    h = _rms(x1, gm_ref[...]).astype(BF16)
    proj = _dot(h, win_ref[...])
    u_ref[...] = proj[:, :CONV_CH] * jax.nn.sigmoid(proj[:, CONV_CH:2 * CONV_CH])
    r_ref[...] = proj[:, 2 * CONV_CH:]


def _resident(shape, layer):
    nd = len(shape)
    return pl.BlockSpec((None,) + tuple(shape), lambda *_: (layer,) + (0,) * nd,
                        pipeline_mode=pl.Buffered(1))


def _stream_cast(w_hbm, layer, dst_ref):
    rows, cols = dst_ref.shape
    ch = 1 << ((STAGE_BYTES // (4 * cols)).bit_length() - 1)
    assert ch % SUBLANES == 0 and rows % ch == 0
    n = rows // ch

    ahead = min(STAGE_SLOTS - 1, n)

    def run(stage, sem):
        def chunk_copy(k):
            slot = lax.rem(k, STAGE_SLOTS)
            return pltpu.make_async_copy(w_hbm.at[layer, pl.ds(k * ch, ch), :], stage.at[slot], sem.at[slot])

        for k in range(ahead):
            chunk_copy(k).start()

        def body(k, carry):
            @pl.when(k + ahead < n)
            def _():
                chunk_copy(k + ahead).start()

            chunk_copy(k).wait()
            dst_ref[pl.ds(pl.multiple_of(k * ch, ch), ch), :] = stage[lax.rem(k, STAGE_SLOTS)].astype(BF16)
            return carry

        lax.fori_loop(0, n, body, 0)

    pl.run_scoped(run, pltpu.VMEM((STAGE_SLOTS, ch, cols), F32), pltpu.SemaphoreType.DMA((STAGE_SLOTS,)))


def _weights_call(body, layer, weights, **kw):
    idx = sorted(weights)
    in_specs = list(kw.pop("in_specs"))
    for i in idx:
        in_specs[i] = pl.BlockSpec(memory_space=pl.ANY)
    scratch = list(kw.pop("scratch_shapes", ())) + [pltpu.VMEM(weights[i], BF16) for i in idx]

    def with_weights(*refs):
        refs, slabs = list(refs[:len(refs) - len(idx)]), refs[len(refs) - len(idx):]

        @pl.when(pl.program_id(0) == 0)
        def _():
            for i, slab in zip(idx, slabs):
                _stream_cast(refs[i], layer, slab)

        for i, slab in zip(idx, slabs):
            refs[i] = slab
        return body(*refs)

    return pl.pallas_call(with_weights, in_specs=in_specs, scratch_shapes=scratch, **kw)


def _layered_call(body, n_in, prev, make_call=pl.pallas_call, **kw):
    idx = sorted(prev)
    kw["in_specs"] = list(kw["in_specs"]) + [pl.BlockSpec(memory_space=pl.ANY)] * len(idx)

    def with_aliased(*refs):
        return body(*refs[:n_in], *refs[n_in + len(idx):])

    call = make_call(with_aliased, input_output_aliases={n_in + k: o for k, o in enumerate(idx)}, **kw)
    return lambda *args: call(*args, *[prev[o] for o in idx])


def _ffn_inproj(x, layer, g1, w1, w3, w2, gm, win):
    t = x.shape[0]
    row = lambda n: pl.BlockSpec((TM, n), lambda i: (i, 0))
    return _weights_call(
        _ffn_inproj_kernel, layer,
        {2: (D_MODEL, D_FF), 3: (D_MODEL, D_FF), 4: (D_FF, D_MODEL), 6: (D_MODEL, MIX_IN)},
        grid=(t // TM,),
        in_specs=[row(D_MODEL), _resident((1, D_MODEL), layer), None, None, None,
                  _resident((1, D_MODEL), layer), None],
        out_specs=[row(D_MODEL), row(CONV_CH), row(MIX_IN - 2 * CONV_CH)],
        out_shape=[jax.ShapeDtypeStruct((t, D_MODEL), F32),
                   jax.ShapeDtypeStruct((t, CONV_CH), F32),
                   jax.ShapeDtypeStruct((t, MIX_IN - 2 * CONV_CH), F32)],
        compiler_params=pltpu.CompilerParams(vmem_limit_bytes=VMEM_LIMIT),
        name="ffn_inproj",
    )(x, g1, w1, w3, w2, gm, win)


def _layer_norm_silu(y, lg, lb):
    mu = jnp.mean(y, axis=-1, keepdims=True)
    d = y - mu
    var = jnp.mean(d * d, axis=-1, keepdims=True)
    return _silu(d * lax.rsqrt(var + EPS) * lg + lb)


def _conv_p_kernel(u_ref, w_ref, b_ref, lg_ref, lb_ref, y_ref, nb_ref, win_ref, acc_ref):
    j = pl.program_id(1)
    for t in range(CONV_LT):
        @pl.when(j == 0)
        def _():
            win_ref[0, t, 0:CONV_HALO, :] = jnp.zeros((CONV_HALO, LANES), F32)

        @pl.when(j > 0)
        def _():
            win_ref[0, t, 0:CONV_HALO, :] = win_ref[0, t, CONV_TT:CONV_TT + CONV_HALO, :]

        win_ref[0, t, CONV_HALO:CONV_ROWS, :] = u_ref[:, t * LANES:(t + 1) * LANES]
        win_ref[0, t, CONV_ROWS:CONV_ROWS + SUBLANES, :] = jnp.zeros((SUBLANES, LANES), F32)
        for sh in range(1, SUBLANES):
            win_ref[sh, t, 0:CONV_ROWS, :] = win_ref[0, t, sh:sh + CONV_ROWS, :]

    first = CONV_HALO - (CONV_K - 1)
    n_rc = CONV_TT // CONV_RC

    def taps(idx, carry):
        t = idx // n_rc
        r0 = pl.multiple_of((idx % n_rc) * CONV_RC, CONV_RC)
        accs = [None] * (CONV_RC // SUBLANES)
        for k in range(CONV_K):
            sh = (first + k) % SUBLANES
            wk = w_ref[t, k]
            for i in range(CONV_RC // SUBLANES):
                a0 = pl.multiple_of(r0 + (first + k - sh) + i * SUBLANES, SUBLANES)
                term = wk * win_ref[sh, t, pl.ds(a0, SUBLANES), :]
                accs[i] = term if accs[i] is None else accs[i] + term
        acc_ref[t, pl.ds(r0, CONV_RC), :] = jnp.concatenate(accs, axis=0)
        return carry

    lax.fori_loop(0, CONV_LT * n_rc, taps, 0)

    def norm(c, carry):
        r0 = pl.multiple_of(c * CONV_RC, CONV_RC)
        y = jnp.concatenate([acc_ref[t, pl.ds(r0, CONV_RC), :] for t in range(CONV_LT)], axis=-1) + b_ref[...]
        y_ref[pl.ds(r0, CONV_RC), :] = _layer_norm_silu(y, lg_ref[...], lb_ref[...])
        return carry

    lax.fori_loop(0, n_rc, norm, 0, unroll=CONV_NORM_UNROLL)

    @pl.when(j == pl.num_programs(1) - 1)
    def _():
        nb_ref[0, 0] = u_ref[CONV_TT - (CONV_K - 1):CONV_TT, :]


def _conv_p(u, layer, conv_w, conv_b, ln_g, ln_b, prev_nb):
    nt = SEQ // CONV_TT
    vec = lambda: pl.BlockSpec((None, 1, CONV_CH), lambda b, j: (layer, 0, 0))
    return _layered_call(
        _conv_p_kernel, 5, {1: prev_nb},
        grid=(BATCH, nt),
        in_specs=[pl.BlockSpec((CONV_TT, CONV_CH), lambda b, j: (b * nt + j, 0)),
                  pl.BlockSpec((None, CONV_LT, CONV_K, SUBLANES, LANES), lambda b, j: (layer, 0, 0, 0, 0)),
                  vec(), vec(), vec()],
        out_specs=[pl.BlockSpec((CONV_TT, CONV_CH), lambda b, j: (b * nt + j, 0)),
                   pl.BlockSpec((1, 1, CONV_K - 1, CONV_CH), lambda b, j: (layer, b, 0, 0))],
        out_shape=[jax.ShapeDtypeStruct((T_P, CONV_CH), F32),
                   jax.ShapeDtypeStruct((DEPTH, BATCH, CONV_K - 1, CONV_CH), F32)],
        scratch_shapes=[pltpu.VMEM((SUBLANES, CONV_LT, CONV_ROWS + SUBLANES, LANES), F32),
                        pltpu.VMEM((CONV_LT, CONV_TT, LANES), F32)],
        name="conv_prompt",
    )(u, conv_w, conv_b, ln_g, ln_b)


def _conv_s_kernel(st_ref, ut_ref, w_ref, b_ref, lg_ref, lb_ref, y_ref, nb_ref):
    keep = CONV_K - 1 - DEC_SEQ
    for t in range(DEC_SEQ):
        y = None
        for j in range(t, CONV_K - 1):
            term = st_ref[0, j] * w_ref[j - t:j - t + 1, :]
            y = term if y is None else y + term
        for j in range(t + 1):
            k = CONV_K - 1 - t + j
            y = y + ut_ref[j] * w_ref[k:k + 1, :]
        y_ref[t] = _layer_norm_silu(y + b_ref[...], lg_ref[...], lb_ref[...])
    for j in range(keep):
        nb_ref[0, j] = st_ref[0, j + DEC_SEQ]
    for t in range(DEC_SEQ):
        nb_ref[0, keep + t] = ut_ref[t]


def _conv_s(state_t, ut, layer, conv_w, conv_b, ln_g, ln_b, prev_nb):
    bt = CONV_S_BT
    vec = lambda: pl.BlockSpec((None, 1, CONV_CH), lambda i: (layer, 0, 0))
    return _layered_call(
        _conv_s_kernel, 6, {1: prev_nb},
        grid=(DEC_BATCH // bt,),
        in_specs=[pl.BlockSpec((1, CONV_K - 1, bt, CONV_CH), lambda i: (layer, 0, i, 0)),
                  pl.BlockSpec((DEC_SEQ, bt, CONV_CH), lambda i: (0, i, 0)),
                  pl.BlockSpec((None, CONV_K, CONV_CH), lambda i: (layer, 0, 0)),
                  vec(), vec(), vec()],
        out_specs=[pl.BlockSpec((DEC_SEQ, bt, CONV_CH), lambda i: (0, i, 0)),
                   pl.BlockSpec((1, CONV_K - 1, bt, CONV_CH), lambda i: (layer, 0, i, 0))],
        out_shape=[jax.ShapeDtypeStruct((DEC_SEQ, DEC_BATCH, CONV_CH), F32),
                   jax.ShapeDtypeStruct((DEPTH, CONV_K - 1, DEC_BATCH, CONV_CH), F32)],
        name="conv_sample",
    )(state_t, ut, conv_w, conv_b, ln_g, ln_b)


def _ret_p_kernel(q_ref, k_ref, v_ref, g_ref, cos_ref, sin_ref, dec_ref, qd_ref, kd_ref,
                  cd_ref, gn_ref, o_ref, ns_ref, s_ref):
    j = pl.program_id(1)

    @pl.when(j == 0)
    def _():
        s_ref[...] = jnp.zeros_like(s_ref)

    for c in range(RET_CPS):
        rows = slice(c * RET_CHUNK, (c + 1) * RET_CHUNK)
        cos2 = cos_ref[rows, :]
        sin2 = sin_ref[rows, :]
        for h in range(RET_HEADS):
            sl = slice(h * RET_DK, (h + 1) * RET_DK)
            for e in range(RET_PB):
                qh = _rotary(q_ref[e, rows, sl], cos2, sin2)
                kh = _rotary(k_ref[e, rows, sl], cos2, sin2) * (RET_DK ** -0.5)
                qb = qh.astype(BF16)
                vb = v_ref[e, rows, sl].astype(BF16)
                s_h = s_ref[e, h]
                sc = _dot_nt(qb, kh.astype(BF16)) * dec_ref[h]
                o = _dot(sc.astype(BF16), vb) + _dot(qb, s_h.astype(BF16)) * qd_ref[h]
                kdec = (kh * kd_ref[h]).astype(BF16)
                upd = lax.dot_general(kdec, vb, (((0,), (0,)), ((), ())), preferred_element_type=F32)
                s_ref[e, h] = s_h * cd_ref[h] + upd
                o_ref[e, rows, sl] = _group_norm_gate(o, g_ref[e, rows, sl], gn_ref[:, sl])

    @pl.when(j == pl.num_programs(1) - 1)
    def _():
        ns_ref[0] = s_ref[...]


def _ret_p(r, layer, tabs, gn, prev_ns):
    cos2, sin2, dec, qd, kd, cd = tabs
    rows = RET_CPS * RET_CHUNK
    r3 = r.reshape(BATCH, SEQ, r.shape[-1])
    col = lambda c: pl.BlockSpec((RET_PB, rows, RET_WIDTH), lambda b, j: (b, j, c))
    const = lambda: pl.BlockSpec((RET_HEADS, RET_CHUNK, RET_DK), lambda b, j: (0, 0, 0))
    tab = lambda: pl.BlockSpec((rows, RET_DK), lambda b, j: (j, 0))
    out, ns = _layered_call(
        _ret_p_kernel, 11, {1: prev_ns},
        grid=(BATCH // RET_PB, SEQ // rows),
        in_specs=[col(0), col(1), col(2), col(3), tab(), tab(), const(), const(), const(), const(),
                  pl.BlockSpec((None, 1, RET_WIDTH), lambda b, j: (layer, 0, 0))],
        out_specs=[pl.BlockSpec((RET_PB, rows, RET_WIDTH), lambda b, j: (b, j, 0)),
                   pl.BlockSpec((1, RET_PB, RET_HEADS, RET_DK, RET_DV), lambda b, j: (layer, b, 0, 0, 0))],
        out_shape=[jax.ShapeDtypeStruct((BATCH, SEQ, RET_WIDTH), F32),
                   jax.ShapeDtypeStruct((DEPTH, BATCH, RET_HEADS, RET_DK, RET_DV), F32)],
        scratch_shapes=[pltpu.VMEM((RET_PB, RET_HEADS, RET_DK, RET_DV), F32)],
        name="retention_prompt",
    )(r3, r3, r3, r3, cos2, sin2, dec, qd, kd, cd, gn)
    return out.reshape(T_P, RET_WIDTH), ns


def _ret_s_kernel(q_ref, k_ref, v_ref, g_ref, st_ref, cos_ref, sin_ref, dec_ref, qd_ref, kd_ref,
                  cd_ref, gn_ref, o_ref, ns_ref):
    cos2 = cos_ref[...]
    sin2 = sin_ref[...]
    zpad = jnp.zeros((RET_DK - SUBLANES, RET_DK), F32)
    second = lax.broadcasted_iota(jnp.int32, (SUBLANES, RET_DK), 0) >= DEC_SEQ

    def pick(a, jj):
        return jnp.where(second, a[DEC_SEQ + jj:DEC_SEQ + jj + 1, :], a[jj:jj + 1, :])

    def body(p, carry):
        rows = pl.ds(pl.multiple_of(p * SUBLANES, SUBLANES), SUBLANES)
        for h in range(RET_HEADS):
            sl = slice(h * RET_DK, (h + 1) * RET_DK)
            qh = _rotary(q_ref[rows, sl], cos2, sin2)
            kh = _rotary(k_ref[rows, sl], cos2, sin2) * (RET_DK ** -0.5)
            vh = v_ref[rows, sl]
            qb = qh.astype(BF16)
            kdec = kh * kd_ref[h]
            vpad = jnp.concatenate([vh, zpad], axis=0).astype(BF16)
            from_state = []
            for x in range(PAIR):
                s_x = st_ref[0, p * PAIR + x, h]
                from_state.append(_dot(qb, s_x.astype(BF16)))
                mine = second if x else jnp.logical_not(second)
                kx = jnp.concatenate([jnp.where(mine, kdec, 0.0), zpad], axis=0)
                upd = _dot(kx.T.astype(BF16), vpad)
                ns_ref[0, p * PAIR + x, h] = s_x * cd_ref[h] + upd
            o = jnp.where(second, from_state[1], from_state[0]) * qd_ref[h]
            for jj in range(DEC_SEQ):
                sj = jnp.sum(qh * pick(kh, jj), axis=-1, keepdims=True)
                o = o + (sj * dec_ref[h, jj]) * pick(vh, jj)
            o_ref[rows, sl] = _group_norm_gate(o, g_ref[rows, sl], gn_ref[:, sl])
        return carry

    lax.fori_loop(0, RET_S_BT // PAIR, body, 0, unroll=2)


def _ret_s(r, state_ret, layer, tabs, gn, prev_ns):
    cos2, sin2, dec, qd, kd, cd = tabs
    bt = RET_S_BT
    col = lambda c: pl.BlockSpec((bt * DEC_SEQ, RET_WIDTH), lambda i: (i, c))
    full = lambda a: pl.BlockSpec(a.shape, lambda i: (0,) * a.ndim)
    st = lambda: pl.BlockSpec((1, bt, RET_HEADS, RET_DK, RET_DV), lambda i: (layer, i, 0, 0, 0))
    return _layered_call(
        _ret_s_kernel, 12, {1: prev_ns},
        grid=(DEC_BATCH // bt,),
        in_specs=[col(0), col(1), col(2), col(3), st(), full(cos2), full(sin2), full(dec), full(qd),
                  full(kd), full(cd), pl.BlockSpec((None, 1, RET_WIDTH), lambda i: (layer, 0, 0))],
        out_specs=[pl.BlockSpec((bt * DEC_SEQ, RET_WIDTH), lambda i: (i, 0)), st()],
        out_shape=[jax.ShapeDtypeStruct((T_S, RET_WIDTH), F32),
                   jax.ShapeDtypeStruct((DEPTH, DEC_BATCH, RET_HEADS, RET_DK, RET_DV), F32)],
        name="retention_sample",
    )(r, r, r, r, state_ret, cos2, sin2, dec, qd, kd, cd, gn)


def _memkv_kernel(m_ref, g_ref, wk_ref, wv_ref, kf_ref, vf_ref, kb_ref, vb_ref):
    mn = _rms(m_ref[...], g_ref[...]).astype(BF16)
    for w_ref, flat_ref, b_ref in ((wk_ref, kf_ref, kb_ref), (wv_ref, vf_ref, vb_ref)):
        y = _dot(mn, w_ref[...])
        b_ref[...] = y.astype(BF16)
        for h in range(CA_HEADS):
            for half in range(2):
                c0 = (h * 2 + half) * LANES
                flat_ref[0, pl.ds(half * CA_HEADS + h, N_MEM, stride=2 * CA_HEADS), :] = y[:, c0:c0 + LANES]


def _memkv(mem, layer, g, wk, wv, prev):
    t = mem.shape[0]
    row = lambda: pl.BlockSpec((N_MEM, D_MODEL), lambda i: (i, 0))
    flat = lambda: pl.BlockSpec((1, KV_ROWS, LANES), lambda i: (layer, i, 0))
    flat_shape = jax.ShapeDtypeStruct((DEPTH, BATCH * KV_ROWS, LANES), F32)
    square = (D_MODEL, D_MODEL)
    return _layered_call(
        _memkv_kernel, 4, {0: prev[0], 1: prev[1]},
        make_call=functools.partial(_weights_call, layer=layer, weights={2: square, 3: square}),
        grid=(t // N_MEM,),
        in_specs=[row(), _resident((1, D_MODEL), layer), None, None],
        out_specs=[flat(), flat(), row(), row()],
        out_shape=[flat_shape, flat_shape, jax.ShapeDtypeStruct((t, D_MODEL), BF16),
                   jax.ShapeDtypeStruct((t, D_MODEL), BF16)],
        name="memory_kv",
    )(mem, g, wk, wv)


def _mix_out_q(x1, conv, ret, wo_ref, gca_ref, wq_ref):
    mix = jnp.concatenate([conv, ret], axis=-1).astype(BF16)
    x2 = x1 + _dot(mix, wo_ref[...])
    q = _dot(_rms(x2, gca_ref[...]).astype(BF16), wq_ref[...])
    return x2, q


def _post_p_kernel(x1_ref, conv_ref, ret_ref, mk_ref, mv_ref, wo_ref, gca_ref, wq_ref, wco_ref,
                   g2_ref, w1_ref, w3_ref, w2_ref, gf_ref, y_ref, *, final):
    x2, q = _mix_out_q(x1_ref[...], conv_ref[...], ret_ref[...], wo_ref, gca_ref, wq_ref)
    heads = []
    for h in range(CA_HEADS):
        sl = slice(h * CA_HEAD_DIM, (h + 1) * CA_HEAD_DIM)
        s = _dot_nt(q[:, sl].astype(BF16), mk_ref[:, sl]) * (CA_HEAD_DIM ** -0.5)
        p = _softmax_rows(s)
        heads.append(_dot(p.astype(BF16), mv_ref[:, sl]))
    o = jnp.concatenate(heads, axis=-1).astype(BF16)
    x3 = x2 + _dot(o, wco_ref[...])
    x4 = _ffn(x3, g2_ref, w1_ref, w3_ref, w2_ref)
    y_ref[...] = _rms(x4, gf_ref[...]) if final else x4


def _post_p(x1, conv, ret, mk, mv, layer, wo, gca, wq, wco, g2, w1, w3, w2, gf, final):
    row = lambda n: pl.BlockSpec((TM, n), lambda i: (i, 0))
    mem = lambda: pl.BlockSpec((N_MEM, D_MODEL), lambda i: (i // (SEQ // TM), 0))
    square = (D_MODEL, D_MODEL)
    return _weights_call(
        functools.partial(_post_p_kernel, final=final), layer,
        {5: square, 7: square, 8: square, 10: (D_MODEL, D_FF), 11: (D_MODEL, D_FF), 12: (D_FF, D_MODEL)},
        grid=(T_P // TM,),
        in_specs=[row(D_MODEL), row(CONV_CH), row(RET_WIDTH), mem(), mem(),
                  None, _resident((1, D_MODEL), layer), None, None, _resident((1, D_MODEL), layer),
                  None, None, None,
                  pl.BlockSpec((1, D_MODEL), lambda i: (0, 0))],
        out_specs=row(D_MODEL),
        out_shape=jax.ShapeDtypeStruct((T_P, D_MODEL), F32),
        compiler_params=pltpu.CompilerParams(vmem_limit_bytes=VMEM_LIMIT),
        name="post_prompt",
    )(x1, conv, ret, mk, mv, wo, gca, wq, wco, g2, w1, w3, w2, gf)


def _post_a_s_kernel(x1_ref, conv_ref, ret_ref, wo_ref, gca_ref, wq_ref, x2_ref, q_ref):
    x2, q = _mix_out_q(x1_ref[...], conv_ref[...], ret_ref[...], wo_ref, gca_ref, wq_ref)
    x2_ref[...] = x2
    q_ref[...] = q


def _post_a_s(x1, conv, ret, layer, wo, gca, wq):
    row = lambda n: pl.BlockSpec((TM, n), lambda i: (i, 0))
    square = (D_MODEL, D_MODEL)
    return _weights_call(
        _post_a_s_kernel, layer, {3: square, 5: square},
        grid=(T_S // TM,),
        in_specs=[row(D_MODEL), row(CONV_CH), row(RET_WIDTH), None, _resident((1, D_MODEL), layer), None],
        out_specs=[row(D_MODEL), row(D_MODEL)],
        out_shape=[jax.ShapeDtypeStruct((T_S, D_MODEL), F32)] * 2,
        name="mix_out_q_sample",
    )(x1, conv, ret, wo, gca, wq)


def _attn_s_kernel(q_ref, k_ref, v_ref, o_ref):
    nh = CA_HEADS * SUBLANES
    lane = lax.broadcasted_iota(jnp.int32, (nh, KV_ROWS), 1)
    row = lax.broadcasted_iota(jnp.int32, (nh, KV_ROWS), 0)
    valid = (lane % (2 * CA_HEADS)) == (row // SUBLANES)
    second = lax.broadcasted_iota(jnp.int32, (2 * nh, LANES), 0) % SUBLANES >= DEC_SEQ
    for pr in range(ATT_S_BT // PAIR):
        q = q_ref[pr * SUBLANES:(pr + 1) * SUBLANES, :]
        blocks = [q[:, (h * 2 + half) * LANES:(h * 2 + half + 1) * LANES]
                  for half in range(2) for h in range(CA_HEADS)]
        qt = jnp.concatenate(blocks, axis=0).astype(BF16)
        outs = []
        for x in range(PAIR):
            b = pr * PAIR + x
            st = _dot_nt(qt, k_ref[0, b].astype(BF16))
            s = (st[:nh] + pltpu.roll(st[nh:], KV_ROWS - CA_HEADS, 1)) * (CA_HEAD_DIM ** -0.5)
            p = _softmax_rows(jnp.where(valid, s, NEG_BIG))
            w = jnp.concatenate([p, pltpu.roll(p, CA_HEADS, 1)], axis=0).astype(BF16)
            outs.append(_dot(w, v_ref[0, b].astype(BF16)))
        o = jnp.where(second, outs[1], outs[0])
        for half in range(2):
            for h in range(CA_HEADS):
                r0 = (half * CA_HEADS + h) * SUBLANES
                c0 = (h * 2 + half) * LANES
                o_ref[pr * SUBLANES:(pr + 1) * SUBLANES, c0:c0 + LANES] = o[r0:r0 + SUBLANES]


def _flat_cache(c):
    c = c.reshape(DEPTH, DEC_BATCH, N_MEM, CA_HEADS, 2, LANES)
    return c.transpose(0, 1, 2, 4, 3, 5).reshape(DEPTH, DEC_BATCH, KV_ROWS, LANES)


def _unflat_cache(f):
    f = f.reshape(DEPTH, BATCH, N_MEM, 2, CA_HEADS, LANES)
    return f.transpose(0, 1, 2, 4, 3, 5).reshape(DEPTH, BATCH, N_MEM, CA_HEADS, CA_HEAD_DIM)


def _attn_s(q, cache_k, cache_v, layer):
    bt = ATT_S_BT
    qs = lambda: pl.BlockSpec((bt * DEC_SEQ, D_MODEL), lambda i: (i, 0))
    kv = lambda: pl.BlockSpec((1, bt, KV_ROWS, LANES), lambda i: (layer, i, 0, 0))
    return pl.pallas_call(
        _attn_s_kernel,
        grid=(DEC_BATCH // bt,),
        in_specs=[qs(), kv(), kv()],
        out_specs=qs(),
        out_shape=jax.ShapeDtypeStruct((T_S, D_MODEL), F32),
        compiler_params=pltpu.CompilerParams(vmem_limit_bytes=VMEM_LIMIT),
        name="cross_attn_sample",
    )(q, cache_k, cache_v)


def _post_b_s_kernel(x2_ref, o_ref, wco_ref, g2_ref, w1_ref, w3_ref, w2_ref, gf_ref, y_ref, *, final):
    x3 = x2_ref[...] + _dot(o_ref[...].astype(BF16), wco_ref[...])
    x4 = _ffn(x3, g2_ref, w1_ref, w3_ref, w2_ref)
    y_ref[...] = _rms(x4, gf_ref[...]) if final else x4


def _post_b_s(x2, o, layer, wco, g2, w1, w3, w2, gf, final):
    row = lambda: pl.BlockSpec((TM, D_MODEL), lambda i: (i, 0))
    return _weights_call(
        functools.partial(_post_b_s_kernel, final=final), layer,
        {2: (D_MODEL, D_MODEL), 4: (D_MODEL, D_FF), 5: (D_MODEL, D_FF), 6: (D_FF, D_MODEL)},
        grid=(T_S // TM,),
        in_specs=[row(), row(), None, _resident((1, D_MODEL), layer), None, None, None,
                  pl.BlockSpec((1, D_MODEL), lambda i: (0, 0))],
        out_specs=row(),
        out_shape=jax.ShapeDtypeStruct((T_S, D_MODEL), F32),
        compiler_params=pltpu.CompilerParams(vmem_limit_bytes=VMEM_LIMIT),
        name="attn_out_ffn_sample",
    )(x2, o, wco, g2, w1, w3, w2, gf)


def _rope_tables(pos):
    inv_freq = ROPE_BASE ** (-jnp.arange(0, RET_DK, 2, dtype=F32) / RET_DK)
    ang = pos[:, None] * inv_freq[None, :]
    cos, sin = jnp.cos(ang), jnp.sin(ang)
    return jnp.concatenate([cos, cos], axis=-1), jnp.concatenate([-sin, sin], axis=-1)


def _decay_tables(c):
    log_gamma = jnp.log1p(-jnp.exp2(-5.0 - jnp.arange(RET_HEADS, dtype=F32)))
    idx = jnp.arange(c, dtype=F32)
    rel = idx[:, None] - idx[None, :]
    decay = jnp.where(rel[None] >= 0,
                      jnp.exp(log_gamma[:, None, None] * jnp.maximum(rel, 0.0)[None]), 0.0)
    q_dec = jnp.exp(log_gamma[:, None] * (idx[None, :] + 1.0))
    k_dec = jnp.exp(log_gamma[:, None] * (c - 1.0 - idx[None, :]))
    chunk_dec = jnp.exp(log_gamma * c)
    return decay, q_dec, k_dec, chunk_dec


def _prompt_tables():
    cos2, sin2 = _rope_tables(jnp.arange(SEQ, dtype=F32))
    decay, q_dec, k_dec, chunk_dec = _decay_tables(RET_CHUNK)
    lanes = (RET_HEADS, RET_CHUNK, RET_DK)
    return (cos2, sin2, decay,
            jnp.broadcast_to(q_dec[:, :, None], lanes),
            jnp.broadcast_to(k_dec[:, :, None], lanes),
            jnp.broadcast_to(chunk_dec[:, None, None], lanes))


def _sample_tables():
    slab = lambda a: jnp.concatenate([a] * PAIR, axis=-2)
    cos2, sin2 = _rope_tables(PAST_LEN + jnp.arange(DEC_SEQ, dtype=F32))
    decay, q_dec, k_dec, chunk_dec = _decay_tables(DEC_SEQ)
    rows = (RET_HEADS, DEC_SEQ, RET_DK)
    dec = jnp.broadcast_to(jnp.swapaxes(decay, 1, 2)[:, :, :, None], (RET_HEADS, DEC_SEQ, DEC_SEQ, RET_DK))
    return (slab(cos2), slab(sin2), slab(dec),
            slab(jnp.broadcast_to(q_dec[:, :, None], rows)),
            slab(jnp.broadcast_to(k_dec[:, :, None], rows)),
            jnp.broadcast_to(chunk_dec[:, None, None], (RET_HEADS, RET_DK, RET_DV)))


def kernel(x_prompt, x_sample, state_conv, state_ret, cache_mem_k, cache_mem_v, mem_prompt, g_ffn1, w1_ffn1, w3_ffn1, w2_ffn1, g_mix, w_in, conv_w, conv_b, conv_ln_g, conv_ln_b, ret_gn_g, w_out, g_ca, g_mem, w_cq, w_ck, w_cv, w_co, g_ffn2, w1_ffn2, w3_ffn2, w2_ffn2, g_final):
    vec = lambda g: g.reshape(DEPTH, 1, -1)
    w1a, w3a, w2a, w1b, w3b, w2b = w1_ffn1, w3_ffn1, w2_ffn1, w1_ffn2, w3_ffn2, w2_ffn2
    win, wout, wcq, wck, wcv, wco = w_in, w_out, w_cq, w_ck, w_cv, w_co
    g1, gm, gca, gmem, g2 = map(vec, (g_ffn1, g_mix, g_ca, g_mem, g_ffn2))
    cb, clg, clb, gn = map(vec, (conv_b, conv_ln_g, conv_ln_b, ret_gn_g))
    gf = g_final.reshape(1, D_MODEL)
    cache_k = _flat_cache(cache_mem_k)
    cache_v = _flat_cache(cache_mem_v)
    mem = mem_prompt.reshape(BATCH * N_MEM, D_MODEL)
    tabs_p = _prompt_tables()
    tabs_s = _sample_tables()
    cw8 = conv_w.reshape(DEPTH, CONV_K, CONV_LT, LANES).transpose(0, 2, 1, 3)
    cw8 = jnp.broadcast_to(cw8[:, :, :, None, :], (DEPTH, CONV_LT, CONV_K, SUBLANES, LANES))

    state_t = jnp.swapaxes(state_conv, 1, 2)

    xp = x_prompt.reshape(T_P, D_MODEL)
    xs = x_sample.reshape(T_S, D_MODEL)
    conv_p = jnp.zeros((DEPTH, BATCH, CONV_K - 1, CONV_CH), F32)
    ret_p = jnp.zeros((DEPTH, BATCH, RET_HEADS, RET_DK, RET_DV), F32)
    mem_p = tuple(jnp.zeros((DEPTH, BATCH * KV_ROWS, LANES), F32) for _ in range(2))
    conv_s = jnp.zeros((DEPTH, CONV_K - 1, DEC_BATCH, CONV_CH), F32)
    ret_s = jnp.zeros((DEPTH, DEC_BATCH, RET_HEADS, RET_DK, RET_DV), F32)
    for l in range(DEPTH):
        final = l == DEPTH - 1
        memk_f, memv_f, mk, mv = _memkv(mem, l, gmem, wck, wcv, mem_p)
        mem_p = (memk_f, memv_f)
        x1, u, r = _ffn_inproj(xp, l, g1, w1a, w3a, w2a, gm, win)
        cv, conv_p = _conv_p(u, l, cw8, cb, clg, clb, conv_p)
        rt, ret_p = _ret_p(r, l, tabs_p, gn, ret_p)
        xp = _post_p(x1, cv, rt, mk, mv, l, wout, gca, wcq, wco, g2, w1b, w3b, w2b, gf, final)

        x1, u, r = _ffn_inproj(xs, l, g1, w1a, w3a, w2a, gm, win)
        ut = jnp.swapaxes(u.reshape(DEC_BATCH, DEC_SEQ, CONV_CH), 0, 1)
        cv, conv_s = _conv_s(state_t, ut, l, conv_w, cb, clg, clb, conv_s)
        cv = jnp.swapaxes(cv, 0, 1).reshape(T_S, CONV_CH)
        rt, ret_s = _ret_s(r, state_ret, l, tabs_s, gn, ret_s)
        x2, q = _post_a_s(x1, cv, rt, l, wout, gca, wcq)
        o = _attn_s(q, cache_k, cache_v, l)
        xs = _post_b_s(x2, o, l, wco, g2, w1b, w3b, w2b, gf, final)

    return (xp.reshape(BATCH, SEQ, D_MODEL), xs.reshape(DEC_BATCH, DEC_SEQ, D_MODEL),
            conv_p, ret_p, _unflat_cache(mem_p[0]), _unflat_cache(mem_p[1]),
            jnp.swapaxes(conv_s, 1, 2), ret_s)
```

```python
import functools

import jax
import jax.numpy as jnp
from jax import lax
from jax.experimental import pallas as pl
from jax.experimental.pallas import tpu as pltpu

F32 = jnp.float32
BF16 = jnp.bfloat16

D_MODEL = 1024
BATCH = 8
SEQ = 2048
DEPTH = 2
DEC_BATCH = 128
DEC_SEQ = 4
PAST_LEN = 16384
CONV_CH = 512
CONV_K = 31
RET_HEADS = 4
RET_DK = 128
RET_DV = 128
RET_WIDTH = 512
MIX_IN = 3072
RET_CHUNK = 128
ROPE_BASE = 10000.0
D_FF = 4096
N_MEM = 256
CA_HEADS = 4
CA_HEAD_DIM = 256
EPS = 1e-6
GN_EPS = 1e-5

T_P = BATCH * SEQ
T_S = DEC_BATCH * DEC_SEQ

TM = 512
TF = 1024
CONV_TT = 512
CONV_RC = 64
CONV_HALO = 32
SUBLANES = 8
LANES = 128
CONV_ROWS = CONV_HALO + CONV_TT
CONV_LT = CONV_CH // LANES
CONV_NORM_UNROLL = 4
RET_CPS = 4
RET_PB = 2
PAIR = SUBLANES // DEC_SEQ
RET_S_BT = 8
ATT_S_BT = 4
KV_ROWS = N_MEM * 2 * CA_HEADS
NEG_BIG = -1e30
CONV_S_BT = 32
STAGE_BYTES = 512 * 1024
STAGE_SLOTS = 8
VMEM_LIMIT = 60 * 1024 * 1024


def _dot(a, b):
    return jnp.dot(a, b, preferred_element_type=F32)


def _dot_nt(a, b):
    return lax.dot_general(a, b, (((1,), (1,)), ((), ())), preferred_element_type=F32)


def _rms(x, g):
    return x * lax.rsqrt(jnp.mean(x * x, axis=-1, keepdims=True) + EPS) * g


def _silu(x):
    return x * jax.nn.sigmoid(x)


def _ffn(x, g_ref, w1_ref, w3_ref, w2_ref):
    xn = _rms(x, g_ref[...]).astype(BF16)
    acc = None
    for c in range(D_FF // TF):
        sl = slice(c * TF, (c + 1) * TF)
        h1 = _dot(xn, w1_ref[:, sl])
        h3 = _dot(xn, w3_ref[:, sl])
        a = (_silu(h1) * h3).astype(BF16)
        part = _dot(a, w2_ref[sl, :])
        acc = part if acc is None else acc + part
    return x + 0.5 * acc


def _softmax_rows(s):
    m = jnp.max(s, axis=-1, keepdims=True)
    e = jnp.exp(s - m)
    return e * (1.0 / jnp.sum(e, axis=-1, keepdims=True))


def _group_norm_gate(o, gate, gn):
    mu = jnp.mean(o, axis=-1, keepdims=True)
    d = o - mu
    var = jnp.mean(d * d, axis=-1, keepdims=True)
    return _silu(gate) * (d * lax.rsqrt(var + GN_EPS) * gn)


def _rotary(t, cos2, sin2):
    return t * cos2 + pltpu.roll(t, RET_DK // 2, 1) * sin2


def _ffn_inproj_kernel(x_ref, g1_ref, w1_ref, w3_ref, w2_ref, gm_ref, win_ref,
                       x1_ref, u_ref, r_ref):
    x1 = _ffn(x_ref[...], g1_ref, w1_ref, w3_ref, w2_ref)
    x1_ref[...] = x1
    h = _rms(x1, gm_ref[...]).astype(BF16)
    proj = _dot(h, win_ref[...])
    u_ref[...] = proj[:, :CONV_CH] * jax.nn.sigmoid(proj[:, CONV_CH:2 * CONV_CH])
    r_ref[...] = proj[:, 2 * CONV_CH:]


def _resident(shape, layer):
    nd = len(shape)
    return pl.BlockSpec((None,) + tuple(shape), lambda *_: (layer,) + (0,) * nd,
                        pipeline_mode=pl.Buffered(1))


def _whole(a):
    return pl.BlockSpec(a.shape, lambda *_: (0,) * a.ndim, pipeline_mode=pl.Buffered(1))


def _stream_cast(w_hbm, layer, dst_ref):
    rows, cols = dst_ref.shape
    ch = 1 << ((STAGE_BYTES // (4 * cols)).bit_length() - 1)
    assert ch % SUBLANES == 0 and rows % ch == 0
    n = rows // ch

    ahead = min(STAGE_SLOTS - 1, n)

    def run(stage, sem):
        def chunk_copy(k):
            slot = lax.rem(k, STAGE_SLOTS)
            return pltpu.make_async_copy(w_hbm.at[layer, pl.ds(k * ch, ch), :], stage.at[slot], sem.at[slot])

        for k in range(ahead):
            chunk_copy(k).start()

        def body(k, carry):
            @pl.when(k + ahead < n)
            def _():
                chunk_copy(k + ahead).start()

            chunk_copy(k).wait()
            dst_ref[pl.ds(pl.multiple_of(k * ch, ch), ch), :] = stage[lax.rem(k, STAGE_SLOTS)].astype(BF16)
            return carry

        lax.fori_loop(0, n, body, 0)

    pl.run_scoped(run, pltpu.VMEM((STAGE_SLOTS, ch, cols), F32), pltpu.SemaphoreType.DMA((STAGE_SLOTS,)))


def _weights_call(body, layer, weights, emit=False, **kw):
    idx = sorted(weights)
    nw = len(idx)
    in_specs = list(kw.pop("in_specs"))
    for i in idx:
        in_specs[i] = pl.BlockSpec(memory_space=pl.ANY)
    scratch = list(kw.pop("scratch_shapes", ())) + [pltpu.VMEM(weights[i], BF16) for i in idx]
    out_specs, out_shape = list(kw.pop("out_specs")), list(kw.pop("out_shape"))
    n_io = len(in_specs) + len(out_specs)
    if emit:
        out_specs += [pl.BlockSpec(memory_space=pl.ANY)] * nw
        out_shape += [jax.ShapeDtypeStruct(weights[i], BF16) for i in idx]
        scratch.append(pltpu.SemaphoreType.DMA((nw,)))

    def with_weights(*refs):
        refs = list(refs)
        sem = refs.pop() if emit else None
        slabs = refs[len(refs) - nw:]
        del refs[len(refs) - nw:]
        copies = []
        if emit:
            copies = [pltpu.make_async_copy(slab, out, sem.at[k])
                      for k, (slab, out) in enumerate(zip(slabs, refs[n_io:n_io + nw]))]
            del refs[n_io:n_io + nw]

        @pl.when(pl.program_id(0) == 0)
        def _():
            for k, (i, slab) in enumerate(zip(idx, slabs)):
                _stream_cast(refs[i], layer, slab)
                if emit:
                    copies[k].start()

        for i, slab in zip(idx, slabs):
            refs[i] = slab
        body(*refs)

        if emit:
            @pl.when(pl.program_id(0) == pl.num_programs(0) - 1)
            def _():
                for c in copies:
                    c.wait()

    return pl.pallas_call(with_weights, in_specs=in_specs, out_specs=out_specs, out_shape=out_shape,
                          scratch_shapes=scratch, **kw)


def _layered_call(body, n_in, prev, make_call=pl.pallas_call, **kw):
    idx = sorted(prev)
    kw["in_specs"] = list(kw["in_specs"]) + [pl.BlockSpec(memory_space=pl.ANY)] * len(idx)

    def with_aliased(*refs):
        return body(*refs[:n_in], *refs[n_in + len(idx):])

    call = make_call(with_aliased, input_output_aliases={n_in + k: o for k, o in enumerate(idx)}, **kw)
    return lambda *args: call(*args, *[prev[o] for o in idx])


def _ffn_inproj(x, layer, g1, gm, w, cast):
    t = x.shape[0]
    row = lambda n: pl.BlockSpec((TM, n), lambda i: (i, 0))
    kw = dict(
        grid=(t // TM,),
        out_specs=[row(D_MODEL), row(CONV_CH), row(MIX_IN - 2 * CONV_CH)],
        out_shape=[jax.ShapeDtypeStruct((t, D_MODEL), F32),
                   jax.ShapeDtypeStruct((t, CONV_CH), F32),
                   jax.ShapeDtypeStruct((t, MIX_IN - 2 * CONV_CH), F32)],
        compiler_params=pltpu.CompilerParams(vmem_limit_bytes=VMEM_LIMIT),
        name="ffn_inproj")
    vec = _resident((1, D_MODEL), layer)
    args = (x, g1, w[0], w[1], w[2], gm, w[3])
    if not cast:
        return pl.pallas_call(
            _ffn_inproj_kernel,
            in_specs=[row(D_MODEL), vec, _whole(w[0]), _whole(w[1]), _whole(w[2]), vec, _whole(w[3])], **kw)(*args)
    return _weights_call(
        _ffn_inproj_kernel, layer,
        {2: (D_MODEL, D_FF), 3: (D_MODEL, D_FF), 4: (D_FF, D_MODEL), 6: (D_MODEL, MIX_IN)}, emit=True,
        in_specs=[row(D_MODEL), vec, None, None, None, vec, None], **kw)(*args)


def _layer_norm_silu(y, lg, lb):
    mu = jnp.mean(y, axis=-1, keepdims=True)
    d = y - mu
    var = jnp.mean(d * d, axis=-1, keepdims=True)
    return _silu(d * lax.rsqrt(var + EPS) * lg + lb)


def _conv_p_kernel(u_ref, w_ref, b_ref, lg_ref, lb_ref, y_ref, nb_ref, win_ref, acc_ref):
    j = pl.program_id(1)
    for t in range(CONV_LT):
        @pl.when(j == 0)
        def _():
            win_ref[0, t, 0:CONV_HALO, :] = jnp.zeros((CONV_HALO, LANES), F32)

        @pl.when(j > 0)
        def _():
            win_ref[0, t, 0:CONV_HALO, :] = win_ref[0, t, CONV_TT:CONV_TT + CONV_HALO, :]

        win_ref[0, t, CONV_HALO:CONV_ROWS, :] = u_ref[:, t * LANES:(t + 1) * LANES]
        win_ref[0, t, CONV_ROWS:CONV_ROWS + SUBLANES, :] = jnp.zeros((SUBLANES, LANES), F32)
        for sh in range(1, SUBLANES):
            win_ref[sh, t, 0:CONV_ROWS, :] = win_ref[0, t, sh:sh + CONV_ROWS, :]

    first = CONV_HALO - (CONV_K - 1)
    n_rc = CONV_TT // CONV_RC

    def taps(idx, carry):
        t = idx // n_rc
        r0 = pl.multiple_of((idx % n_rc) * CONV_RC, CONV_RC)
        accs = [None] * (CONV_RC // SUBLANES)
        for k in range(CONV_K):
            sh = (first + k) % SUBLANES
            wk = w_ref[t, k]
            for i in range(CONV_RC // SUBLANES):
                a0 = pl.multiple_of(r0 + (first + k - sh) + i * SUBLANES, SUBLANES)
                term = wk * win_ref[sh, t, pl.ds(a0, SUBLANES), :]
                accs[i] = term if accs[i] is None else accs[i] + term
        acc_ref[t, pl.ds(r0, CONV_RC), :] = jnp.concatenate(accs, axis=0)
        return carry

    lax.fori_loop(0, CONV_LT * n_rc, taps, 0)

    def norm(c, carry):
        r0 = pl.multiple_of(c * CONV_RC, CONV_RC)
        y = jnp.concatenate([acc_ref[t, pl.ds(r0, CONV_RC), :] for t in range(CONV_LT)], axis=-1) + b_ref[...]
        y_ref[pl.ds(r0, CONV_RC), :] = _layer_norm_silu(y, lg_ref[...], lb_ref[...])
        return carry

    lax.fori_loop(0, n_rc, norm, 0, unroll=CONV_NORM_UNROLL)

    @pl.when(j == pl.num_programs(1) - 1)
    def _():
        nb_ref[0, 0] = u_ref[CONV_TT - (CONV_K - 1):CONV_TT, :]


def _conv_p(u, layer, conv_w, conv_b, ln_g, ln_b, prev_nb):
    nt = SEQ // CONV_TT
    vec = lambda: pl.BlockSpec((None, 1, CONV_CH), lambda b, j: (layer, 0, 0))
    return _layered_call(
        _conv_p_kernel, 5, {1: prev_nb},
        grid=(BATCH, nt),
        in_specs=[pl.BlockSpec((CONV_TT, CONV_CH), lambda b, j: (b * nt + j, 0)),
                  pl.BlockSpec((None, CONV_LT, CONV_K, SUBLANES, LANES), lambda b, j: (layer, 0, 0, 0, 0)),
                  vec(), vec(), vec()],
        out_specs=[pl.BlockSpec((CONV_TT, CONV_CH), lambda b, j: (b * nt + j, 0)),
                   pl.BlockSpec((1, 1, CONV_K - 1, CONV_CH), lambda b, j: (layer, b, 0, 0))],
        out_shape=[jax.ShapeDtypeStruct((T_P, CONV_CH), F32),
                   jax.ShapeDtypeStruct((DEPTH, BATCH, CONV_K - 1, CONV_CH), F32)],
        scratch_shapes=[pltpu.VMEM((SUBLANES, CONV_LT, CONV_ROWS + SUBLANES, LANES), F32),
                        pltpu.VMEM((CONV_LT, CONV_TT, LANES), F32)],
        name="conv_prompt",
    )(u, conv_w, conv_b, ln_g, ln_b)


def _conv_s_kernel(st_ref, ut_ref, w_ref, b_ref, lg_ref, lb_ref, y_ref, nb_ref):
    keep = CONV_K - 1 - DEC_SEQ
    for t in range(DEC_SEQ):
        y = None
        for j in range(t, CONV_K - 1):
            term = st_ref[0, j] * w_ref[j - t:j - t + 1, :]
            y = term if y is None else y + term
        for j in range(t + 1):
            k = CONV_K - 1 - t + j
            y = y + ut_ref[j] * w_ref[k:k + 1, :]
        y_ref[t] = _layer_norm_silu(y + b_ref[...], lg_ref[...], lb_ref[...])
    for j in range(keep):
        nb_ref[0, j] = st_ref[0, j + DEC_SEQ]
    for t in range(DEC_SEQ):
        nb_ref[0, keep + t] = ut_ref[t]


def _conv_s(state_t, ut, layer, conv_w, conv_b, ln_g, ln_b, prev_nb):
    bt = CONV_S_BT
    vec = lambda: pl.BlockSpec((None, 1, CONV_CH), lambda i: (layer, 0, 0))
    return _layered_call(
        _conv_s_kernel, 6, {1: prev_nb},
        grid=(DEC_BATCH // bt,),
        in_specs=[pl.BlockSpec((1, CONV_K - 1, bt, CONV_CH), lambda i: (layer, 0, i, 0)),
                  pl.BlockSpec((DEC_SEQ, bt, CONV_CH), lambda i: (0, i, 0)),
                  pl.BlockSpec((None, CONV_K, CONV_CH), lambda i: (layer, 0, 0)),
                  vec(), vec(), vec()],
        out_specs=[pl.BlockSpec((DEC_SEQ, bt, CONV_CH), lambda i: (0, i, 0)),
                   pl.BlockSpec((1, CONV_K - 1, bt, CONV_CH), lambda i: (layer, 0, i, 0))],
        out_shape=[jax.ShapeDtypeStruct((DEC_SEQ, DEC_BATCH, CONV_CH), F32),
                   jax.ShapeDtypeStruct((DEPTH, CONV_K - 1, DEC_BATCH, CONV_CH), F32)],
        name="conv_sample",
    )(state_t, ut, conv_w, conv_b, ln_g, ln_b)


def _ret_p_kernel(q_ref, k_ref, v_ref, g_ref, cos_ref, sin_ref, dec_ref, qd_ref, kd_ref,
                  cd_ref, gn_ref, o_ref, ns_ref, s_ref):
    j = pl.program_id(1)

    @pl.when(j == 0)
    def _():
        s_ref[...] = jnp.zeros_like(s_ref)

    for c in range(RET_CPS):
        rows = slice(c * RET_CHUNK, (c + 1) * RET_CHUNK)
        cos2 = cos_ref[rows, :]
        sin2 = sin_ref[rows, :]
        for h in range(RET_HEADS):
            sl = slice(h * RET_DK, (h + 1) * RET_DK)
            for e in range(RET_PB):
                qh = _rotary(q_ref[e, rows, sl], cos2, sin2)
                kh = _rotary(k_ref[e, rows, sl], cos2, sin2) * (RET_DK ** -0.5)
                qb = qh.astype(BF16)
                vb = v_ref[e, rows, sl].astype(BF16)
                s_h = s_ref[e, h]
                sc = _dot_nt(qb, kh.astype(BF16)) * dec_ref[h]
                o = _dot(sc.astype(BF16), vb) + _dot(qb, s_h.astype(BF16)) * qd_ref[h]
                kdec = (kh * kd_ref[h]).astype(BF16)
                upd = lax.dot_general(kdec, vb, (((0,), (0,)), ((), ())), preferred_element_type=F32)
                s_ref[e, h] = s_h * cd_ref[h] + upd
                o_ref[e, rows, sl] = _group_norm_gate(o, g_ref[e, rows, sl], gn_ref[:, sl])

    @pl.when(j == pl.num_programs(1) - 1)
    def _():
        ns_ref[0] = s_ref[...]


def _ret_p(r, layer, tabs, gn, prev_ns):
    cos2, sin2, dec, qd, kd, cd = tabs
    rows = RET_CPS * RET_CHUNK
    r3 = r.reshape(BATCH, SEQ, r.shape[-1])
    col = lambda c: pl.BlockSpec((RET_PB, rows, RET_WIDTH), lambda b, j: (b, j, c))
    const = lambda: pl.BlockSpec((RET_HEADS, RET_CHUNK, RET_DK), lambda b, j: (0, 0, 0))
    tab = lambda: pl.BlockSpec((rows, RET_DK), lambda b, j: (j, 0))
    out, ns = _layered_call(
        _ret_p_kernel, 11, {1: prev_ns},
        grid=(BATCH // RET_PB, SEQ // rows),
        in_specs=[col(0), col(1), col(2), col(3), tab(), tab(), const(), const(), const(), const(),
                  pl.BlockSpec((None, 1, RET_WIDTH), lambda b, j: (layer, 0, 0))],
        out_specs=[pl.BlockSpec((RET_PB, rows, RET_WIDTH), lambda b, j: (b, j, 0)),
                   pl.BlockSpec((1, RET_PB, RET_HEADS, RET_DK, RET_DV), lambda b, j: (layer, b, 0, 0, 0))],
        out_shape=[jax.ShapeDtypeStruct((BATCH, SEQ, RET_WIDTH), F32),
                   jax.ShapeDtypeStruct((DEPTH, BATCH, RET_HEADS, RET_DK, RET_DV), F32)],
        scratch_shapes=[pltpu.VMEM((RET_PB, RET_HEADS, RET_DK, RET_DV), F32)],
        name="retention_prompt",
    )(r3, r3, r3, r3, cos2, sin2, dec, qd, kd, cd, gn)
    return out.reshape(T_P, RET_WIDTH), ns


def _ret_s_kernel(q_ref, k_ref, v_ref, g_ref, st_ref, cos_ref, sin_ref, dec_ref, qd_ref, kd_ref,
                  cd_ref, gn_ref, o_ref, ns_ref):
    cos2 = cos_ref[...]
    sin2 = sin_ref[...]
    zpad = jnp.zeros((RET_DK - SUBLANES, RET_DK), F32)
    second = lax.broadcasted_iota(jnp.int32, (SUBLANES, RET_DK), 0) >= DEC_SEQ

    def pick(a, jj):
        return jnp.where(second, a[DEC_SEQ + jj:DEC_SEQ + jj + 1, :], a[jj:jj + 1, :])

    def body(p, carry):
        rows = pl.ds(pl.multiple_of(p * SUBLANES, SUBLANES), SUBLANES)
        for h in range(RET_HEADS):
            sl = slice(h * RET_DK, (h + 1) * RET_DK)
            qh = _rotary(q_ref[rows, sl], cos2, sin2)
            kh = _rotary(k_ref[rows, sl], cos2, sin2) * (RET_DK ** -0.5)
            vh = v_ref[rows, sl]
            qb = qh.astype(BF16)
            kdec = kh * kd_ref[h]
            vpad = jnp.concatenate([vh, zpad], axis=0).astype(BF16)
            from_state = []
            for x in range(PAIR):
                s_x = st_ref[0, p * PAIR + x, h]
                from_state.append(_dot(qb, s_x.astype(BF16)))
                mine = second if x else jnp.logical_not(second)
                kx = jnp.concatenate([jnp.where(mine, kdec, 0.0), zpad], axis=0)
                upd = _dot(kx.T.astype(BF16), vpad)
                ns_ref[0, p * PAIR + x, h] = s_x * cd_ref[h] + upd
            o = jnp.where(second, from_state[1], from_state[0]) * qd_ref[h]
            for jj in range(DEC_SEQ):
                sj = jnp.sum(qh * pick(kh, jj), axis=-1, keepdims=True)
                o = o + (sj * dec_ref[h, jj]) * pick(vh, jj)
            o_ref[rows, sl] = _group_norm_gate(o, g_ref[rows, sl], gn_ref[:, sl])
        return carry

    lax.fori_loop(0, RET_S_BT // PAIR, body, 0, unroll=2)


def _ret_s(r, state_ret, layer, tabs, gn, prev_ns):
    cos2, sin2, dec, qd, kd, cd = tabs
    bt = RET_S_BT
    col = lambda c: pl.BlockSpec((bt * DEC_SEQ, RET_WIDTH), lambda i: (i, c))
    full = lambda a: pl.BlockSpec(a.shape, lambda i: (0,) * a.ndim)
    st = lambda: pl.BlockSpec((1, bt, RET_HEADS, RET_DK, RET_DV), lambda i: (layer, i, 0, 0, 0))
    return _layered_call(
        _ret_s_kernel, 12, {1: prev_ns},
        grid=(DEC_BATCH // bt,),
        in_specs=[col(0), col(1), col(2), col(3), st(), full(cos2), full(sin2), full(dec), full(qd),
                  full(kd), full(cd), pl.BlockSpec((None, 1, RET_WIDTH), lambda i: (layer, 0, 0))],
        out_specs=[pl.BlockSpec((bt * DEC_SEQ, RET_WIDTH), lambda i: (i, 0)), st()],
        out_shape=[jax.ShapeDtypeStruct((T_S, RET_WIDTH), F32),
                   jax.ShapeDtypeStruct((DEPTH, DEC_BATCH, RET_HEADS, RET_DK, RET_DV), F32)],
        name="retention_sample",
    )(r, r, r, r, state_ret, cos2, sin2, dec, qd, kd, cd, gn)


def _memkv_kernel(m_ref, g_ref, wk_ref, wv_ref, kf_ref, vf_ref, kb_ref, vb_ref):
    mn = _rms(m_ref[...], g_ref[...]).astype(BF16)
    for w_ref, flat_ref, b_ref in ((wk_ref, kf_ref, kb_ref), (wv_ref, vf_ref, vb_ref)):
        y = _dot(mn, w_ref[...])
        b_ref[...] = y.astype(BF16)
        for h in range(CA_HEADS):
            for half in range(2):
                c0 = (h * 2 + half) * LANES
                flat_ref[0, pl.ds(half * CA_HEADS + h, N_MEM, stride=2 * CA_HEADS), :] = y[:, c0:c0 + LANES]


def _memkv(mem, layer, g, wk, wv, prev):
    t = mem.shape[0]
    row = lambda: pl.BlockSpec((N_MEM, D_MODEL), lambda i: (i, 0))
    flat = lambda: pl.BlockSpec((1, KV_ROWS, LANES), lambda i: (layer, i, 0))
    flat_shape = jax.ShapeDtypeStruct((DEPTH, BATCH * KV_ROWS, LANES), F32)
    square = (D_MODEL, D_MODEL)
    return _layered_call(
        _memkv_kernel, 4, {0: prev[0], 1: prev[1]},
        make_call=functools.partial(_weights_call, layer=layer, weights={2: square, 3: square}),
        grid=(t // N_MEM,),
        in_specs=[row(), _resident((1, D_MODEL), layer), None, None],
        out_specs=[flat(), flat(), row(), row()],
        out_shape=[flat_shape, flat_shape, jax.ShapeDtypeStruct((t, D_MODEL), BF16),
                   jax.ShapeDtypeStruct((t, D_MODEL), BF16)],
        name="memory_kv",
    )(mem, g, wk, wv)


def _mix_out_q(x1, conv, ret, wo_ref, gca_ref, wq_ref):
    mix = jnp.concatenate([conv, ret], axis=-1).astype(BF16)
    x2 = x1 + _dot(mix, wo_ref[...])
    q = _dot(_rms(x2, gca_ref[...]).astype(BF16), wq_ref[...])
    return x2, q


def _finish(x2, o, wco_ref, g2_ref, w1_ref, w3_ref, w2_ref, gf_ref, final):
    x3 = x2 + _dot(o, wco_ref[...])
    x4 = _ffn(x3, g2_ref, w1_ref, w3_ref, w2_ref)
    return _rms(x4, gf_ref[...]) if final else x4


def _post_p_kernel(x1_ref, conv_ref, ret_ref, mk_ref, mv_ref, wo_ref, gca_ref, wq_ref, wco_ref,
                   g2_ref, w1_ref, w3_ref, w2_ref, gf_ref, y_ref, *, final):
    x2, q = _mix_out_q(x1_ref[...], conv_ref[...], ret_ref[...], wo_ref, gca_ref, wq_ref)
    heads = []
    for h in range(CA_HEADS):
        sl = slice(h * CA_HEAD_DIM, (h + 1) * CA_HEAD_DIM)
        s = _dot_nt(q[:, sl].astype(BF16), mk_ref[:, sl]) * (CA_HEAD_DIM ** -0.5)
        p = _softmax_rows(s)
        heads.append(_dot(p.astype(BF16), mv_ref[:, sl]))
    o = jnp.concatenate(heads, axis=-1).astype(BF16)
    y_ref[...] = _finish(x2, o, wco_ref, g2_ref, w1_ref, w3_ref, w2_ref, gf_ref, final)


def _post_p(x1, conv, ret, mk, mv, layer, wo, gca, wq, wco, g2, w1, w3, w2, gf, final):
    row = lambda n: pl.BlockSpec((TM, n), lambda i: (i, 0))
    mem = lambda: pl.BlockSpec((N_MEM, D_MODEL), lambda i: (i // (SEQ // TM), 0))
    square = (D_MODEL, D_MODEL)
    return _weights_call(
        functools.partial(_post_p_kernel, final=final), layer,
        {5: square, 7: square, 8: square, 10: (D_MODEL, D_FF), 11: (D_MODEL, D_FF), 12: (D_FF, D_MODEL)},
        emit=True,
        grid=(T_P // TM,),
        in_specs=[row(D_MODEL), row(CONV_CH), row(RET_WIDTH), mem(), mem(),
                  None, _resident((1, D_MODEL), layer), None, None, _resident((1, D_MODEL), layer),
                  None, None, None,
                  pl.BlockSpec((1, D_MODEL), lambda i: (0, 0))],
        out_specs=[row(D_MODEL)],
        out_shape=[jax.ShapeDtypeStruct((T_P, D_MODEL), F32)],
        compiler_params=pltpu.CompilerParams(vmem_limit_bytes=VMEM_LIMIT),
        name="post_prompt",
    )(x1, conv, ret, mk, mv, wo, gca, wq, wco, g2, w1, w3, w2, gf)


def _post_a_s_kernel(x1_ref, conv_ref, ret_ref, wo_ref, gca_ref, wq_ref, x2_ref, q_ref):
    x2, q = _mix_out_q(x1_ref[...], conv_ref[...], ret_ref[...], wo_ref, gca_ref, wq_ref)
    x2_ref[...] = x2
    q_ref[...] = q


def _post_a_s(x1, conv, ret, layer, wo, gca, wq):
    row = lambda n: pl.BlockSpec((TM, n), lambda i: (i, 0))
    return pl.pallas_call(
        _post_a_s_kernel,
        grid=(T_S // TM,),
        in_specs=[row(D_MODEL), row(CONV_CH), row(RET_WIDTH), _whole(wo), _resident((1, D_MODEL), layer),
                  _whole(wq)],
        out_specs=[row(D_MODEL), row(D_MODEL)],
        out_shape=[jax.ShapeDtypeStruct((T_S, D_MODEL), F32)] * 2,
        name="mix_out_q_sample",
    )(x1, conv, ret, wo, gca, wq)


def _post_b_s_kernel(x2_ref, o_ref, wco_ref, g2_ref, w1_ref, w3_ref, w2_ref, gf_ref, y_ref, *, final):
    y_ref[...] = _finish(x2_ref[...], o_ref[...].astype(BF16), wco_ref, g2_ref, w1_ref, w3_ref, w2_ref,
                         gf_ref, final)


def _post_b_s(x2, o, layer, wco, g2, w1, w3, w2, gf, final):
    row = lambda: pl.BlockSpec((TM, D_MODEL), lambda i: (i, 0))
    return pl.pallas_call(
        functools.partial(_post_b_s_kernel, final=final),
        grid=(T_S // TM,),
        in_specs=[row(), row(), _whole(wco), _resident((1, D_MODEL), layer), _whole(w1), _whole(w3), _whole(w2),
                  pl.BlockSpec((1, D_MODEL), lambda i: (0, 0))],
        out_specs=row(),
        out_shape=jax.ShapeDtypeStruct((T_S, D_MODEL), F32),
        compiler_params=pltpu.CompilerParams(vmem_limit_bytes=VMEM_LIMIT),
        name="attn_out_ffn_sample",
    )(x2, o, wco, g2, w1, w3, w2, gf)


def _attn_s_kernel(q_ref, k_ref, v_ref, o_ref):
    nh = CA_HEADS * SUBLANES
    lane = lax.broadcasted_iota(jnp.int32, (nh, KV_ROWS), 1)
    row = lax.broadcasted_iota(jnp.int32, (nh, KV_ROWS), 0)
    valid = (lane % (2 * CA_HEADS)) == (row // SUBLANES)
    second = lax.broadcasted_iota(jnp.int32, (2 * nh, LANES), 0) % SUBLANES >= DEC_SEQ
    for pr in range(ATT_S_BT // PAIR):
        q = q_ref[pr * SUBLANES:(pr + 1) * SUBLANES, :]
        blocks = [q[:, (h * 2 + half) * LANES:(h * 2 + half + 1) * LANES]
                  for half in range(2) for h in range(CA_HEADS)]
        qt = jnp.concatenate(blocks, axis=0).astype(BF16)
        outs = []
        for x in range(PAIR):
            b = pr * PAIR + x
            st = _dot_nt(qt, k_ref[0, b].astype(BF16))
            s = (st[:nh] + pltpu.roll(st[nh:], KV_ROWS - CA_HEADS, 1)) * (CA_HEAD_DIM ** -0.5)
            p = _softmax_rows(jnp.where(valid, s, NEG_BIG))
            w = jnp.concatenate([p, pltpu.roll(p, CA_HEADS, 1)], axis=0).astype(BF16)
            outs.append(_dot(w, v_ref[0, b].astype(BF16)))
        o = jnp.where(second, outs[1], outs[0])
        for half in range(2):
            for h in range(CA_HEADS):
                r0 = (half * CA_HEADS + h) * SUBLANES
                c0 = (h * 2 + half) * LANES
                o_ref[pr * SUBLANES:(pr + 1) * SUBLANES, c0:c0 + LANES] = o[r0:r0 + SUBLANES]


def _flat_cache(c):
    c = c.reshape(DEPTH, DEC_BATCH, N_MEM, CA_HEADS, 2, LANES)
    return c.transpose(0, 1, 2, 4, 3, 5).reshape(DEPTH, DEC_BATCH, KV_ROWS, LANES)


def _unflat_cache(f):
    f = f.reshape(DEPTH, BATCH, N_MEM, 2, CA_HEADS, LANES)
    return f.transpose(0, 1, 2, 4, 3, 5).reshape(DEPTH, BATCH, N_MEM, CA_HEADS, CA_HEAD_DIM)


def _attn_s(q, cache_k, cache_v, layer):
    bt = ATT_S_BT
    qs = lambda: pl.BlockSpec((bt * DEC_SEQ, D_MODEL), lambda i: (i, 0))
    kv = lambda: pl.BlockSpec((1, bt, KV_ROWS, LANES), lambda i: (layer, i, 0, 0))
    return pl.pallas_call(
        _attn_s_kernel,
        grid=(DEC_BATCH // bt,),
        in_specs=[qs(), kv(), kv()],
        out_specs=qs(),
        out_shape=jax.ShapeDtypeStruct((T_S, D_MODEL), F32),
        compiler_params=pltpu.CompilerParams(vmem_limit_bytes=VMEM_LIMIT),
        name="cross_attn_sample",
    )(q, cache_k, cache_v)


def _rope_tables(pos):
    inv_freq = ROPE_BASE ** (-jnp.arange(0, RET_DK, 2, dtype=F32) / RET_DK)
    ang = pos[:, None] * inv_freq[None, :]
    cos, sin = jnp.cos(ang), jnp.sin(ang)
    return jnp.concatenate([cos, cos], axis=-1), jnp.concatenate([-sin, sin], axis=-1)


def _decay_tables(c):
    log_gamma = jnp.log1p(-jnp.exp2(-5.0 - jnp.arange(RET_HEADS, dtype=F32)))
    idx = jnp.arange(c, dtype=F32)
    rel = idx[:, None] - idx[None, :]
    decay = jnp.where(rel[None] >= 0,
                      jnp.exp(log_gamma[:, None, None] * jnp.maximum(rel, 0.0)[None]), 0.0)
    q_dec = jnp.exp(log_gamma[:, None] * (idx[None, :] + 1.0))
    k_dec = jnp.exp(log_gamma[:, None] * (c - 1.0 - idx[None, :]))
    chunk_dec = jnp.exp(log_gamma * c)
    return decay, q_dec, k_dec, chunk_dec


def _prompt_tables():
    cos2, sin2 = _rope_tables(jnp.arange(SEQ, dtype=F32))
    decay, q_dec, k_dec, chunk_dec = _decay_tables(RET_CHUNK)
    lanes = (RET_HEADS, RET_CHUNK, RET_DK)
    return (cos2, sin2, decay,
            jnp.broadcast_to(q_dec[:, :, None], lanes),
            jnp.broadcast_to(k_dec[:, :, None], lanes),
            jnp.broadcast_to(chunk_dec[:, None, None], lanes))


def _sample_tables():
    slab = lambda a: jnp.concatenate([a] * PAIR, axis=-2)
    cos2, sin2 = _rope_tables(PAST_LEN + jnp.arange(DEC_SEQ, dtype=F32))
    decay, q_dec, k_dec, chunk_dec = _decay_tables(DEC_SEQ)
    rows = (RET_HEADS, DEC_SEQ, RET_DK)
    dec = jnp.broadcast_to(jnp.swapaxes(decay, 1, 2)[:, :, :, None], (RET_HEADS, DEC_SEQ, DEC_SEQ, RET_DK))
    return (slab(cos2), slab(sin2), slab(dec),
            slab(jnp.broadcast_to(q_dec[:, :, None], rows)),
            slab(jnp.broadcast_to(k_dec[:, :, None], rows)),
            jnp.broadcast_to(chunk_dec[:, None, None], (RET_HEADS, RET_DK, RET_DV)))


def kernel(x_prompt, x_sample, state_conv, state_ret, cache_mem_k, cache_mem_v, mem_prompt, g_ffn1, w1_ffn1, w3_ffn1, w2_ffn1, g_mix, w_in, conv_w, conv_b, conv_ln_g, conv_ln_b, ret_gn_g, w_out, g_ca, g_mem, w_cq, w_ck, w_cv, w_co, g_ffn2, w1_ffn2, w3_ffn2, w2_ffn2, g_final):
    vec = lambda g: g.reshape(DEPTH, 1, -1)
    w1a, w3a, w2a, w1b, w3b, w2b = w1_ffn1, w3_ffn1, w2_ffn1, w1_ffn2, w3_ffn2, w2_ffn2
    win, wout, wcq, wck, wcv, wco = w_in, w_out, w_cq, w_ck, w_cv, w_co
    g1, gm, gca, gmem, g2 = map(vec, (g_ffn1, g_mix, g_ca, g_mem, g_ffn2))
    cb, clg, clb, gn = map(vec, (conv_b, conv_ln_g, conv_ln_b, ret_gn_g))
    gf = g_final.reshape(1, D_MODEL)
    cache_k = _flat_cache(cache_mem_k)
    cache_v = _flat_cache(cache_mem_v)
    mem = mem_prompt.reshape(BATCH * N_MEM, D_MODEL)
    tabs_p = _prompt_tables()
    tabs_s = _sample_tables()
    cw8 = conv_w.reshape(DEPTH, CONV_K, CONV_LT, LANES).transpose(0, 2, 1, 3)
    cw8 = jnp.broadcast_to(cw8[:, :, :, None, :], (DEPTH, CONV_LT, CONV_K, SUBLANES, LANES))

    state_t = jnp.swapaxes(state_conv, 1, 2)

    xp = x_prompt.reshape(T_P, D_MODEL)
    xs = x_sample.reshape(T_S, D_MODEL)
    conv_p = jnp.zeros((DEPTH, BATCH, CONV_K - 1, CONV_CH), F32)
    ret_p = jnp.zeros((DEPTH, BATCH, RET_HEADS, RET_DK, RET_DV), F32)
    mem_p = tuple(jnp.zeros((DEPTH, BATCH * KV_ROWS, LANES), F32) for _ in range(2))
    conv_s = jnp.zeros((DEPTH, CONV_K - 1, DEC_BATCH, CONV_CH), F32)
    ret_s = jnp.zeros((DEPTH, DEC_BATCH, RET_HEADS, RET_DK, RET_DV), F32)
    for l in range(DEPTH):
        final = l == DEPTH - 1
        memk_f, memv_f, mk, mv = _memkv(mem, l, gmem, wck, wcv, mem_p)
        mem_p = (memk_f, memv_f)
        x1, u, r, *mix_w = _ffn_inproj(xp, l, g1, gm, (w1a, w3a, w2a, win), cast=True)
        cv, conv_p = _conv_p(u, l, cw8, cb, clg, clb, conv_p)
        rt, ret_p = _ret_p(r, l, tabs_p, gn, ret_p)
        xp, wout_b, wcq_b, wco_b, w1_b, w3_b, w2_b = _post_p(
            x1, cv, rt, mk, mv, l, wout, gca, wcq, wco, g2, w1b, w3b, w2b, gf, final)

        x1, u, r = _ffn_inproj(xs, l, g1, gm, mix_w, cast=False)
        ut = jnp.swapaxes(u.reshape(DEC_BATCH, DEC_SEQ, CONV_CH), 0, 1)
        cv, conv_s = _conv_s(state_t, ut, l, conv_w, cb, clg, clb, conv_s)
        cv = jnp.swapaxes(cv, 0, 1).reshape(T_S, CONV_CH)
        rt, ret_s = _ret_s(r, state_ret, l, tabs_s, gn, ret_s)
        x2, q = _post_a_s(x1, cv, rt, l, wout_b, gca, wcq_b)
        o = _attn_s(q, cache_k, cache_v, l)
        xs = _post_b_s(x2, o, l, wco_b, g2, w1_b, w3_b, w2_b, gf, final)

    return (xp.reshape(BATCH, SEQ, D_MODEL), xs.reshape(DEC_BATCH, DEC_SEQ, D_MODEL),
            conv_p, ret_p, _unflat_cache(mem_p[0]), _unflat_cache(mem_p[1]),
            jnp.swapaxes(conv_s, 1, 2), ret_s)
```

```python
import functools

import jax
import jax.numpy as jnp
from jax import lax
from jax.experimental import pallas as pl
from jax.experimental.pallas import tpu as pltpu

F32 = jnp.float32
BF16 = jnp.bfloat16

D_MODEL = 1024
BATCH = 8
SEQ = 2048
DEPTH = 2
DEC_BATCH = 128
DEC_SEQ = 4
PAST_LEN = 16384
CONV_CH = 512
CONV_K = 31
RET_HEADS = 4
RET_DK = 128
RET_DV = 128
RET_WIDTH = 512
MIX_IN = 3072
RET_CHUNK = 128
ROPE_BASE = 10000.0
D_FF = 4096
N_MEM = 256
CA_HEADS = 4
CA_HEAD_DIM = 256
EPS = 1e-6
GN_EPS = 1e-5

T_P = BATCH * SEQ
T_S = DEC_BATCH * DEC_SEQ

TM = 512
TF = 1024
CONV_TT = 512
CONV_RC = 64
CONV_HALO = 32
SUBLANES = 8
LANES = 128
CONV_ROWS = CONV_HALO + CONV_TT
CONV_LT = CONV_CH // LANES
CONV_NORM_UNROLL = 4
RET_CPS = 4
RET_PB = 2
PAIR = SUBLANES // DEC_SEQ
RET_S_BT = 8
ATT_S_BT = 4
KV_ROWS = N_MEM * 2 * CA_HEADS
NEG_BIG = -1e30
CONV_S_BT = 32
STAGE_BYTES = 512 * 1024
STAGE_SLOTS = 8
VMEM_LIMIT = 60 * 1024 * 1024


def _dot(a, b):
    return jnp.dot(a, b, preferred_element_type=F32)


def _dot_nt(a, b):
    return lax.dot_general(a, b, (((1,), (1,)), ((), ())), preferred_element_type=F32)


def _rms(x, g):
    return x * lax.rsqrt(jnp.mean(x * x, axis=-1, keepdims=True) + EPS) * g


def _silu(x):
    return x * jax.nn.sigmoid(x)


def _ffn(x, g_ref, w1_ref, w3_ref, w2_ref):
    xn = _rms(x, g_ref[...]).astype(BF16)
    acc = None
    for c in range(D_FF // TF):
        sl = slice(c * TF, (c + 1) * TF)
        h1 = _dot(xn, w1_ref[:, sl])
        h3 = _dot(xn, w3_ref[:, sl])
        a = (_silu(h1) * h3).astype(BF16)
        part = _dot(a, w2_ref[sl, :])
        acc = part if acc is None else acc + part
    return x + 0.5 * acc


def _softmax_rows(s):
    m = jnp.max(s, axis=-1, keepdims=True)
    e = jnp.exp(s - m)
    return e * (1.0 / jnp.sum(e, axis=-1, keepdims=True))


def _group_norm_gate(o, gate, gn):
    mu = jnp.mean(o, axis=-1, keepdims=True)
    d = o - mu
    var = jnp.mean(d * d, axis=-1, keepdims=True)
    return _silu(gate) * (d * lax.rsqrt(var + GN_EPS) * gn)


def _rotary(t, cos2, sin2):
    return t * cos2 + pltpu.roll(t, RET_DK // 2, 1) * sin2


def _ffn_inproj_kernel(x_ref, g1_ref, w1_ref, w3_ref, w2_ref, gm_ref, win_ref,
                       x1_ref, u_ref, r_ref):
    x1 = _ffn(x_ref[...], g1_ref, w1_ref, w3_ref, w2_ref)
    x1_ref[...] = x1
    h = _rms(x1, gm_ref[...]).astype(BF16)
    proj = _dot(h, win_ref[...])
    u_ref[...] = proj[:, :CONV_CH] * jax.nn.sigmoid(proj[:, CONV_CH:2 * CONV_CH])
    r_ref[...] = proj[:, 2 * CONV_CH:]


def _resident(shape, layer):
    nd = len(shape)
    return pl.BlockSpec((None,) + tuple(shape), lambda *_: (layer,) + (0,) * nd,
                        pipeline_mode=pl.Buffered(1))


def _whole(a):
    return pl.BlockSpec(a.shape, lambda *_: (0,) * a.ndim, pipeline_mode=pl.Buffered(1))


def _stream_cast(w_hbm, layer, dst_ref):
    rows, cols = dst_ref.shape
    ch = 1 << ((STAGE_BYTES // (4 * cols)).bit_length() - 1)
    assert ch % SUBLANES == 0 and rows % ch == 0
    n = rows // ch

    ahead = min(STAGE_SLOTS - 1, n)

    def run(stage, sem):
        def chunk_copy(k):
            slot = lax.rem(k, STAGE_SLOTS)
            return pltpu.make_async_copy(w_hbm.at[layer, pl.ds(k * ch, ch), :], stage.at[slot], sem.at[slot])

        for k in range(ahead):
            chunk_copy(k).start()

        def body(k, carry):
            @pl.when(k + ahead < n)
            def _():
                chunk_copy(k + ahead).start()

            chunk_copy(k).wait()
            dst_ref[pl.ds(pl.multiple_of(k * ch, ch), ch), :] = stage[lax.rem(k, STAGE_SLOTS)].astype(BF16)
            return carry

        lax.fori_loop(0, n, body, 0)

    pl.run_scoped(run, pltpu.VMEM((STAGE_SLOTS, ch, cols), F32), pltpu.SemaphoreType.DMA((STAGE_SLOTS,)))


def _weights_call(body, layer, weights, emit=False, **kw):
    idx = sorted(weights)
    nw = len(idx)
    in_specs = list(kw.pop("in_specs"))
    for i in idx:
        in_specs[i] = pl.BlockSpec(memory_space=pl.ANY)
    scratch = list(kw.pop("scratch_shapes", ())) + [pltpu.VMEM(weights[i], BF16) for i in idx]
    out_specs, out_shape = list(kw.pop("out_specs")), list(kw.pop("out_shape"))
    n_io = len(in_specs) + len(out_specs)
    if emit:
        assert kw["grid"][0] >= 2
        out_specs += [pl.BlockSpec(memory_space=pl.ANY)] * nw
        out_shape += [jax.ShapeDtypeStruct(weights[i], BF16) for i in idx]
        scratch.append(pltpu.SemaphoreType.DMA((nw,)))

    def with_weights(*refs):
        refs = list(refs)
        sem = refs.pop() if emit else None
        slabs = refs[len(refs) - nw:]
        del refs[len(refs) - nw:]
        copies = []
        if emit:
            copies = [pltpu.make_async_copy(slab, out, sem.at[k])
                      for k, (slab, out) in enumerate(zip(slabs, refs[n_io:n_io + nw]))]
            del refs[n_io:n_io + nw]

        @pl.when(pl.program_id(0) == 0)
        def _():
            for i, slab in zip(idx, slabs):
                _stream_cast(refs[i], layer, slab)

        if emit:
            @pl.when(pl.program_id(0) == 1)
            def _():
                for c in copies:
                    c.start()

        for i, slab in zip(idx, slabs):
            refs[i] = slab
        body(*refs)

        if emit:
            @pl.when(pl.program_id(0) == pl.num_programs(0) - 1)
            def _():
                for c in copies:
                    c.wait()

    return pl.pallas_call(with_weights, in_specs=in_specs, out_specs=out_specs, out_shape=out_shape,
                          scratch_shapes=scratch, **kw)


def _layered_call(body, n_in, prev, make_call=pl.pallas_call, **kw):
    idx = sorted(prev)
    kw["in_specs"] = list(kw["in_specs"]) + [pl.BlockSpec(memory_space=pl.ANY)] * len(idx)

    def with_aliased(*refs):
        return body(*refs[:n_in], *refs[n_in + len(idx):])

    call = make_call(with_aliased, input_output_aliases={n_in + k: o for k, o in enumerate(idx)}, **kw)
    return lambda *args: call(*args, *[prev[o] for o in idx])


def _ffn_inproj(x, layer, g1, gm, w, cast):
    t = x.shape[0]
    row = lambda n: pl.BlockSpec((TM, n), lambda i: (i, 0))
    kw = dict(
        grid=(t // TM,),
        out_specs=[row(D_MODEL), row(CONV_CH), row(MIX_IN - 2 * CONV_CH)],
        out_shape=[jax.ShapeDtypeStruct((t, D_MODEL), F32),
                   jax.ShapeDtypeStruct((t, CONV_CH), F32),
                   jax.ShapeDtypeStruct((t, MIX_IN - 2 * CONV_CH), F32)],
        compiler_params=pltpu.CompilerParams(vmem_limit_bytes=VMEM_LIMIT),
        name="ffn_inproj")
    vec = _resident((1, D_MODEL), layer)
    args = (x, g1, w[0], w[1], w[2], gm, w[3])
    if not cast:
        return pl.pallas_call(
            _ffn_inproj_kernel,
            in_specs=[row(D_MODEL), vec, _whole(w[0]), _whole(w[1]), _whole(w[2]), vec, _whole(w[3])], **kw)(*args)
    return _weights_call(
        _ffn_inproj_kernel, layer,
        {2: (D_MODEL, D_FF), 3: (D_MODEL, D_FF), 4: (D_FF, D_MODEL), 6: (D_MODEL, MIX_IN)}, emit=True,
        in_specs=[row(D_MODEL), vec, None, None, None, vec, None], **kw)(*args)


def _layer_norm_silu(y, lg, lb):
    mu = jnp.mean(y, axis=-1, keepdims=True)
    d = y - mu
    var = jnp.mean(d * d, axis=-1, keepdims=True)
    return _silu(d * lax.rsqrt(var + EPS) * lg + lb)


def _conv_p_kernel(u_ref, w_ref, b_ref, lg_ref, lb_ref, y_ref, nb_ref, win_ref, acc_ref):
    j = pl.program_id(1)
    for t in range(CONV_LT):
        @pl.when(j == 0)
        def _():
            win_ref[0, t, 0:CONV_HALO, :] = jnp.zeros((CONV_HALO, LANES), F32)

        @pl.when(j > 0)
        def _():
            win_ref[0, t, 0:CONV_HALO, :] = win_ref[0, t, CONV_TT:CONV_TT + CONV_HALO, :]

        win_ref[0, t, CONV_HALO:CONV_ROWS, :] = u_ref[:, t * LANES:(t + 1) * LANES]
        win_ref[0, t, CONV_ROWS:CONV_ROWS + SUBLANES, :] = jnp.zeros((SUBLANES, LANES), F32)
        for sh in range(1, SUBLANES):
            win_ref[sh, t, 0:CONV_ROWS, :] = win_ref[0, t, sh:sh + CONV_ROWS, :]

    first = CONV_HALO - (CONV_K - 1)
    n_rc = CONV_TT // CONV_RC

    def taps(idx, carry):
        t = idx // n_rc
        r0 = pl.multiple_of((idx % n_rc) * CONV_RC, CONV_RC)
        accs = [None] * (CONV_RC // SUBLANES)
        for k in range(CONV_K):
            sh = (first + k) % SUBLANES
            wk = w_ref[t, k]
            for i in range(CONV_RC // SUBLANES):
                a0 = pl.multiple_of(r0 + (first + k - sh) + i * SUBLANES, SUBLANES)
                term = wk * win_ref[sh, t, pl.ds(a0, SUBLANES), :]
                accs[i] = term if accs[i] is None else accs[i] + term
        acc_ref[t, pl.ds(r0, CONV_RC), :] = jnp.concatenate(accs, axis=0)
        return carry

    lax.fori_loop(0, CONV_LT * n_rc, taps, 0)

    def norm(c, carry):
        r0 = pl.multiple_of(c * CONV_RC, CONV_RC)
        y = jnp.concatenate([acc_ref[t, pl.ds(r0, CONV_RC), :] for t in range(CONV_LT)], axis=-1) + b_ref[...]
        y_ref[pl.ds(r0, CONV_RC), :] = _layer_norm_silu(y, lg_ref[...], lb_ref[...])
        return carry

    lax.fori_loop(0, n_rc, norm, 0, unroll=CONV_NORM_UNROLL)

    @pl.when(j == pl.num_programs(1) - 1)
    def _():
        nb_ref[0, 0] = u_ref[CONV_TT - (CONV_K - 1):CONV_TT, :]


def _conv_p(u, layer, conv_w, conv_b, ln_g, ln_b, prev_nb):
    nt = SEQ // CONV_TT
    vec = lambda: pl.BlockSpec((None, 1, CONV_CH), lambda b, j: (layer, 0, 0))
    return _layered_call(
        _conv_p_kernel, 5, {1: prev_nb},
        grid=(BATCH, nt),
        in_specs=[pl.BlockSpec((CONV_TT, CONV_CH), lambda b, j: (b * nt + j, 0)),
                  pl.BlockSpec((None, CONV_LT, CONV_K, SUBLANES, LANES), lambda b, j: (layer, 0, 0, 0, 0)),
                  vec(), vec(), vec()],
        out_specs=[pl.BlockSpec((CONV_TT, CONV_CH), lambda b, j: (b * nt + j, 0)),
                   pl.BlockSpec((1, 1, CONV_K - 1, CONV_CH), lambda b, j: (layer, b, 0, 0))],
        out_shape=[jax.ShapeDtypeStruct((T_P, CONV_CH), F32),
                   jax.ShapeDtypeStruct((DEPTH, BATCH, CONV_K - 1, CONV_CH), F32)],
        scratch_shapes=[pltpu.VMEM((SUBLANES, CONV_LT, CONV_ROWS + SUBLANES, LANES), F32),
                        pltpu.VMEM((CONV_LT, CONV_TT, LANES), F32)],
        name="conv_prompt",
    )(u, conv_w, conv_b, ln_g, ln_b)


def _conv_s_kernel(st_ref, ut_ref, w_ref, b_ref, lg_ref, lb_ref, y_ref, nb_ref):
    keep = CONV_K - 1 - DEC_SEQ
    for t in range(DEC_SEQ):
        y = None
        for j in range(t, CONV_K - 1):
            term = st_ref[0, j] * w_ref[j - t:j - t + 1, :]
            y = term if y is None else y + term
        for j in range(t + 1):
            k = CONV_K - 1 - t + j
            y = y + ut_ref[j] * w_ref[k:k + 1, :]
        y_ref[t] = _layer_norm_silu(y + b_ref[...], lg_ref[...], lb_ref[...])
    for j in range(keep):
        nb_ref[0, j] = st_ref[0, j + DEC_SEQ]
    for t in range(DEC_SEQ):
        nb_ref[0, keep + t] = ut_ref[t]


def _conv_s(state_t, ut, layer, conv_w, conv_b, ln_g, ln_b, prev_nb):
    bt = CONV_S_BT
    vec = lambda: pl.BlockSpec((None, 1, CONV_CH), lambda i: (layer, 0, 0))
    return _layered_call(
        _conv_s_kernel, 6, {1: prev_nb},
        grid=(DEC_BATCH // bt,),
        in_specs=[pl.BlockSpec((1, CONV_K - 1, bt, CONV_CH), lambda i: (layer, 0, i, 0)),
                  pl.BlockSpec((DEC_SEQ, bt, CONV_CH), lambda i: (0, i, 0)),
                  pl.BlockSpec((None, CONV_K, CONV_CH), lambda i: (layer, 0, 0)),
                  vec(), vec(), vec()],
        out_specs=[pl.BlockSpec((DEC_SEQ, bt, CONV_CH), lambda i: (0, i, 0)),
                   pl.BlockSpec((1, CONV_K - 1, bt, CONV_CH), lambda i: (layer, 0, i, 0))],
        out_shape=[jax.ShapeDtypeStruct((DEC_SEQ, DEC_BATCH, CONV_CH), F32),
                   jax.ShapeDtypeStruct((DEPTH, CONV_K - 1, DEC_BATCH, CONV_CH), F32)],
        name="conv_sample",
    )(state_t, ut, conv_w, conv_b, ln_g, ln_b)


def _ret_p_kernel(q_ref, k_ref, v_ref, g_ref, cos_ref, sin_ref, dec_ref, qd_ref, kd_ref,
                  cd_ref, gn_ref, o_ref, ns_ref, s_ref):
    j = pl.program_id(1)

    @pl.when(j == 0)
    def _():
        s_ref[...] = jnp.zeros_like(s_ref)

    for c in range(RET_CPS):
        rows = slice(c * RET_CHUNK, (c + 1) * RET_CHUNK)
        cos2 = cos_ref[rows, :]
        sin2 = sin_ref[rows, :]
        for h in range(RET_HEADS):
            sl = slice(h * RET_DK, (h + 1) * RET_DK)
            for e in range(RET_PB):
                qh = _rotary(q_ref[e, rows, sl], cos2, sin2)
                kh = _rotary(k_ref[e, rows, sl], cos2, sin2) * (RET_DK ** -0.5)
                qb = qh.astype(BF16)
                vb = v_ref[e, rows, sl].astype(BF16)
                s_h = s_ref[e, h]
                sc = _dot_nt(qb, kh.astype(BF16)) * dec_ref[h]
                o = _dot(sc.astype(BF16), vb) + _dot(qb, s_h.astype(BF16)) * qd_ref[h]
                kdec = (kh * kd_ref[h]).astype(BF16)
                upd = lax.dot_general(kdec, vb, (((0,), (0,)), ((), ())), preferred_element_type=F32)
                s_ref[e, h] = s_h * cd_ref[h] + upd
                o_ref[e, rows, sl] = _group_norm_gate(o, g_ref[e, rows, sl], gn_ref[:, sl])

    @pl.when(j == pl.num_programs(1) - 1)
    def _():
        ns_ref[0] = s_ref[...]


def _ret_p(r, layer, tabs, gn, prev_ns):
    cos2, sin2, dec, qd, kd, cd = tabs
    rows = RET_CPS * RET_CHUNK
    r3 = r.reshape(BATCH, SEQ, r.shape[-1])
    col = lambda c: pl.BlockSpec((RET_PB, rows, RET_WIDTH), lambda b, j: (b, j, c))
    const = lambda: pl.BlockSpec((RET_HEADS, RET_CHUNK, RET_DK), lambda b, j: (0, 0, 0))
    tab = lambda: pl.BlockSpec((rows, RET_DK), lambda b, j: (j, 0))
    out, ns = _layered_call(
        _ret_p_kernel, 11, {1: prev_ns},
        grid=(BATCH // RET_PB, SEQ // rows),
        in_specs=[col(0), col(1), col(2), col(3), tab(), tab(), const(), const(), const(), const(),
                  pl.BlockSpec((None, 1, RET_WIDTH), lambda b, j: (layer, 0, 0))],
        out_specs=[pl.BlockSpec((RET_PB, rows, RET_WIDTH), lambda b, j: (b, j, 0)),
                   pl.BlockSpec((1, RET_PB, RET_HEADS, RET_DK, RET_DV), lambda b, j: (layer, b, 0, 0, 0))],
        out_shape=[jax.ShapeDtypeStruct((BATCH, SEQ, RET_WIDTH), F32),
                   jax.ShapeDtypeStruct((DEPTH, BATCH, RET_HEADS, RET_DK, RET_DV), F32)],
        scratch_shapes=[pltpu.VMEM((RET_PB, RET_HEADS, RET_DK, RET_DV), F32)],
        name="retention_prompt",
    )(r3, r3, r3, r3, cos2, sin2, dec, qd, kd, cd, gn)
    return out.reshape(T_P, RET_WIDTH), ns


def _ret_s_kernel(q_ref, k_ref, v_ref, g_ref, st_ref, cos_ref, sin_ref, dec_ref, qd_ref, kd_ref,
                  cd_ref, gn_ref, o_ref, ns_ref):
    cos2 = cos_ref[...]
    sin2 = sin_ref[...]
    zpad = jnp.zeros((RET_DK - SUBLANES, RET_DK), F32)
    second = lax.broadcasted_iota(jnp.int32, (SUBLANES, RET_DK), 0) >= DEC_SEQ

    def pick(a, jj):
        return jnp.where(second, a[DEC_SEQ + jj:DEC_SEQ + jj + 1, :], a[jj:jj + 1, :])

    def body(p, carry):
        rows = pl.ds(pl.multiple_of(p * SUBLANES, SUBLANES), SUBLANES)
        for h in range(RET_HEADS):
            sl = slice(h * RET_DK, (h + 1) * RET_DK)
            qh = _rotary(q_ref[rows, sl], cos2, sin2)
            kh = _rotary(k_ref[rows, sl], cos2, sin2) * (RET_DK ** -0.5)
            vh = v_ref[rows, sl]
            qb = qh.astype(BF16)
            kdec = kh * kd_ref[h]
            vpad = jnp.concatenate([vh, zpad], axis=0).astype(BF16)
            from_state = []
            for x in range(PAIR):
                s_x = st_ref[0, p * PAIR + x, h]
                from_state.append(_dot(qb, s_x.astype(BF16)))
                mine = second if x else jnp.logical_not(second)
                kx = jnp.concatenate([jnp.where(mine, kdec, 0.0), zpad], axis=0)
                upd = _dot(kx.T.astype(BF16), vpad)
                ns_ref[0, p * PAIR + x, h] = s_x * cd_ref[h] + upd
            o = jnp.where(second, from_state[1], from_state[0]) * qd_ref[h]
            for jj in range(DEC_SEQ):
                sj = jnp.sum(qh * pick(kh, jj), axis=-1, keepdims=True)
                o = o + (sj * dec_ref[h, jj]) * pick(vh, jj)
            o_ref[rows, sl] = _group_norm_gate(o, g_ref[rows, sl], gn_ref[:, sl])
        return carry

    lax.fori_loop(0, RET_S_BT // PAIR, body, 0, unroll=2)


def _ret_s(r, state_ret, layer, tabs, gn, prev_ns):
    cos2, sin2, dec, qd, kd, cd = tabs
    bt = RET_S_BT
    col = lambda c: pl.BlockSpec((bt * DEC_SEQ, RET_WIDTH), lambda i: (i, c))
    full = lambda a: pl.BlockSpec(a.shape, lambda i: (0,) * a.ndim)
    st = lambda: pl.BlockSpec((1, bt, RET_HEADS, RET_DK, RET_DV), lambda i: (layer, i, 0, 0, 0))
    return _layered_call(
        _ret_s_kernel, 12, {1: prev_ns},
        grid=(DEC_BATCH // bt,),
        in_specs=[col(0), col(1), col(2), col(3), st(), full(cos2), full(sin2), full(dec), full(qd),
                  full(kd), full(cd), pl.BlockSpec((None, 1, RET_WIDTH), lambda i: (layer, 0, 0))],
        out_specs=[pl.BlockSpec((bt * DEC_SEQ, RET_WIDTH), lambda i: (i, 0)), st()],
        out_shape=[jax.ShapeDtypeStruct((T_S, RET_WIDTH), F32),
                   jax.ShapeDtypeStruct((DEPTH, DEC_BATCH, RET_HEADS, RET_DK, RET_DV), F32)],
        name="retention_sample",
    )(r, r, r, r, state_ret, cos2, sin2, dec, qd, kd, cd, gn)


def _memkv_kernel(m_ref, g_ref, wk_ref, wv_ref, kf_ref, vf_ref, kb_ref, vb_ref):
    mn = _rms(m_ref[...], g_ref[...]).astype(BF16)
    for w_ref, flat_ref, b_ref in ((wk_ref, kf_ref, kb_ref), (wv_ref, vf_ref, vb_ref)):
        y = _dot(mn, w_ref[...])
        b_ref[...] = y.astype(BF16)
        for h in range(CA_HEADS):
            for half in range(2):
                c0 = (h * 2 + half) * LANES
                flat_ref[0, pl.ds(half * CA_HEADS + h, N_MEM, stride=2 * CA_HEADS), :] = y[:, c0:c0 + LANES]


def _memkv(mem, layer, g, wk, wv, prev):
    t = mem.shape[0]
    row = lambda: pl.BlockSpec((N_MEM, D_MODEL), lambda i: (i, 0))
    flat = lambda: pl.BlockSpec((1, KV_ROWS, LANES), lambda i: (layer, i, 0))
    flat_shape = jax.ShapeDtypeStruct((DEPTH, BATCH * KV_ROWS, LANES), F32)
    square = (D_MODEL, D_MODEL)
    return _layered_call(
        _memkv_kernel, 4, {0: prev[0], 1: prev[1]},
        make_call=functools.partial(_weights_call, layer=layer, weights={2: square, 3: square}),
        grid=(t // N_MEM,),
        in_specs=[row(), _resident((1, D_MODEL), layer), None, None],
        out_specs=[flat(), flat(), row(), row()],
        out_shape=[flat_shape, flat_shape, jax.ShapeDtypeStruct((t, D_MODEL), BF16),
                   jax.ShapeDtypeStruct((t, D_MODEL), BF16)],
        name="memory_kv",
    )(mem, g, wk, wv)


def _mix_out_q(x1, conv, ret, wo_ref, gca_ref, wq_ref):
    mix = jnp.concatenate([conv, ret], axis=-1).astype(BF16)
    x2 = x1 + _dot(mix, wo_ref[...])
    q = _dot(_rms(x2, gca_ref[...]).astype(BF16), wq_ref[...])
    return x2, q


def _finish(x2, o, wco_ref, g2_ref, w1_ref, w3_ref, w2_ref, gf_ref, final):
    x3 = x2 + _dot(o, wco_ref[...])
    x4 = _ffn(x3, g2_ref, w1_ref, w3_ref, w2_ref)
    return _rms(x4, gf_ref[...]) if final else x4


def _post_p_kernel(x1_ref, conv_ref, ret_ref, mk_ref, mv_ref, wo_ref, gca_ref, wq_ref, wco_ref,
                   g2_ref, w1_ref, w3_ref, w2_ref, gf_ref, y_ref, *, final):
    x2, q = _mix_out_q(x1_ref[...], conv_ref[...], ret_ref[...], wo_ref, gca_ref, wq_ref)
    heads = []
    for h in range(CA_HEADS):
        sl = slice(h * CA_HEAD_DIM, (h + 1) * CA_HEAD_DIM)
        s = _dot_nt(q[:, sl].astype(BF16), mk_ref[:, sl]) * (CA_HEAD_DIM ** -0.5)
        p = _softmax_rows(s)
        heads.append(_dot(p.astype(BF16), mv_ref[:, sl]))
    o = jnp.concatenate(heads, axis=-1).astype(BF16)
    y_ref[...] = _finish(x2, o, wco_ref, g2_ref, w1_ref, w3_ref, w2_ref, gf_ref, final)


def _post_p(x1, conv, ret, mk, mv, layer, wo, gca, wq, wco, g2, w1, w3, w2, gf, final):
    row = lambda n: pl.BlockSpec((TM, n), lambda i: (i, 0))
    mem = lambda: pl.BlockSpec((N_MEM, D_MODEL), lambda i: (i // (SEQ // TM), 0))
    square = (D_MODEL, D_MODEL)
    return _weights_call(
        functools.partial(_post_p_kernel, final=final), layer,
        {5: square, 7: square, 8: square, 10: (D_MODEL, D_FF), 11: (D_MODEL, D_FF), 12: (D_FF, D_MODEL)},
        emit=True,
        grid=(T_P // TM,),
        in_specs=[row(D_MODEL), row(CONV_CH), row(RET_WIDTH), mem(), mem(),
                  None, _resident((1, D_MODEL), layer), None, None, _resident((1, D_MODEL), layer),
                  None, None, None,
                  pl.BlockSpec((1, D_MODEL), lambda i: (0, 0))],
        out_specs=[row(D_MODEL)],
        out_shape=[jax.ShapeDtypeStruct((T_P, D_MODEL), F32)],
        compiler_params=pltpu.CompilerParams(vmem_limit_bytes=VMEM_LIMIT),
        name="post_prompt",
    )(x1, conv, ret, mk, mv, wo, gca, wq, wco, g2, w1, w3, w2, gf)


def _post_a_s_kernel(x1_ref, conv_ref, ret_ref, wo_ref, gca_ref, wq_ref, x2_ref, q_ref):
    x2, q = _mix_out_q(x1_ref[...], conv_ref[...], ret_ref[...], wo_ref, gca_ref, wq_ref)
    x2_ref[...] = x2
    q_ref[...] = q


def _post_a_s(x1, conv, ret, layer, wo, gca, wq):
    row = lambda n: pl.BlockSpec((TM, n), lambda i: (i, 0))
    return pl.pallas_call(
        _post_a_s_kernel,
        grid=(T_S // TM,),
        in_specs=[row(D_MODEL), row(CONV_CH), row(RET_WIDTH), _whole(wo), _resident((1, D_MODEL), layer),
                  _whole(wq)],
        out_specs=[row(D_MODEL), row(D_MODEL)],
        out_shape=[jax.ShapeDtypeStruct((T_S, D_MODEL), F32)] * 2,
        name="mix_out_q_sample",
    )(x1, conv, ret, wo, gca, wq)


def _post_b_s_kernel(x2_ref, o_ref, wco_ref, g2_ref, w1_ref, w3_ref, w2_ref, gf_ref, y_ref, *, final):
    y_ref[...] = _finish(x2_ref[...], o_ref[...].astype(BF16), wco_ref, g2_ref, w1_ref, w3_ref, w2_ref,
                         gf_ref, final)


def _post_b_s(x2, o, layer, wco, g2, w1, w3, w2, gf, final):
    row = lambda: pl.BlockSpec((TM, D_MODEL), lambda i: (i, 0))
    return pl.pallas_call(
        functools.partial(_post_b_s_kernel, final=final),
        grid=(T_S // TM,),
        in_specs=[row(), row(), _whole(wco), _resident((1, D_MODEL), layer), _whole(w1), _whole(w3), _whole(w2),
                  pl.BlockSpec((1, D_MODEL), lambda i: (0, 0))],
        out_specs=row(),
        out_shape=jax.ShapeDtypeStruct((T_S, D_MODEL), F32),
        compiler_params=pltpu.CompilerParams(vmem_limit_bytes=VMEM_LIMIT),
        name="attn_out_ffn_sample",
    )(x2, o, wco, g2, w1, w3, w2, gf)


def _attn_s_kernel(q_ref, k_ref, v_ref, o_ref):
    nh = CA_HEADS * SUBLANES
    lane = lax.broadcasted_iota(jnp.int32, (nh, KV_ROWS), 1)
    row = lax.broadcasted_iota(jnp.int32, (nh, KV_ROWS), 0)
    valid = (lane % (2 * CA_HEADS)) == (row // SUBLANES)
    second = lax.broadcasted_iota(jnp.int32, (2 * nh, LANES), 0) % SUBLANES >= DEC_SEQ
    for pr in range(ATT_S_BT // PAIR):
        q = q_ref[pr * SUBLANES:(pr + 1) * SUBLANES, :]
        blocks = [q[:, (h * 2 + half) * LANES:(h * 2 + half + 1) * LANES]
                  for half in range(2) for h in range(CA_HEADS)]
        qt = jnp.concatenate(blocks, axis=0).astype(BF16)
        outs = []
        for x in range(PAIR):
            b = pr * PAIR + x
            st = _dot_nt(qt, k_ref[0, b].astype(BF16))
            s = (st[:nh] + pltpu.roll(st[nh:], KV_ROWS - CA_HEADS, 1)) * (CA_HEAD_DIM ** -0.5)
            p = _softmax_rows(jnp.where(valid, s, NEG_BIG))
            w = jnp.concatenate([p, pltpu.roll(p, CA_HEADS, 1)], axis=0).astype(BF16)
            outs.append(_dot(w, v_ref[0, b].astype(BF16)))
        o = jnp.where(second, outs[1], outs[0])
        for half in range(2):
            for h in range(CA_HEADS):
                r0 = (half * CA_HEADS + h) * SUBLANES
                c0 = (h * 2 + half) * LANES
                o_ref[pr * SUBLANES:(pr + 1) * SUBLANES, c0:c0 + LANES] = o[r0:r0 + SUBLANES]


def _flat_cache(c):
    c = c.reshape(DEPTH, DEC_BATCH, N_MEM, CA_HEADS, 2, LANES)
    return c.transpose(0, 1, 2, 4, 3, 5).reshape(DEPTH, DEC_BATCH, KV_ROWS, LANES)


def _unflat_cache(f):
    f = f.reshape(DEPTH, BATCH, N_MEM, 2, CA_HEADS, LANES)
    return f.transpose(0, 1, 2, 4, 3, 5).reshape(DEPTH, BATCH, N_MEM, CA_HEADS, CA_HEAD_DIM)


def _attn_s(q, cache_k, cache_v, layer):
    bt = ATT_S_BT
    qs = lambda: pl.BlockSpec((bt * DEC_SEQ, D_MODEL), lambda i: (i, 0))
    kv = lambda: pl.BlockSpec((1, bt, KV_ROWS, LANES), lambda i: (layer, i, 0, 0))
    return pl.pallas_call(
        _attn_s_kernel,
        grid=(DEC_BATCH // bt,),
        in_specs=[qs(), kv(), kv()],
        out_specs=qs(),
        out_shape=jax.ShapeDtypeStruct((T_S, D_MODEL), F32),
        compiler_params=pltpu.CompilerParams(vmem_limit_bytes=VMEM_LIMIT),
        name="cross_attn_sample",
    )(q, cache_k, cache_v)


def _rope_tables(pos):
    inv_freq = ROPE_BASE ** (-jnp.arange(0, RET_DK, 2, dtype=F32) / RET_DK)
    ang = pos[:, None] * inv_freq[None, :]
    cos, sin = jnp.cos(ang), jnp.sin(ang)
    return jnp.concatenate([cos, cos], axis=-1), jnp.concatenate([-sin, sin], axis=-1)


def _decay_tables(c):
    log_gamma = jnp.log1p(-jnp.exp2(-5.0 - jnp.arange(RET_HEADS, dtype=F32)))
    idx = jnp.arange(c, dtype=F32)
    rel = idx[:, None] - idx[None, :]
    decay = jnp.where(rel[None] >= 0,
                      jnp.exp(log_gamma[:, None, None] * jnp.maximum(rel, 0.0)[None]), 0.0)
    q_dec = jnp.exp(log_gamma[:, None] * (idx[None, :] + 1.0))
    k_dec = jnp.exp(log_gamma[:, None] * (c - 1.0 - idx[None, :]))
    chunk_dec = jnp.exp(log_gamma * c)
    return decay, q_dec, k_dec, chunk_dec


def _prompt_tables():
    cos2, sin2 = _rope_tables(jnp.arange(SEQ, dtype=F32))
    decay, q_dec, k_dec, chunk_dec = _decay_tables(RET_CHUNK)
    lanes = (RET_HEADS, RET_CHUNK, RET_DK)
    return (cos2, sin2, decay,
            jnp.broadcast_to(q_dec[:, :, None], lanes),
            jnp.broadcast_to(k_dec[:, :, None], lanes),
            jnp.broadcast_to(chunk_dec[:, None, None], lanes))


def _sample_tables():
    slab = lambda a: jnp.concatenate([a] * PAIR, axis=-2)
    cos2, sin2 = _rope_tables(PAST_LEN + jnp.arange(DEC_SEQ, dtype=F32))
    decay, q_dec, k_dec, chunk_dec = _decay_tables(DEC_SEQ)
    rows = (RET_HEADS, DEC_SEQ, RET_DK)
    dec = jnp.broadcast_to(jnp.swapaxes(decay, 1, 2)[:, :, :, None], (RET_HEADS, DEC_SEQ, DEC_SEQ, RET_DK))
    return (slab(cos2), slab(sin2), slab(dec),
            slab(jnp.broadcast_to(q_dec[:, :, None], rows)),
            slab(jnp.broadcast_to(k_dec[:, :, None], rows)),
            jnp.broadcast_to(chunk_dec[:, None, None], (RET_HEADS, RET_DK, RET_DV)))


def kernel(x_prompt, x_sample, state_conv, state_ret, cache_mem_k, cache_mem_v, mem_prompt, g_ffn1, w1_ffn1, w3_ffn1, w2_ffn1, g_mix, w_in, conv_w, conv_b, conv_ln_g, conv_ln_b, ret_gn_g, w_out, g_ca, g_mem, w_cq, w_ck, w_cv, w_co, g_ffn2, w1_ffn2, w3_ffn2, w2_ffn2, g_final):
    vec = lambda g: g.reshape(DEPTH, 1, -1)
    w1a, w3a, w2a, w1b, w3b, w2b = w1_ffn1, w3_ffn1, w2_ffn1, w1_ffn2, w3_ffn2, w2_ffn2
    win, wout, wcq, wck, wcv, wco = w_in, w_out, w_cq, w_ck, w_cv, w_co
    g1, gm, gca, gmem, g2 = map(vec, (g_ffn1, g_mix, g_ca, g_mem, g_ffn2))
    cb, clg, clb, gn = map(vec, (conv_b, conv_ln_g, conv_ln_b, ret_gn_g))
    gf = g_final.reshape(1, D_MODEL)
    cache_k = _flat_cache(cache_mem_k)
    cache_v = _flat_cache(cache_mem_v)
    mem = mem_prompt.reshape(BATCH * N_MEM, D_MODEL)
    tabs_p = _prompt_tables()
    tabs_s = _sample_tables()
    cw8 = conv_w.reshape(DEPTH, CONV_K, CONV_LT, LANES).transpose(0, 2, 1, 3)
    cw8 = jnp.broadcast_to(cw8[:, :, :, None, :], (DEPTH, CONV_LT, CONV_K, SUBLANES, LANES))

    state_t = jnp.swapaxes(state_conv, 1, 2)

    xp = x_prompt.reshape(T_P, D_MODEL)
    xs = x_sample.reshape(T_S, D_MODEL)
    conv_p = jnp.zeros((DEPTH, BATCH, CONV_K - 1, CONV_CH), F32)
    ret_p = jnp.zeros((DEPTH, BATCH, RET_HEADS, RET_DK, RET_DV), F32)
    mem_p = tuple(jnp.zeros((DEPTH, BATCH * KV_ROWS, LANES), F32) for _ in range(2))
    conv_s = jnp.zeros((DEPTH, CONV_K - 1, DEC_BATCH, CONV_CH), F32)
    ret_s = jnp.zeros((DEPTH, DEC_BATCH, RET_HEADS, RET_DK, RET_DV), F32)
    for l in range(DEPTH):
        final = l == DEPTH - 1
        memk_f, memv_f, mk, mv = _memkv(mem, l, gmem, wck, wcv, mem_p)
        mem_p = (memk_f, memv_f)
        x1, u, r, *mix_w = _ffn_inproj(xp, l, g1, gm, (w1a, w3a, w2a, win), cast=True)
        cv, conv_p = _conv_p(u, l, cw8, cb, clg, clb, conv_p)
        rt, ret_p = _ret_p(r, l, tabs_p, gn, ret_p)
        xp, wout_b, wcq_b, wco_b, w1_b, w3_b, w2_b = _post_p(
            x1, cv, rt, mk, mv, l, wout, gca, wcq, wco, g2, w1b, w3b, w2b, gf, final)

        x1, u, r = _ffn_inproj(xs, l, g1, gm, mix_w, cast=False)
        ut = jnp.swapaxes(u.reshape(DEC_BATCH, DEC_SEQ, CONV_CH), 0, 1)
        cv, conv_s = _conv_s(state_t, ut, l, conv_w, cb, clg, clb, conv_s)
        cv = jnp.swapaxes(cv, 0, 1).reshape(T_S, CONV_CH)
        rt, ret_s = _ret_s(r, state_ret, l, tabs_s, gn, ret_s)
        x2, q = _post_a_s(x1, cv, rt, l, wout_b, gca, wcq_b)
        o = _attn_s(q, cache_k, cache_v, l)
        xs = _post_b_s(x2, o, l, wco_b, g2, w1_b, w3_b, w2_b, gf, final)

    return (xp.reshape(BATCH, SEQ, D_MODEL), xs.reshape(DEC_BATCH, DEC_SEQ, D_MODEL),
            conv_p, ret_p, _unflat_cache(mem_p[0]), _unflat_cache(mem_p[1]),
            jnp.swapaxes(conv_s, 1, 2), ret_s)
```

```python
import functools

import jax
import jax.numpy as jnp
from jax import lax
from jax.experimental import pallas as pl
from jax.experimental.pallas import tpu as pltpu

F32 = jnp.float32
BF16 = jnp.bfloat16

D_MODEL = 1024
BATCH = 8
SEQ = 2048
DEPTH = 2
DEC_BATCH = 128
DEC_SEQ = 4
PAST_LEN = 16384
CONV_CH = 512
CONV_K = 31
RET_HEADS = 4
RET_DK = 128
RET_DV = 128
RET_WIDTH = 512
MIX_IN = 3072
RET_CHUNK = 128
ROPE_BASE = 10000.0
D_FF = 4096
N_MEM = 256
CA_HEADS = 4
CA_HEAD_DIM = 256
EPS = 1e-6
GN_EPS = 1e-5

T_P = BATCH * SEQ
T_S = DEC_BATCH * DEC_SEQ

TM = 512
TF = 1024
CONV_TT = 512
CONV_RC = 64
CONV_HALO = 32
SUBLANES = 8
LANES = 128
CONV_ROWS = CONV_HALO + CONV_TT
CONV_LT = CONV_CH // LANES
CONV_NORM_UNROLL = 4
RET_CPS = 4
RET_PB = 2
PAIR = SUBLANES // DEC_SEQ
RET_S_BT = 8
ATT_S_BT = 4
KV_ROWS = N_MEM * 2 * CA_HEADS
NEG_BIG = -1e30
CONV_S_BT = 32
STAGE_BYTES = 512 * 1024
STAGE_SLOTS = 8
VMEM_LIMIT = 60 * 1024 * 1024


def _dot(a, b):
    return jnp.dot(a, b, preferred_element_type=F32)


def _dot_nt(a, b):
    return lax.dot_general(a, b, (((1,), (1,)), ((), ())), preferred_element_type=F32)


def _rms(x, g):
    return x * lax.rsqrt(jnp.mean(x * x, axis=-1, keepdims=True) + EPS) * g


def _silu(x):
    return x * jax.nn.sigmoid(x)


def _ffn(x, g_ref, w1_ref, w3_ref, w2_ref):
    xn = _rms(x, g_ref[...]).astype(BF16)
    acc = None
    for c in range(D_FF // TF):
        sl = slice(c * TF, (c + 1) * TF)
        h1 = _dot(xn, w1_ref[:, sl])
        h3 = _dot(xn, w3_ref[:, sl])
        a = (_silu(h1) * h3).astype(BF16)
        part = _dot(a, w2_ref[sl, :])
        acc = part if acc is None else acc + part
    return x + 0.5 * acc


def _softmax_rows(s):
    m = jnp.max(s, axis=-1, keepdims=True)
    e = jnp.exp(s - m)
    return e * (1.0 / jnp.sum(e, axis=-1, keepdims=True))


def _group_norm_gate(o, gate, gn):
    mu = jnp.mean(o, axis=-1, keepdims=True)
    d = o - mu
    var = jnp.mean(d * d, axis=-1, keepdims=True)
    return _silu(gate) * (d * lax.rsqrt(var + GN_EPS) * gn)


def _rotary(t, cos2, sin2):
    return t * cos2 + pltpu.roll(t, RET_DK // 2, 1) * sin2


def _ffn_inproj_kernel(x_ref, g1_ref, w1_ref, w3_ref, w2_ref, gm_ref, win_ref,
                       x1_ref, u_ref, r_ref):
    x1 = _ffn(x_ref[...], g1_ref, w1_ref, w3_ref, w2_ref)
    x1_ref[...] = x1
    h = _rms(x1, gm_ref[...]).astype(BF16)
    proj = _dot(h, win_ref[...])
    u_ref[...] = proj[:, :CONV_CH] * jax.nn.sigmoid(proj[:, CONV_CH:2 * CONV_CH])
    r_ref[...] = proj[:, 2 * CONV_CH:]


def _resident(shape, layer):
    nd = len(shape)
    return pl.BlockSpec((None,) + tuple(shape), lambda *_: (layer,) + (0,) * nd,
                        pipeline_mode=pl.Buffered(1))


def _whole(a):
    return pl.BlockSpec(a.shape, lambda *_: (0,) * a.ndim, pipeline_mode=pl.Buffered(1))


def _stream_cast(w_hbm, layer, dst_ref):
    rows, cols = dst_ref.shape
    ch = 1 << ((STAGE_BYTES // (4 * cols)).bit_length() - 1)
    assert ch % SUBLANES == 0 and rows % ch == 0
    n = rows // ch

    ahead = min(STAGE_SLOTS - 1, n)

    def run(stage, sem):
        def chunk_copy(k):
            slot = lax.rem(k, STAGE_SLOTS)
            return pltpu.make_async_copy(w_hbm.at[layer, pl.ds(k * ch, ch), :], stage.at[slot], sem.at[slot])

        for k in range(ahead):
            chunk_copy(k).start()

        def body(k, carry):
            @pl.when(k + ahead < n)
            def _():
                chunk_copy(k + ahead).start()

            chunk_copy(k).wait()
            dst_ref[pl.ds(pl.multiple_of(k * ch, ch), ch), :] = stage[lax.rem(k, STAGE_SLOTS)].astype(BF16)
            return carry

        lax.fori_loop(0, n, body, 0)

    pl.run_scoped(run, pltpu.VMEM((STAGE_SLOTS, ch, cols), F32), pltpu.SemaphoreType.DMA((STAGE_SLOTS,)))


def _weights_call(body, layer, weights, emit=False, **kw):
    idx = sorted(weights)
    nw = len(idx)
    in_specs = list(kw.pop("in_specs"))
    for i in idx:
        in_specs[i] = pl.BlockSpec(memory_space=pl.ANY)
    scratch = list(kw.pop("scratch_shapes", ())) + [pltpu.VMEM(weights[i], BF16) for i in idx]
    out_specs, out_shape = list(kw.pop("out_specs")), list(kw.pop("out_shape"))
    n_io = len(in_specs) + len(out_specs)
    if emit:
        assert kw["grid"][0] >= 2
        out_specs += [pl.BlockSpec(memory_space=pl.ANY)] * nw
        out_shape += [jax.ShapeDtypeStruct(weights[i], BF16) for i in idx]
        scratch.append(pltpu.SemaphoreType.DMA((nw,)))

    def with_weights(*refs):
        refs = list(refs)
        sem = refs.pop() if emit else None
        slabs = refs[len(refs) - nw:]
        del refs[len(refs) - nw:]
        copies = []
        if emit:
            copies = [pltpu.make_async_copy(slab, out, sem.at[k])
                      for k, (slab, out) in enumerate(zip(slabs, refs[n_io:n_io + nw]))]
            del refs[n_io:n_io + nw]

        @pl.when(pl.program_id(0) == 0)
        def _():
            for i, slab in zip(idx, slabs):
                _stream_cast(refs[i], layer, slab)

        if emit:
            @pl.when(pl.program_id(0) == 1)
            def _():
                for c in copies:
                    c.start()

        for i, slab in zip(idx, slabs):
            refs[i] = slab
        body(*refs)

        if emit:
            @pl.when(pl.program_id(0) == pl.num_programs(0) - 1)
            def _():
                for c in copies:
                    c.wait()

    return pl.pallas_call(with_weights, in_specs=in_specs, out_specs=out_specs, out_shape=out_shape,
                          scratch_shapes=scratch, **kw)


def _layered_call(body, n_in, prev, make_call=pl.pallas_call, **kw):
    idx = sorted(prev)
    kw["in_specs"] = list(kw["in_specs"]) + [pl.BlockSpec(memory_space=pl.ANY)] * len(idx)

    def with_aliased(*refs):
        return body(*refs[:n_in], *refs[n_in + len(idx):])

    call = make_call(with_aliased, input_output_aliases={n_in + k: o for k, o in enumerate(idx)}, **kw)
    return lambda *args: call(*args, *[prev[o] for o in idx])


def _ffn_inproj(x, layer, g1, gm, w, cast):
    t = x.shape[0]
    row = lambda n: pl.BlockSpec((TM, n), lambda i: (i, 0))
    kw = dict(
        grid=(t // TM,),
        out_specs=[row(D_MODEL), row(CONV_CH), row(MIX_IN - 2 * CONV_CH)],
        out_shape=[jax.ShapeDtypeStruct((t, D_MODEL), F32),
                   jax.ShapeDtypeStruct((t, CONV_CH), F32),
                   jax.ShapeDtypeStruct((t, MIX_IN - 2 * CONV_CH), F32)],
        compiler_params=pltpu.CompilerParams(vmem_limit_bytes=VMEM_LIMIT),
        name="ffn_inproj")
    vec = _resident((1, D_MODEL), layer)
    args = (x, g1, w[0], w[1], w[2], gm, w[3])
    if not cast:
        return pl.pallas_call(
            _ffn_inproj_kernel,
            in_specs=[row(D_MODEL), vec, _whole(w[0]), _whole(w[1]), _whole(w[2]), vec, _whole(w[3])], **kw)(*args)
    return _weights_call(
        _ffn_inproj_kernel, layer,
        {2: (D_MODEL, D_FF), 3: (D_MODEL, D_FF), 4: (D_FF, D_MODEL), 6: (D_MODEL, MIX_IN)}, emit=True,
        in_specs=[row(D_MODEL), vec, None, None, None, vec, None], **kw)(*args)


def _layer_norm_silu(y, lg, lb):
    mu = jnp.mean(y, axis=-1, keepdims=True)
    d = y - mu
    var = jnp.mean(d * d, axis=-1, keepdims=True)
    return _silu(d * lax.rsqrt(var + EPS) * lg + lb)


def _conv_p_kernel(u_ref, w_ref, b_ref, lg_ref, lb_ref, y_ref, nb_ref, win_ref, acc_ref):
    j = pl.program_id(1)
    for t in range(CONV_LT):
        @pl.when(j == 0)
        def _():
            win_ref[0, t, 0:CONV_HALO, :] = jnp.zeros((CONV_HALO, LANES), F32)

        @pl.when(j > 0)
        def _():
            win_ref[0, t, 0:CONV_HALO, :] = win_ref[0, t, CONV_TT:CONV_TT + CONV_HALO, :]

        win_ref[0, t, CONV_HALO:CONV_ROWS, :] = u_ref[:, t * LANES:(t + 1) * LANES]
        win_ref[0, t, CONV_ROWS:CONV_ROWS + SUBLANES, :] = jnp.zeros((SUBLANES, LANES), F32)
        for sh in range(1, SUBLANES):
            win_ref[sh, t, 0:CONV_ROWS, :] = win_ref[0, t, sh:sh + CONV_ROWS, :]

    first = CONV_HALO - (CONV_K - 1)
    n_rc = CONV_TT // CONV_RC

    def taps(idx, carry):
        t = idx // n_rc
        r0 = pl.multiple_of((idx % n_rc) * CONV_RC, CONV_RC)
        accs = [None] * (CONV_RC // SUBLANES)
        for k in range(CONV_K):
            sh = (first + k) % SUBLANES
            wk = w_ref[t, k]
            for i in range(CONV_RC // SUBLANES):
                a0 = pl.multiple_of(r0 + (first + k - sh) + i * SUBLANES, SUBLANES)
                term = wk * win_ref[sh, t, pl.ds(a0, SUBLANES), :]
                accs[i] = term if accs[i] is None else accs[i] + term
        acc_ref[t, pl.ds(r0, CONV_RC), :] = jnp.concatenate(accs, axis=0)
        return carry

    lax.fori_loop(0, CONV_LT * n_rc, taps, 0)

    def norm(c, carry):
        r0 = pl.multiple_of(c * CONV_RC, CONV_RC)
        y = jnp.concatenate([acc_ref[t, pl.ds(r0, CONV_RC), :] for t in range(CONV_LT)], axis=-1) + b_ref[...]
        y_ref[pl.ds(r0, CONV_RC), :] = _layer_norm_silu(y, lg_ref[...], lb_ref[...])
        return carry

    lax.fori_loop(0, n_rc, norm, 0, unroll=CONV_NORM_UNROLL)

    @pl.when(j == pl.num_programs(1) - 1)
    def _():
        nb_ref[0, 0] = u_ref[CONV_TT - (CONV_K - 1):CONV_TT, :]


def _conv_p(u, layer, conv_w, conv_b, ln_g, ln_b, prev_nb):
    nt = SEQ // CONV_TT
    vec = lambda: pl.BlockSpec((None, 1, CONV_CH), lambda b, j: (layer, 0, 0))
    return _layered_call(
        _conv_p_kernel, 5, {1: prev_nb},
        grid=(BATCH, nt),
        in_specs=[pl.BlockSpec((CONV_TT, CONV_CH), lambda b, j: (b * nt + j, 0)),
                  pl.BlockSpec((None, CONV_LT, CONV_K, SUBLANES, LANES), lambda b, j: (layer, 0, 0, 0, 0)),
                  vec(), vec(), vec()],
        out_specs=[pl.BlockSpec((CONV_TT, CONV_CH), lambda b, j: (b * nt + j, 0)),
                   pl.BlockSpec((1, 1, CONV_K - 1, CONV_CH), lambda b, j: (layer, b, 0, 0))],
        out_shape=[jax.ShapeDtypeStruct((T_P, CONV_CH), F32),
                   jax.ShapeDtypeStruct((DEPTH, BATCH, CONV_K - 1, CONV_CH), F32)],
        scratch_shapes=[pltpu.VMEM((SUBLANES, CONV_LT, CONV_ROWS + SUBLANES, LANES), F32),
                        pltpu.VMEM((CONV_LT, CONV_TT, LANES), F32)],
        name="conv_prompt",
    )(u, conv_w, conv_b, ln_g, ln_b)


def _conv_s_kernel(st_ref, ut_ref, w_ref, b_ref, lg_ref, lb_ref, y_ref, nb_ref):
    keep = CONV_K - 1 - DEC_SEQ
    for t in range(DEC_SEQ):
        y = None
        for j in range(t, CONV_K - 1):
            term = st_ref[0, j] * w_ref[j - t:j - t + 1, :]
            y = term if y is None else y + term
        for j in range(t + 1):
            k = CONV_K - 1 - t + j
            y = y + ut_ref[j] * w_ref[k:k + 1, :]
        y_ref[t] = _layer_norm_silu(y + b_ref[...], lg_ref[...], lb_ref[...])
    for j in range(keep):
        nb_ref[0, j] = st_ref[0, j + DEC_SEQ]
    for t in range(DEC_SEQ):
        nb_ref[0, keep + t] = ut_ref[t]


def _conv_s(state_t, ut, layer, conv_w, conv_b, ln_g, ln_b, prev_nb):
    bt = CONV_S_BT
    vec = lambda: pl.BlockSpec((None, 1, CONV_CH), lambda i: (layer, 0, 0))
    return _layered_call(
        _conv_s_kernel, 6, {1: prev_nb},
        grid=(DEC_BATCH // bt,),
        in_specs=[pl.BlockSpec((1, CONV_K - 1, bt, CONV_CH), lambda i: (layer, 0, i, 0)),
                  pl.BlockSpec((DEC_SEQ, bt, CONV_CH), lambda i: (0, i, 0)),
                  pl.BlockSpec((None, CONV_K, CONV_CH), lambda i: (layer, 0, 0)),
                  vec(), vec(), vec()],
        out_specs=[pl.BlockSpec((DEC_SEQ, bt, CONV_CH), lambda i: (0, i, 0)),
                   pl.BlockSpec((1, CONV_K - 1, bt, CONV_CH), lambda i: (layer, 0, i, 0))],
        out_shape=[jax.ShapeDtypeStruct((DEC_SEQ, DEC_BATCH, CONV_CH), F32),
                   jax.ShapeDtypeStruct((DEPTH, CONV_K - 1, DEC_BATCH, CONV_CH), F32)],
        name="conv_sample",
    )(state_t, ut, conv_w, conv_b, ln_g, ln_b)


def _ret_p_kernel(q_ref, k_ref, v_ref, g_ref, cos_ref, sin_ref, dec_ref, qd_ref, kd_ref,
                  cd_ref, gn_ref, o_ref, ns_ref, s_ref):
    j = pl.program_id(1)

    @pl.when(j == 0)
    def _():
        s_ref[...] = jnp.zeros_like(s_ref)

    units = [(e, h, slice(h * RET_DK, (h + 1) * RET_DK)) for h in range(RET_HEADS) for e in range(RET_PB)]
    chunks = []
    for c in range(RET_CPS):
        rows = slice(c * RET_CHUNK, (c + 1) * RET_CHUNK)
        cos2 = cos_ref[rows, :]
        sin2 = sin_ref[rows, :]
        qs = [_rotary(q_ref[e, rows, sl], cos2, sin2).astype(BF16) for e, h, sl in units]
        ks = [_rotary(k_ref[e, rows, sl], cos2, sin2) * (RET_DK ** -0.5) for e, h, sl in units]
        vs = [v_ref[e, rows, sl].astype(BF16) for e, h, sl in units]
        scores = [(_dot_nt(qb, kh.astype(BF16)) * dec_ref[h]).astype(BF16)
                  for qb, kh, (e, h, sl) in zip(qs, ks, units)]
        intra = [_dot(sc, vb) for sc, vb in zip(scores, vs)]
        upd = [lax.dot_general((kh * kd_ref[h]).astype(BF16), vb, (((0,), (0,)), ((), ())),
                               preferred_element_type=F32) for kh, vb, (e, h, sl) in zip(ks, vs, units)]
        chunks.append((rows, qs, intra, upd))

    states = [s_ref[e, h] for e, h, sl in units]
    outs = []
    for rows, qs, intra, upd in chunks:
        outs.append([a + _dot(qb, s_h.astype(BF16)) * qd_ref[h]
                     for a, qb, s_h, (e, h, sl) in zip(intra, qs, states, units)])
        states = [s_h * cd_ref[h] + u for s_h, u, (e, h, sl) in zip(states, upd, units)]

    for (rows, _, _, _), out in zip(chunks, outs):
        for o, (e, h, sl) in zip(out, units):
            o_ref[e, rows, sl] = _group_norm_gate(o, g_ref[e, rows, sl], gn_ref[:, sl])
    for s_h, (e, h, sl) in zip(states, units):
        s_ref[e, h] = s_h

    @pl.when(j == pl.num_programs(1) - 1)
    def _():
        ns_ref[0] = s_ref[...]


def _ret_p(r, layer, tabs, gn, prev_ns):
    cos2, sin2, dec, qd, kd, cd = tabs
    rows = RET_CPS * RET_CHUNK
    r3 = r.reshape(BATCH, SEQ, r.shape[-1])
    col = lambda c: pl.BlockSpec((RET_PB, rows, RET_WIDTH), lambda b, j: (b, j, c))
    const = lambda: pl.BlockSpec((RET_HEADS, RET_CHUNK, RET_DK), lambda b, j: (0, 0, 0))
    tab = lambda: pl.BlockSpec((rows, RET_DK), lambda b, j: (j, 0))
    out, ns = _layered_call(
        _ret_p_kernel, 11, {1: prev_ns},
        grid=(BATCH // RET_PB, SEQ // rows),
        in_specs=[col(0), col(1), col(2), col(3), tab(), tab(), const(), const(), const(), const(),
                  pl.BlockSpec((None, 1, RET_WIDTH), lambda b, j: (layer, 0, 0))],
        out_specs=[pl.BlockSpec((RET_PB, rows, RET_WIDTH), lambda b, j: (b, j, 0)),
                   pl.BlockSpec((1, RET_PB, RET_HEADS, RET_DK, RET_DV), lambda b, j: (layer, b, 0, 0, 0))],
        out_shape=[jax.ShapeDtypeStruct((BATCH, SEQ, RET_WIDTH), F32),
                   jax.ShapeDtypeStruct((DEPTH, BATCH, RET_HEADS, RET_DK, RET_DV), F32)],
        scratch_shapes=[pltpu.VMEM((RET_PB, RET_HEADS, RET_DK, RET_DV), F32)],
        name="retention_prompt",
    )(r3, r3, r3, r3, cos2, sin2, dec, qd, kd, cd, gn)
    return out.reshape(T_P, RET_WIDTH), ns


def _ret_s_kernel(q_ref, k_ref, v_ref, g_ref, st_ref, cos_ref, sin_ref, dec_ref, qd_ref, kd_ref,
                  cd_ref, gn_ref, o_ref, ns_ref):
    cos2 = cos_ref[...]
    sin2 = sin_ref[...]
    zpad = jnp.zeros((RET_DK - SUBLANES, RET_DK), F32)
    second = lax.broadcasted_iota(jnp.int32, (SUBLANES, RET_DK), 0) >= DEC_SEQ

    def pick(a, jj):
        return jnp.where(second, a[DEC_SEQ + jj:DEC_SEQ + jj + 1, :], a[jj:jj + 1, :])

    def body(p, carry):
        rows = pl.ds(pl.multiple_of(p * SUBLANES, SUBLANES), SUBLANES)
        for h in range(RET_HEADS):
            sl = slice(h * RET_DK, (h + 1) * RET_DK)
            qh = _rotary(q_ref[rows, sl], cos2, sin2)
            kh = _rotary(k_ref[rows, sl], cos2, sin2) * (RET_DK ** -0.5)
            vh = v_ref[rows, sl]
            qb = qh.astype(BF16)
            kdec = kh * kd_ref[h]
            vpad = jnp.concatenate([vh, zpad], axis=0).astype(BF16)
            from_state = []
            for x in range(PAIR):
                s_x = st_ref[0, p * PAIR + x, h]
                from_state.append(_dot(qb, s_x.astype(BF16)))
                mine = second if x else jnp.logical_not(second)
                kx = jnp.concatenate([jnp.where(mine, kdec, 0.0), zpad], axis=0)
                upd = _dot(kx.T.astype(BF16), vpad)
                ns_ref[0, p * PAIR + x, h] = s_x * cd_ref[h] + upd
            o = jnp.where(second, from_state[1], from_state[0]) * qd_ref[h]
            for jj in range(DEC_SEQ):
                sj = jnp.sum(qh * pick(kh, jj), axis=-1, keepdims=True)
                o = o + (sj * dec_ref[h, jj]) * pick(vh, jj)
            o_ref[rows, sl] = _group_norm_gate(o, g_ref[rows, sl], gn_ref[:, sl])
        return carry

    lax.fori_loop(0, RET_S_BT // PAIR, body, 0, unroll=2)


def _ret_s(r, state_ret, layer, tabs, gn, prev_ns):
    cos2, sin2, dec, qd, kd, cd = tabs
    bt = RET_S_BT
    col = lambda c: pl.BlockSpec((bt * DEC_SEQ, RET_WIDTH), lambda i: (i, c))
    full = lambda a: pl.BlockSpec(a.shape, lambda i: (0,) * a.ndim)
    st = lambda: pl.BlockSpec((1, bt, RET_HEADS, RET_DK, RET_DV), lambda i: (layer, i, 0, 0, 0))
    return _layered_call(
        _ret_s_kernel, 12, {1: prev_ns},
        grid=(DEC_BATCH // bt,),
        in_specs=[col(0), col(1), col(2), col(3), st(), full(cos2), full(sin2), full(dec), full(qd),
                  full(kd), full(cd), pl.BlockSpec((None, 1, RET_WIDTH), lambda i: (layer, 0, 0))],
        out_specs=[pl.BlockSpec((bt * DEC_SEQ, RET_WIDTH), lambda i: (i, 0)), st()],
        out_shape=[jax.ShapeDtypeStruct((T_S, RET_WIDTH), F32),
                   jax.ShapeDtypeStruct((DEPTH, DEC_BATCH, RET_HEADS, RET_DK, RET_DV), F32)],
        name="retention_sample",
    )(r, r, r, r, state_ret, cos2, sin2, dec, qd, kd, cd, gn)


def _memkv_kernel(m_ref, g_ref, wk_ref, wv_ref, kf_ref, vf_ref, kb_ref, vb_ref):
    mn = _rms(m_ref[...], g_ref[...]).astype(BF16)
    for w_ref, flat_ref, b_ref in ((wk_ref, kf_ref, kb_ref), (wv_ref, vf_ref, vb_ref)):
        y = _dot(mn, w_ref[...])
        b_ref[...] = y.astype(BF16)
        for h in range(CA_HEADS):
            for half in range(2):
                c0 = (h * 2 + half) * LANES
                flat_ref[0, pl.ds(half * CA_HEADS + h, N_MEM, stride=2 * CA_HEADS), :] = y[:, c0:c0 + LANES]


def _memkv(mem, layer, g, wk, wv, prev):
    t = mem.shape[0]
    row = lambda: pl.BlockSpec((N_MEM, D_MODEL), lambda i: (i, 0))
    flat = lambda: pl.BlockSpec((1, KV_ROWS, LANES), lambda i: (layer, i, 0))
    flat_shape = jax.ShapeDtypeStruct((DEPTH, BATCH * KV_ROWS, LANES), F32)
    square = (D_MODEL, D_MODEL)
    return _layered_call(
        _memkv_kernel, 4, {0: prev[0], 1: prev[1]},
        make_call=functools.partial(_weights_call, layer=layer, weights={2: square, 3: square}),
        grid=(t // N_MEM,),
        in_specs=[row(), _resident((1, D_MODEL), layer), None, None],
        out_specs=[flat(), flat(), row(), row()],
        out_shape=[flat_shape, flat_shape, jax.ShapeDtypeStruct((t, D_MODEL), BF16),
                   jax.ShapeDtypeStruct((t, D_MODEL), BF16)],
        name="memory_kv",
    )(mem, g, wk, wv)


def _mix_out_q(x1, conv, ret, wo_ref, gca_ref, wq_ref):
    mix = jnp.concatenate([conv, ret], axis=-1).astype(BF16)
    x2 = x1 + _dot(mix, wo_ref[...])
    q = _dot(_rms(x2, gca_ref[...]).astype(BF16), wq_ref[...])
    return x2, q


def _finish(x2, o, wco_ref, g2_ref, w1_ref, w3_ref, w2_ref, gf_ref, final):
    x3 = x2 + _dot(o, wco_ref[...])
    x4 = _ffn(x3, g2_ref, w1_ref, w3_ref, w2_ref)
    return _rms(x4, gf_ref[...]) if final else x4


def _post_p_kernel(x1_ref, conv_ref, ret_ref, mk_ref, mv_ref, wo_ref, gca_ref, wq_ref, wco_ref,
                   g2_ref, w1_ref, w3_ref, w2_ref, gf_ref, y_ref, *, final):
    x2, q = _mix_out_q(x1_ref[...], conv_ref[...], ret_ref[...], wo_ref, gca_ref, wq_ref)
    cols = [slice(h * CA_HEAD_DIM, (h + 1) * CA_HEAD_DIM) for h in range(CA_HEADS)]
    qb = q.astype(BF16)
    scores = [_dot_nt(qb[:, sl], mk_ref[:, sl]) * (CA_HEAD_DIM ** -0.5) for sl in cols]
    probs = [_softmax_rows(s).astype(BF16) for s in scores]
    heads = [_dot(p, mv_ref[:, sl]) for p, sl in zip(probs, cols)]
    o = jnp.concatenate(heads, axis=-1).astype(BF16)
    y_ref[...] = _finish(x2, o, wco_ref, g2_ref, w1_ref, w3_ref, w2_ref, gf_ref, final)


def _post_p(x1, conv, ret, mk, mv, layer, wo, gca, wq, wco, g2, w1, w3, w2, gf, final):
    row = lambda n: pl.BlockSpec((TM, n), lambda i: (i, 0))
    mem = lambda: pl.BlockSpec((N_MEM, D_MODEL), lambda i: (i // (SEQ // TM), 0))
    square = (D_MODEL, D_MODEL)
    return _weights_call(
        functools.partial(_post_p_kernel, final=final), layer,
        {5: square, 7: square, 8: square, 10: (D_MODEL, D_FF), 11: (D_MODEL, D_FF), 12: (D_FF, D_MODEL)},
        emit=True,
        grid=(T_P // TM,),
        in_specs=[row(D_MODEL), row(CONV_CH), row(RET_WIDTH), mem(), mem(),
                  None, _resident((1, D_MODEL), layer), None, None, _resident((1, D_MODEL), layer),
                  None, None, None,
                  pl.BlockSpec((1, D_MODEL), lambda i: (0, 0))],
        out_specs=[row(D_MODEL)],
        out_shape=[jax.ShapeDtypeStruct((T_P, D_MODEL), F32)],
        compiler_params=pltpu.CompilerParams(vmem_limit_bytes=VMEM_LIMIT),
        name="post_prompt",
    )(x1, conv, ret, mk, mv, wo, gca, wq, wco, g2, w1, w3, w2, gf)


def _post_a_s_kernel(x1_ref, conv_ref, ret_ref, wo_ref, gca_ref, wq_ref, x2_ref, q_ref):
    x2, q = _mix_out_q(x1_ref[...], conv_ref[...], ret_ref[...], wo_ref, gca_ref, wq_ref)
    x2_ref[...] = x2
    q_ref[...] = q


def _post_a_s(x1, conv, ret, layer, wo, gca, wq):
    row = lambda n: pl.BlockSpec((TM, n), lambda i: (i, 0))
    return pl.pallas_call(
        _post_a_s_kernel,
        grid=(T_S // TM,),
        in_specs=[row(D_MODEL), row(CONV_CH), row(RET_WIDTH), _whole(wo), _resident((1, D_MODEL), layer),
                  _whole(wq)],
        out_specs=[row(D_MODEL), row(D_MODEL)],
        out_shape=[jax.ShapeDtypeStruct((T_S, D_MODEL), F32)] * 2,
        name="mix_out_q_sample",
    )(x1, conv, ret, wo, gca, wq)


def _post_b_s_kernel(x2_ref, o_ref, wco_ref, g2_ref, w1_ref, w3_ref, w2_ref, gf_ref, y_ref, *, final):
    y_ref[...] = _finish(x2_ref[...], o_ref[...].astype(BF16), wco_ref, g2_ref, w1_ref, w3_ref, w2_ref,
                         gf_ref, final)


def _post_b_s(x2, o, layer, wco, g2, w1, w3, w2, gf, final):
    row = lambda: pl.BlockSpec((TM, D_MODEL), lambda i: (i, 0))
    return pl.pallas_call(
        functools.partial(_post_b_s_kernel, final=final),
        grid=(T_S // TM,),
        in_specs=[row(), row(), _whole(wco), _resident((1, D_MODEL), layer), _whole(w1), _whole(w3), _whole(w2),
                  pl.BlockSpec((1, D_MODEL), lambda i: (0, 0))],
        out_specs=row(),
        out_shape=jax.ShapeDtypeStruct((T_S, D_MODEL), F32),
        compiler_params=pltpu.CompilerParams(vmem_limit_bytes=VMEM_LIMIT),
        name="attn_out_ffn_sample",
    )(x2, o, wco, g2, w1, w3, w2, gf)


def _attn_s_kernel(q_ref, k_ref, v_ref, o_ref):
    nh = CA_HEADS * SUBLANES
    lane = lax.broadcasted_iota(jnp.int32, (nh, KV_ROWS), 1)
    row = lax.broadcasted_iota(jnp.int32, (nh, KV_ROWS), 0)
    valid = (lane % (2 * CA_HEADS)) == (row // SUBLANES)
    second = lax.broadcasted_iota(jnp.int32, (2 * nh, LANES), 0) % SUBLANES >= DEC_SEQ
    batches = range(ATT_S_BT)
    qts = []
    for pr in range(ATT_S_BT // PAIR):
        q = q_ref[pr * SUBLANES:(pr + 1) * SUBLANES, :]
        blocks = [q[:, (h * 2 + half) * LANES:(h * 2 + half + 1) * LANES]
                  for half in range(2) for h in range(CA_HEADS)]
        qts.append(jnp.concatenate(blocks, axis=0).astype(BF16))
    sts = [_dot_nt(qts[b // PAIR], k_ref[0, b].astype(BF16)) for b in batches]
    ws = []
    for st in sts:
        s = (st[:nh] + pltpu.roll(st[nh:], KV_ROWS - CA_HEADS, 1)) * (CA_HEAD_DIM ** -0.5)
        p = _softmax_rows(jnp.where(valid, s, NEG_BIG))
        ws.append(jnp.concatenate([p, pltpu.roll(p, CA_HEADS, 1)], axis=0).astype(BF16))
    outs = [_dot(ws[b], v_ref[0, b].astype(BF16)) for b in batches]
    for pr in range(ATT_S_BT // PAIR):
        o = jnp.where(second, outs[pr * PAIR + 1], outs[pr * PAIR])
        for half in range(2):
            for h in range(CA_HEADS):
                r0 = (half * CA_HEADS + h) * SUBLANES
                c0 = (h * 2 + half) * LANES
                o_ref[pr * SUBLANES:(pr + 1) * SUBLANES, c0:c0 + LANES] = o[r0:r0 + SUBLANES]


def _flat_cache(c):
    c = c.reshape(DEPTH, DEC_BATCH, N_MEM, CA_HEADS, 2, LANES)
    return c.transpose(0, 1, 2, 4, 3, 5).reshape(DEPTH, DEC_BATCH, KV_ROWS, LANES)


def _unflat_cache(f):
    f = f.reshape(DEPTH, BATCH, N_MEM, 2, CA_HEADS, LANES)
    return f.transpose(0, 1, 2, 4, 3, 5).reshape(DEPTH, BATCH, N_MEM, CA_HEADS, CA_HEAD_DIM)


def _attn_s(q, cache_k, cache_v, layer):
    bt = ATT_S_BT
    qs = lambda: pl.BlockSpec((bt * DEC_SEQ, D_MODEL), lambda i: (i, 0))
    kv = lambda: pl.BlockSpec((1, bt, KV_ROWS, LANES), lambda i: (layer, i, 0, 0))
    return pl.pallas_call(
        _attn_s_kernel,
        grid=(DEC_BATCH // bt,),
        in_specs=[qs(), kv(), kv()],
        out_specs=qs(),
        out_shape=jax.ShapeDtypeStruct((T_S, D_MODEL), F32),
        compiler_params=pltpu.CompilerParams(vmem_limit_bytes=VMEM_LIMIT),
        name="cross_attn_sample",
    )(q, cache_k, cache_v)


def _rope_tables(pos):
    inv_freq = ROPE_BASE ** (-jnp.arange(0, RET_DK, 2, dtype=F32) / RET_DK)
    ang = pos[:, None] * inv_freq[None, :]
    cos, sin = jnp.cos(ang), jnp.sin(ang)
    return jnp.concatenate([cos, cos], axis=-1), jnp.concatenate([-sin, sin], axis=-1)


def _decay_tables(c):
    log_gamma = jnp.log1p(-jnp.exp2(-5.0 - jnp.arange(RET_HEADS, dtype=F32)))
    idx = jnp.arange(c, dtype=F32)
    rel = idx[:, None] - idx[None, :]
    decay = jnp.where(rel[None] >= 0,
                      jnp.exp(log_gamma[:, None, None] * jnp.maximum(rel, 0.0)[None]), 0.0)
    q_dec = jnp.exp(log_gamma[:, None] * (idx[None, :] + 1.0))
    k_dec = jnp.exp(log_gamma[:, None] * (c - 1.0 - idx[None, :]))
    chunk_dec = jnp.exp(log_gamma * c)
    return decay, q_dec, k_dec, chunk_dec


def _prompt_tables():
    cos2, sin2 = _rope_tables(jnp.arange(SEQ, dtype=F32))
    decay, q_dec, k_dec, chunk_dec = _decay_tables(RET_CHUNK)
    lanes = (RET_HEADS, RET_CHUNK, RET_DK)
    return (cos2, sin2, decay,
            jnp.broadcast_to(q_dec[:, :, None], lanes),
            jnp.broadcast_to(k_dec[:, :, None], lanes),
            jnp.broadcast_to(chunk_dec[:, None, None], lanes))


def _sample_tables():
    slab = lambda a: jnp.concatenate([a] * PAIR, axis=-2)
    cos2, sin2 = _rope_tables(PAST_LEN + jnp.arange(DEC_SEQ, dtype=F32))
    decay, q_dec, k_dec, chunk_dec = _decay_tables(DEC_SEQ)
    rows = (RET_HEADS, DEC_SEQ, RET_DK)
    dec = jnp.broadcast_to(jnp.swapaxes(decay, 1, 2)[:, :, :, None], (RET_HEADS, DEC_SEQ, DEC_SEQ, RET_DK))
    return (slab(cos2), slab(sin2), slab(dec),
            slab(jnp.broadcast_to(q_dec[:, :, None], rows)),
            slab(jnp.broadcast_to(k_dec[:, :, None], rows)),
            jnp.broadcast_to(chunk_dec[:, None, None], (RET_HEADS, RET_DK, RET_DV)))


def kernel(x_prompt, x_sample, state_conv, state_ret, cache_mem_k, cache_mem_v, mem_prompt, g_ffn1, w1_ffn1, w3_ffn1, w2_ffn1, g_mix, w_in, conv_w, conv_b, conv_ln_g, conv_ln_b, ret_gn_g, w_out, g_ca, g_mem, w_cq, w_ck, w_cv, w_co, g_ffn2, w1_ffn2, w3_ffn2, w2_ffn2, g_final):
    vec = lambda g: g.reshape(DEPTH, 1, -1)
    w1a, w3a, w2a, w1b, w3b, w2b = w1_ffn1, w3_ffn1, w2_ffn1, w1_ffn2, w3_ffn2, w2_ffn2
    win, wout, wcq, wck, wcv, wco = w_in, w_out, w_cq, w_ck, w_cv, w_co
    g1, gm, gca, gmem, g2 = map(vec, (g_ffn1, g_mix, g_ca, g_mem, g_ffn2))
    cb, clg, clb, gn = map(vec, (conv_b, conv_ln_g, conv_ln_b, ret_gn_g))
    gf = g_final.reshape(1, D_MODEL)
    cache_k = _flat_cache(cache_mem_k)
    cache_v = _flat_cache(cache_mem_v)
    mem = mem_prompt.reshape(BATCH * N_MEM, D_MODEL)
    tabs_p = _prompt_tables()
    tabs_s = _sample_tables()
    cw8 = conv_w.reshape(DEPTH, CONV_K, CONV_LT, LANES).transpose(0, 2, 1, 3)
    cw8 = jnp.broadcast_to(cw8[:, :, :, None, :], (DEPTH, CONV_LT, CONV_K, SUBLANES, LANES))

    state_t = jnp.swapaxes(state_conv, 1, 2)

    xp = x_prompt.reshape(T_P, D_MODEL)
    xs = x_sample.reshape(T_S, D_MODEL)
    conv_p = jnp.zeros((DEPTH, BATCH, CONV_K - 1, CONV_CH), F32)
    ret_p = jnp.zeros((DEPTH, BATCH, RET_HEADS, RET_DK, RET_DV), F32)
    mem_p = tuple(jnp.zeros((DEPTH, BATCH * KV_ROWS, LANES), F32) for _ in range(2))
    conv_s = jnp.zeros((DEPTH, CONV_K - 1, DEC_BATCH, CONV_CH), F32)
    ret_s = jnp.zeros((DEPTH, DEC_BATCH, RET_HEADS, RET_DK, RET_DV), F32)
    for l in range(DEPTH):
        final = l == DEPTH - 1
        memk_f, memv_f, mk, mv = _memkv(mem, l, gmem, wck, wcv, mem_p)
        mem_p = (memk_f, memv_f)
        x1, u, r, *mix_w = _ffn_inproj(xp, l, g1, gm, (w1a, w3a, w2a, win), cast=True)
        cv, conv_p = _conv_p(u, l, cw8, cb, clg, clb, conv_p)
        rt, ret_p = _ret_p(r, l, tabs_p, gn, ret_p)
        xp, wout_b, wcq_b, wco_b, w1_b, w3_b, w2_b = _post_p(
            x1, cv, rt, mk, mv, l, wout, gca, wcq, wco, g2, w1b, w3b, w2b, gf, final)

        x1, u, r = _ffn_inproj(xs, l, g1, gm, mix_w, cast=False)
        ut = jnp.swapaxes(u.reshape(DEC_BATCH, DEC_SEQ, CONV_CH), 0, 1)
        cv, conv_s = _conv_s(state_t, ut, l, conv_w, cb, clg, clb, conv_s)
        cv = jnp.swapaxes(cv, 0, 1).reshape(T_S, CONV_CH)
        rt, ret_s = _ret_s(r, state_ret, l, tabs_s, gn, ret_s)
        x2, q = _post_a_s(x1, cv, rt, l, wout_b, gca, wcq_b)
        o = _attn_s(q, cache_k, cache_v, l)
        xs = _post_b_s(x2, o, l, wco_b, g2, w1_b, w3_b, w2_b, gf, final)

    return (xp.reshape(BATCH, SEQ, D_MODEL), xs.reshape(DEC_BATCH, DEC_SEQ, D_MODEL),
            conv_p, ret_p, _unflat_cache(mem_p[0]), _unflat_cache(mem_p[1]),
            jnp.swapaxes(conv_s, 1, 2), ret_s)
```

```python
import functools

import jax
import jax.numpy as jnp
from jax import lax
from jax.experimental import pallas as pl
from jax.experimental.pallas import tpu as pltpu

F32 = jnp.float32
BF16 = jnp.bfloat16

D_MODEL = 1024
BATCH = 8
SEQ = 2048
DEPTH = 2
DEC_BATCH = 128
DEC_SEQ = 4
PAST_LEN = 16384
CONV_CH = 512
CONV_K = 31
RET_HEADS = 4
RET_DK = 128
RET_DV = 128
RET_WIDTH = 512
MIX_IN = 3072
RET_CHUNK = 128
ROPE_BASE = 10000.0
D_FF = 4096
N_MEM = 256
CA_HEADS = 4
CA_HEAD_DIM = 256
EPS = 1e-6
GN_EPS = 1e-5

T_P = BATCH * SEQ
T_S = DEC_BATCH * DEC_SEQ

TM = 512
TF = 1024
CONV_TT = 512
CONV_RC = 64
CONV_HALO = 32
SUBLANES = 8
LANES = 128
CONV_ROWS = CONV_HALO + CONV_TT
CONV_LT = CONV_CH // LANES
CONV_NORM_UNROLL = 4
RET_CPS = 4
RET_PB = 2
PAIR = SUBLANES // DEC_SEQ
RET_S_BT = 8
ATT_S_BT = 8
KV_ROWS = N_MEM * 2 * CA_HEADS
NEG_BIG = -1e30
CONV_S_BT = 32
STAGE_BYTES = 512 * 1024
STAGE_SLOTS = 8
VMEM_LIMIT = 60 * 1024 * 1024


def _dot(a, b):
    return jnp.dot(a, b, preferred_element_type=F32)


def _dot_nt(a, b):
    return lax.dot_general(a, b, (((1,), (1,)), ((), ())), preferred_element_type=F32)


def _rms(x, g):
    return x * lax.rsqrt(jnp.mean(x * x, axis=-1, keepdims=True) + EPS) * g


def _silu(x):
    return x * jax.nn.sigmoid(x)


def _ffn(x, g_ref, w1_ref, w3_ref, w2_ref):
    xn = _rms(x, g_ref[...]).astype(BF16)
    acc = None
    for c in range(D_FF // TF):
        sl = slice(c * TF, (c + 1) * TF)
        h1 = _dot(xn, w1_ref[:, sl])
        h3 = _dot(xn, w3_ref[:, sl])
        a = (_silu(h1) * h3).astype(BF16)
        part = _dot(a, w2_ref[sl, :])
        acc = part if acc is None else acc + part
    return x + 0.5 * acc


def _softmax_rows(s):
    m = jnp.max(s, axis=-1, keepdims=True)
    e = jnp.exp(s - m)
    return e * (1.0 / jnp.sum(e, axis=-1, keepdims=True))


def _group_norm_gate(o, gate, gn):
    mu = jnp.mean(o, axis=-1, keepdims=True)
    d = o - mu
    var = jnp.mean(d * d, axis=-1, keepdims=True)
    return _silu(gate) * (d * lax.rsqrt(var + GN_EPS) * gn)


def _rotary(t, cos2, sin2):
    return t * cos2 + pltpu.roll(t, RET_DK // 2, 1) * sin2


def _ffn_inproj_kernel(x_ref, g1_ref, w1_ref, w3_ref, w2_ref, gm_ref, win_ref,
                       x1_ref, u_ref, r_ref):
    x1 = _ffn(x_ref[...], g1_ref, w1_ref, w3_ref, w2_ref)
    x1_ref[...] = x1
    h = _rms(x1, gm_ref[...]).astype(BF16)
    proj = _dot(h, win_ref[...])
    u_ref[...] = proj[:, :CONV_CH] * jax.nn.sigmoid(proj[:, CONV_CH:2 * CONV_CH])
    r_ref[...] = proj[:, 2 * CONV_CH:]


def _resident(shape, layer):
    nd = len(shape)
    return pl.BlockSpec((None,) + tuple(shape), lambda *_: (layer,) + (0,) * nd,
                        pipeline_mode=pl.Buffered(1))


def _whole(a):
    return pl.BlockSpec(a.shape, lambda *_: (0,) * a.ndim, pipeline_mode=pl.Buffered(1))


def _stream_cast(w_hbm, layer, dst_ref):
    rows, cols = dst_ref.shape
    ch = 1 << ((STAGE_BYTES // (4 * cols)).bit_length() - 1)
    assert ch % SUBLANES == 0 and rows % ch == 0
    n = rows // ch

    ahead = min(STAGE_SLOTS - 1, n)

    def run(stage, sem):
        def chunk_copy(k):
            slot = lax.rem(k, STAGE_SLOTS)
            return pltpu.make_async_copy(w_hbm.at[layer, pl.ds(k * ch, ch), :], stage.at[slot], sem.at[slot])

        for k in range(ahead):
            chunk_copy(k).start()

        def body(k, carry):
            @pl.when(k + ahead < n)
            def _():
                chunk_copy(k + ahead).start()

            chunk_copy(k).wait()
            dst_ref[pl.ds(pl.multiple_of(k * ch, ch), ch), :] = stage[lax.rem(k, STAGE_SLOTS)].astype(BF16)
            return carry

        lax.fori_loop(0, n, body, 0)

    pl.run_scoped(run, pltpu.VMEM((STAGE_SLOTS, ch, cols), F32), pltpu.SemaphoreType.DMA((STAGE_SLOTS,)))


def _weights_call(body, layer, weights, emit=False, **kw):
    idx = sorted(weights)
    nw = len(idx)
    in_specs = list(kw.pop("in_specs"))
    for i in idx:
        in_specs[i] = pl.BlockSpec(memory_space=pl.ANY)
    scratch = list(kw.pop("scratch_shapes", ())) + [pltpu.VMEM(weights[i], BF16) for i in idx]
    out_specs, out_shape = list(kw.pop("out_specs")), list(kw.pop("out_shape"))
    n_io = len(in_specs) + len(out_specs)
    if emit:
        assert kw["grid"][0] >= 2
        out_specs += [pl.BlockSpec(memory_space=pl.ANY)] * nw
        out_shape += [jax.ShapeDtypeStruct(weights[i], BF16) for i in idx]
        scratch.append(pltpu.SemaphoreType.DMA((nw,)))

    def with_weights(*refs):
        refs = list(refs)
        sem = refs.pop() if emit else None
        slabs = refs[len(refs) - nw:]
        del refs[len(refs) - nw:]
        copies = []
        if emit:
            copies = [pltpu.make_async_copy(slab, out, sem.at[k])
                      for k, (slab, out) in enumerate(zip(slabs, refs[n_io:n_io + nw]))]
            del refs[n_io:n_io + nw]

        @pl.when(pl.program_id(0) == 0)
        def _():
            for i, slab in zip(idx, slabs):
                _stream_cast(refs[i], layer, slab)

        if emit:
            @pl.when(pl.program_id(0) == 1)
            def _():
                for c in copies:
                    c.start()

        for i, slab in zip(idx, slabs):
            refs[i] = slab
        body(*refs)

        if emit:
            @pl.when(pl.program_id(0) == pl.num_programs(0) - 1)
            def _():
                for c in copies:
                    c.wait()

    return pl.pallas_call(with_weights, in_specs=in_specs, out_specs=out_specs, out_shape=out_shape,
                          scratch_shapes=scratch, **kw)


def _layered_call(body, n_in, layer, prev, make_call=pl.pallas_call, **kw):
    idx = sorted(prev)
    if all(prev[o] is None for o in idx):
        out_specs = list(kw.pop("out_specs"))
        for o in idx:
            spec = out_specs[o]
            out_specs[o] = pl.BlockSpec(
                (DEPTH,) + tuple(spec.block_shape[1:]),
                lambda *a, index_map=spec.index_map: (0,) + tuple(index_map(*a))[1:])

        def with_fill(*refs):
            refs = list(refs)
            for o in idx:
                full = refs[n_in + o]
                for d in range(DEPTH):
                    if d != layer:
                        full[d] = jnp.zeros(full.shape[1:], full.dtype)
                refs[n_in + o] = full.at[pl.ds(layer, 1)]
            return body(*refs)

        return make_call(with_fill, out_specs=out_specs, **kw)

    kw["in_specs"] = list(kw["in_specs"]) + [pl.BlockSpec(memory_space=pl.ANY)] * len(idx)

    def with_aliased(*refs):
        return body(*refs[:n_in], *refs[n_in + len(idx):])

    call = make_call(with_aliased, input_output_aliases={n_in + k: o for k, o in enumerate(idx)}, **kw)
    return lambda *args: call(*args, *[prev[o] for o in idx])


def _ffn_inproj(x, layer, g1, gm, w, cast):
    t = x.shape[0]
    row = lambda n: pl.BlockSpec((TM, n), lambda i: (i, 0))
    kw = dict(
        grid=(t // TM,),
        out_specs=[row(D_MODEL), row(CONV_CH), row(MIX_IN - 2 * CONV_CH)],
        out_shape=[jax.ShapeDtypeStruct((t, D_MODEL), F32),
                   jax.ShapeDtypeStruct((t, CONV_CH), F32),
                   jax.ShapeDtypeStruct((t, MIX_IN - 2 * CONV_CH), F32)],
        compiler_params=pltpu.CompilerParams(vmem_limit_bytes=VMEM_LIMIT),
        name="ffn_inproj")
    vec = _resident((1, D_MODEL), layer)
    args = (x, g1, w[0], w[1], w[2], gm, w[3])
    if not cast:
        return pl.pallas_call(
            _ffn_inproj_kernel,
            in_specs=[row(D_MODEL), vec, _whole(w[0]), _whole(w[1]), _whole(w[2]), vec, _whole(w[3])], **kw)(*args)
    return _weights_call(
        _ffn_inproj_kernel, layer,
        {2: (D_MODEL, D_FF), 3: (D_MODEL, D_FF), 4: (D_FF, D_MODEL), 6: (D_MODEL, MIX_IN)}, emit=True,
        in_specs=[row(D_MODEL), vec, None, None, None, vec, None], **kw)(*args)


def _layer_norm_silu(y, lg, lb):
    mu = jnp.mean(y, axis=-1, keepdims=True)
    d = y - mu
    var = jnp.mean(d * d, axis=-1, keepdims=True)
    return _silu(d * lax.rsqrt(var + EPS) * lg + lb)


def _conv_p_kernel(u_ref, w_ref, b_ref, lg_ref, lb_ref, y_ref, nb_ref, win_ref, acc_ref):
    j = pl.program_id(1)
    for t in range(CONV_LT):
        @pl.when(j == 0)
        def _():
            win_ref[0, t, 0:CONV_HALO, :] = jnp.zeros((CONV_HALO, LANES), F32)

        @pl.when(j > 0)
        def _():
            win_ref[0, t, 0:CONV_HALO, :] = win_ref[0, t, CONV_TT:CONV_TT + CONV_HALO, :]

        win_ref[0, t, CONV_HALO:CONV_ROWS, :] = u_ref[:, t * LANES:(t + 1) * LANES]
        win_ref[0, t, CONV_ROWS:CONV_ROWS + SUBLANES, :] = jnp.zeros((SUBLANES, LANES), F32)
        for sh in range(1, SUBLANES):
            win_ref[sh, t, 0:CONV_ROWS, :] = win_ref[0, t, sh:sh + CONV_ROWS, :]

    first = CONV_HALO - (CONV_K - 1)
    n_rc = CONV_TT // CONV_RC

    def taps(idx, carry):
        t = idx // n_rc
        r0 = pl.multiple_of((idx % n_rc) * CONV_RC, CONV_RC)
        accs = [None] * (CONV_RC // SUBLANES)
        for k in range(CONV_K):
            sh = (first + k) % SUBLANES
            wk = w_ref[t, k]
            for i in range(CONV_RC // SUBLANES):
                a0 = pl.multiple_of(r0 + (first + k - sh) + i * SUBLANES, SUBLANES)
                term = wk * win_ref[sh, t, pl.ds(a0, SUBLANES), :]
                accs[i] = term if accs[i] is None else accs[i] + term
        acc_ref[t, pl.ds(r0, CONV_RC), :] = jnp.concatenate(accs, axis=0)
        return carry

    lax.fori_loop(0, CONV_LT * n_rc, taps, 0)

    def norm(c, carry):
        r0 = pl.multiple_of(c * CONV_RC, CONV_RC)
        y = jnp.concatenate([acc_ref[t, pl.ds(r0, CONV_RC), :] for t in range(CONV_LT)], axis=-1) + b_ref[...]
        y_ref[pl.ds(r0, CONV_RC), :] = _layer_norm_silu(y, lg_ref[...], lb_ref[...])
        return carry

    lax.fori_loop(0, n_rc, norm, 0, unroll=CONV_NORM_UNROLL)

    @pl.when(j == pl.num_programs(1) - 1)
    def _():
        nb_ref[0, 0] = u_ref[CONV_TT - (CONV_K - 1):CONV_TT, :]


def _conv_p(u, layer, conv_w, conv_b, ln_g, ln_b, prev_nb):
    nt = SEQ // CONV_TT
    vec = lambda: pl.BlockSpec((None, 1, CONV_CH), lambda b, j: (layer, 0, 0))
    return _layered_call(
        _conv_p_kernel, 5, layer, {1: prev_nb},
        grid=(BATCH, nt),
        in_specs=[pl.BlockSpec((CONV_TT, CONV_CH), lambda b, j: (b * nt + j, 0)),
                  pl.BlockSpec((None, CONV_LT, CONV_K, SUBLANES, LANES), lambda b, j: (layer, 0, 0, 0, 0)),
                  vec(), vec(), vec()],
        out_specs=[pl.BlockSpec((CONV_TT, CONV_CH), lambda b, j: (b * nt + j, 0)),
                   pl.BlockSpec((1, 1, CONV_K - 1, CONV_CH), lambda b, j: (layer, b, 0, 0))],
        out_shape=[jax.ShapeDtypeStruct((T_P, CONV_CH), F32),
                   jax.ShapeDtypeStruct((DEPTH, BATCH, CONV_K - 1, CONV_CH), F32)],
        scratch_shapes=[pltpu.VMEM((SUBLANES, CONV_LT, CONV_ROWS + SUBLANES, LANES), F32),
                        pltpu.VMEM((CONV_LT, CONV_TT, LANES), F32)],
        name="conv_prompt",
    )(u, conv_w, conv_b, ln_g, ln_b)


def _conv_s_kernel(st_ref, ut_ref, w_ref, b_ref, lg_ref, lb_ref, y_ref, nb_ref):
    keep = CONV_K - 1 - DEC_SEQ
    for t in range(DEC_SEQ):
        y = None
        for j in range(t, CONV_K - 1):
            term = st_ref[0, j] * w_ref[j - t:j - t + 1, :]
            y = term if y is None else y + term
        for j in range(t + 1):
            k = CONV_K - 1 - t + j
            y = y + ut_ref[j] * w_ref[k:k + 1, :]
        y_ref[t] = _layer_norm_silu(y + b_ref[...], lg_ref[...], lb_ref[...])
    for j in range(keep):
        nb_ref[0, j] = st_ref[0, j + DEC_SEQ]
    for t in range(DEC_SEQ):
        nb_ref[0, keep + t] = ut_ref[t]


def _conv_s(state_t, ut, layer, conv_w, conv_b, ln_g, ln_b, prev_nb):
    bt = CONV_S_BT
    vec = lambda: pl.BlockSpec((None, 1, CONV_CH), lambda i: (layer, 0, 0))
    return _layered_call(
        _conv_s_kernel, 6, layer, {1: prev_nb},
        grid=(DEC_BATCH // bt,),
        in_specs=[pl.BlockSpec((1, CONV_K - 1, bt, CONV_CH), lambda i: (layer, 0, i, 0)),
                  pl.BlockSpec((DEC_SEQ, bt, CONV_CH), lambda i: (0, i, 0)),
                  pl.BlockSpec((None, CONV_K, CONV_CH), lambda i: (layer, 0, 0)),
                  vec(), vec(), vec()],
        out_specs=[pl.BlockSpec((DEC_SEQ, bt, CONV_CH), lambda i: (0, i, 0)),
                   pl.BlockSpec((1, CONV_K - 1, bt, CONV_CH), lambda i: (layer, 0, i, 0))],
        out_shape=[jax.ShapeDtypeStruct((DEC_SEQ, DEC_BATCH, CONV_CH), F32),
                   jax.ShapeDtypeStruct((DEPTH, CONV_K - 1, DEC_BATCH, CONV_CH), F32)],
        name="conv_sample",
    )(state_t, ut, conv_w, conv_b, ln_g, ln_b)


def _ret_p_kernel(q_ref, k_ref, v_ref, g_ref, cos_ref, sin_ref, dec_ref, qd_ref, kd_ref,
                  cd_ref, gn_ref, o_ref, ns_ref, s_ref):
    j = pl.program_id(1)

    @pl.when(j == 0)
    def _():
        s_ref[...] = jnp.zeros_like(s_ref)

    units = [(e, h, slice(h * RET_DK, (h + 1) * RET_DK)) for h in range(RET_HEADS) for e in range(RET_PB)]
    chunks = []
    for c in range(RET_CPS):
        rows = slice(c * RET_CHUNK, (c + 1) * RET_CHUNK)
        cos2 = cos_ref[rows, :]
        sin2 = sin_ref[rows, :]
        qs = [_rotary(q_ref[e, rows, sl], cos2, sin2).astype(BF16) for e, h, sl in units]
        ks = [_rotary(k_ref[e, rows, sl], cos2, sin2) * (RET_DK ** -0.5) for e, h, sl in units]
        vs = [v_ref[e, rows, sl].astype(BF16) for e, h, sl in units]
        scores = [(_dot_nt(qb, kh.astype(BF16)) * dec_ref[h]).astype(BF16)
                  for qb, kh, (e, h, sl) in zip(qs, ks, units)]
        intra = [_dot(sc, vb) for sc, vb in zip(scores, vs)]
        upd = [lax.dot_general((kh * kd_ref[h]).astype(BF16), vb, (((0,), (0,)), ((), ())),
                               preferred_element_type=F32) for kh, vb, (e, h, sl) in zip(ks, vs, units)]
        chunks.append((rows, qs, intra, upd))

    states = [s_ref[e, h] for e, h, sl in units]
    outs = []
    for rows, qs, intra, upd in chunks:
        outs.append([a + _dot(qb, s_h.astype(BF16)) * qd_ref[h]
                     for a, qb, s_h, (e, h, sl) in zip(intra, qs, states, units)])
        states = [s_h * cd_ref[h] + u for s_h, u, (e, h, sl) in zip(states, upd, units)]

    for (rows, _, _, _), out in zip(chunks, outs):
        for o, (e, h, sl) in zip(out, units):
            o_ref[e, rows, sl] = _group_norm_gate(o, g_ref[e, rows, sl], gn_ref[:, sl])
    for s_h, (e, h, sl) in zip(states, units):
        s_ref[e, h] = s_h

    @pl.when(j == pl.num_programs(1) - 1)
    def _():
        ns_ref[0] = s_ref[...]


def _ret_p(r, layer, tabs, gn, prev_ns):
    cos2, sin2, dec, qd, kd, cd = tabs
    rows = RET_CPS * RET_CHUNK
    r3 = r.reshape(BATCH, SEQ, r.shape[-1])
    col = lambda c: pl.BlockSpec((RET_PB, rows, RET_WIDTH), lambda b, j: (b, j, c))
    const = lambda: pl.BlockSpec((RET_HEADS, RET_CHUNK, RET_DK), lambda b, j: (0, 0, 0))
    tab = lambda: pl.BlockSpec((rows, RET_DK), lambda b, j: (j, 0))
    out, ns = _layered_call(
        _ret_p_kernel, 11, layer, {1: prev_ns},
        grid=(BATCH // RET_PB, SEQ // rows),
        in_specs=[col(0), col(1), col(2), col(3), tab(), tab(), const(), const(), const(), const(),
                  pl.BlockSpec((None, 1, RET_WIDTH), lambda b, j: (layer, 0, 0))],
        out_specs=[pl.BlockSpec((RET_PB, rows, RET_WIDTH), lambda b, j: (b, j, 0)),
                   pl.BlockSpec((1, RET_PB, RET_HEADS, RET_DK, RET_DV), lambda b, j: (layer, b, 0, 0, 0))],
        out_shape=[jax.ShapeDtypeStruct((BATCH, SEQ, RET_WIDTH), F32),
                   jax.ShapeDtypeStruct((DEPTH, BATCH, RET_HEADS, RET_DK, RET_DV), F32)],
        scratch_shapes=[pltpu.VMEM((RET_PB, RET_HEADS, RET_DK, RET_DV), F32)],
        name="retention_prompt",
    )(r3, r3, r3, r3, cos2, sin2, dec, qd, kd, cd, gn)
    return out.reshape(T_P, RET_WIDTH), ns


def _ret_s_kernel(q_ref, k_ref, v_ref, g_ref, st_ref, cos_ref, sin_ref, dec_ref, qd_ref, kd_ref,
                  cd_ref, gn_ref, o_ref, ns_ref):
    cos2 = cos_ref[...]
    sin2 = sin_ref[...]
    zpad = jnp.zeros((RET_DK - SUBLANES, RET_DK), F32)
    second = lax.broadcasted_iota(jnp.int32, (SUBLANES, RET_DK), 0) >= DEC_SEQ

    def pick(a, jj):
        return jnp.where(second, a[DEC_SEQ + jj:DEC_SEQ + jj + 1, :], a[jj:jj + 1, :])

    def body(p, carry):
        rows = pl.ds(pl.multiple_of(p * SUBLANES, SUBLANES), SUBLANES)
        for h in range(RET_HEADS):
            sl = slice(h * RET_DK, (h + 1) * RET_DK)
            qh = _rotary(q_ref[rows, sl], cos2, sin2)
            kh = _rotary(k_ref[rows, sl], cos2, sin2) * (RET_DK ** -0.5)
            vh = v_ref[rows, sl]
            qb = qh.astype(BF16)
            kdec = kh * kd_ref[h]
            vpad = jnp.concatenate([vh, zpad], axis=0).astype(BF16)
            from_state = []
            for x in range(PAIR):
                s_x = st_ref[0, p * PAIR + x, h]
                from_state.append(_dot(qb, s_x.astype(BF16)))
                mine = second if x else jnp.logical_not(second)
                kx = jnp.concatenate([jnp.where(mine, kdec, 0.0), zpad], axis=0)
                upd = _dot(kx.T.astype(BF16), vpad)
                ns_ref[0, p * PAIR + x, h] = s_x * cd_ref[h] + upd
            o = jnp.where(second, from_state[1], from_state[0]) * qd_ref[h]
            for jj in range(DEC_SEQ):
                sj = jnp.sum(qh * pick(kh, jj), axis=-1, keepdims=True)
                o = o + (sj * dec_ref[h, jj]) * pick(vh, jj)
            o_ref[rows, sl] = _group_norm_gate(o, g_ref[rows, sl], gn_ref[:, sl])
        return carry

    lax.fori_loop(0, RET_S_BT // PAIR, body, 0, unroll=2)


def _ret_s(r, state_ret, layer, tabs, gn, prev_ns):
    cos2, sin2, dec, qd, kd, cd = tabs
    bt = RET_S_BT
    col = lambda c: pl.BlockSpec((bt * DEC_SEQ, RET_WIDTH), lambda i: (i, c))
    full = lambda a: pl.BlockSpec(a.shape, lambda i: (0,) * a.ndim)
    st = lambda: pl.BlockSpec((1, bt, RET_HEADS, RET_DK, RET_DV), lambda i: (layer, i, 0, 0, 0))
    return _layered_call(
        _ret_s_kernel, 12, layer, {1: prev_ns},
        grid=(DEC_BATCH // bt,),
        in_specs=[col(0), col(1), col(2), col(3), st(), full(cos2), full(sin2), full(dec), full(qd),
                  full(kd), full(cd), pl.BlockSpec((None, 1, RET_WIDTH), lambda i: (layer, 0, 0))],
        out_specs=[pl.BlockSpec((bt * DEC_SEQ, RET_WIDTH), lambda i: (i, 0)), st()],
        out_shape=[jax.ShapeDtypeStruct((T_S, RET_WIDTH), F32),
                   jax.ShapeDtypeStruct((DEPTH, DEC_BATCH, RET_HEADS, RET_DK, RET_DV), F32)],
        name="retention_sample",
    )(r, r, r, r, state_ret, cos2, sin2, dec, qd, kd, cd, gn)


def _memkv_kernel(m_ref, g_ref, wk_ref, wv_ref, kf_ref, vf_ref, kb_ref, vb_ref):
    mn = _rms(m_ref[...], g_ref[...]).astype(BF16)
    for w_ref, flat_ref, b_ref in ((wk_ref, kf_ref, kb_ref), (wv_ref, vf_ref, vb_ref)):
        y = _dot(mn, w_ref[...])
        b_ref[...] = y.astype(BF16)
        for h in range(CA_HEADS):
            for half in range(2):
                c0 = (h * 2 + half) * LANES
                flat_ref[0, pl.ds(half * CA_HEADS + h, N_MEM, stride=2 * CA_HEADS), :] = y[:, c0:c0 + LANES]


def _memkv(mem, layer, g, wk, wv, prev):
    t = mem.shape[0]
    row = lambda: pl.BlockSpec((N_MEM, D_MODEL), lambda i: (i, 0))
    flat = lambda: pl.BlockSpec((1, KV_ROWS, LANES), lambda i: (layer, i, 0))
    flat_shape = jax.ShapeDtypeStruct((DEPTH, BATCH * KV_ROWS, LANES), F32)
    square = (D_MODEL, D_MODEL)
    return _layered_call(
        _memkv_kernel, 4, layer, {0: prev[0], 1: prev[1]},
        make_call=functools.partial(_weights_call, layer=layer, weights={2: square, 3: square}),
        grid=(t // N_MEM,),
        in_specs=[row(), _resident((1, D_MODEL), layer), None, None],
        out_specs=[flat(), flat(), row(), row()],
        out_shape=[flat_shape, flat_shape, jax.ShapeDtypeStruct((t, D_MODEL), BF16),
                   jax.ShapeDtypeStruct((t, D_MODEL), BF16)],
        name="memory_kv",
    )(mem, g, wk, wv)


def _mix_out_q(x1, conv, ret, wo_ref, gca_ref, wq_ref):
    mix = jnp.concatenate([conv, ret], axis=-1).astype(BF16)
    x2 = x1 + _dot(mix, wo_ref[...])
    q = _dot(_rms(x2, gca_ref[...]).astype(BF16), wq_ref[...])
    return x2, q


def _finish(x2, o, wco_ref, g2_ref, w1_ref, w3_ref, w2_ref, gf_ref, final):
    x3 = x2 + _dot(o, wco_ref[...])
    x4 = _ffn(x3, g2_ref, w1_ref, w3_ref, w2_ref)
    return _rms(x4, gf_ref[...]) if final else x4


def _post_p_kernel(x1_ref, conv_ref, ret_ref, mk_ref, mv_ref, wo_ref, gca_ref, wq_ref, wco_ref,
                   g2_ref, w1_ref, w3_ref, w2_ref, gf_ref, y_ref, *, final):
    x2, q = _mix_out_q(x1_ref[...], conv_ref[...], ret_ref[...], wo_ref, gca_ref, wq_ref)
    cols = [slice(h * CA_HEAD_DIM, (h + 1) * CA_HEAD_DIM) for h in range(CA_HEADS)]
    qb = q.astype(BF16)
    scores = [_dot_nt(qb[:, sl], mk_ref[:, sl]) * (CA_HEAD_DIM ** -0.5) for sl in cols]
    probs = [_softmax_rows(s).astype(BF16) for s in scores]
    heads = [_dot(p, mv_ref[:, sl]) for p, sl in zip(probs, cols)]
    o = jnp.concatenate(heads, axis=-1).astype(BF16)
    y_ref[...] = _finish(x2, o, wco_ref, g2_ref, w1_ref, w3_ref, w2_ref, gf_ref, final)


def _post_p(x1, conv, ret, mk, mv, layer, wo, gca, wq, wco, g2, w1, w3, w2, gf, final):
    row = lambda n: pl.BlockSpec((TM, n), lambda i: (i, 0))
    mem = lambda: pl.BlockSpec((N_MEM, D_MODEL), lambda i: (i // (SEQ // TM), 0))
    square = (D_MODEL, D_MODEL)
    return _weights_call(
        functools.partial(_post_p_kernel, final=final), layer,
        {5: square, 7: square, 8: square, 10: (D_MODEL, D_FF), 11: (D_MODEL, D_FF), 12: (D_FF, D_MODEL)},
        emit=True,
        grid=(T_P // TM,),
        in_specs=[row(D_MODEL), row(CONV_CH), row(RET_WIDTH), mem(), mem(),
                  None, _resident((1, D_MODEL), layer), None, None, _resident((1, D_MODEL), layer),
                  None, None, None,
                  pl.BlockSpec((1, D_MODEL), lambda i: (0, 0))],
        out_specs=[row(D_MODEL)],
        out_shape=[jax.ShapeDtypeStruct((T_P, D_MODEL), F32)],
        compiler_params=pltpu.CompilerParams(vmem_limit_bytes=VMEM_LIMIT),
        name="post_prompt",
    )(x1, conv, ret, mk, mv, wo, gca, wq, wco, g2, w1, w3, w2, gf)


def _post_a_s_kernel(x1_ref, conv_ref, ret_ref, wo_ref, gca_ref, wq_ref, x2_ref, q_ref):
    x2, q = _mix_out_q(x1_ref[...], conv_ref[...], ret_ref[...], wo_ref, gca_ref, wq_ref)
    x2_ref[...] = x2
    q_ref[...] = q


def _post_a_s(x1, conv, ret, layer, wo, gca, wq):
    row = lambda n: pl.BlockSpec((TM, n), lambda i: (i, 0))
    return pl.pallas_call(
        _post_a_s_kernel,
        grid=(T_S // TM,),
        in_specs=[row(D_MODEL), row(CONV_CH), row(RET_WIDTH), _whole(wo), _resident((1, D_MODEL), layer),
                  _whole(wq)],
        out_specs=[row(D_MODEL), row(D_MODEL)],
        out_shape=[jax.ShapeDtypeStruct((T_S, D_MODEL), F32)] * 2,
        name="mix_out_q_sample",
    )(x1, conv, ret, wo, gca, wq)


def _post_b_s_kernel(x2_ref, o_ref, wco_ref, g2_ref, w1_ref, w3_ref, w2_ref, gf_ref, y_ref, *, final):
    y_ref[...] = _finish(x2_ref[...], o_ref[...].astype(BF16), wco_ref, g2_ref, w1_ref, w3_ref, w2_ref,
                         gf_ref, final)


def _post_b_s(x2, o, layer, wco, g2, w1, w3, w2, gf, final):
    row = lambda: pl.BlockSpec((TM, D_MODEL), lambda i: (i, 0))
    return pl.pallas_call(
        functools.partial(_post_b_s_kernel, final=final),
        grid=(T_S // TM,),
        in_specs=[row(), row(), _whole(wco), _resident((1, D_MODEL), layer), _whole(w1), _whole(w3), _whole(w2),
                  pl.BlockSpec((1, D_MODEL), lambda i: (0, 0))],
        out_specs=row(),
        out_shape=jax.ShapeDtypeStruct((T_S, D_MODEL), F32),
        compiler_params=pltpu.CompilerParams(vmem_limit_bytes=VMEM_LIMIT),
        name="attn_out_ffn_sample",
    )(x2, o, wco, g2, w1, w3, w2, gf)


def _attn_s_kernel(q_ref, k_ref, v_ref, o_ref):
    nh = CA_HEADS * SUBLANES
    lane = lax.broadcasted_iota(jnp.int32, (nh, KV_ROWS), 1)
    row = lax.broadcasted_iota(jnp.int32, (nh, KV_ROWS), 0)
    valid = (lane % (2 * CA_HEADS)) == (row // SUBLANES)
    second = lax.broadcasted_iota(jnp.int32, (2 * nh, LANES), 0) % SUBLANES >= DEC_SEQ
    batches = range(ATT_S_BT)
    qts = []
    for pr in range(ATT_S_BT // PAIR):
        q = q_ref[pr * SUBLANES:(pr + 1) * SUBLANES, :]
        blocks = [q[:, (h * 2 + half) * LANES:(h * 2 + half + 1) * LANES]
                  for half in range(2) for h in range(CA_HEADS)]
        qts.append(jnp.concatenate(blocks, axis=0).astype(BF16))
    sts = [_dot_nt(qts[b // PAIR], k_ref[0, b].astype(BF16)) for b in batches]
    ws = []
    for st in sts:
        s = (st[:nh] + pltpu.roll(st[nh:], KV_ROWS - CA_HEADS, 1)) * (CA_HEAD_DIM ** -0.5)
        p = _softmax_rows(jnp.where(valid, s, NEG_BIG))
        ws.append(jnp.concatenate([p, pltpu.roll(p, CA_HEADS, 1)], axis=0).astype(BF16))
    outs = [_dot(ws[b], v_ref[0, b].astype(BF16)) for b in batches]
    for pr in range(ATT_S_BT // PAIR):
        o = jnp.where(second, outs[pr * PAIR + 1], outs[pr * PAIR])
        for half in range(2):
            for h in range(CA_HEADS):
                r0 = (half * CA_HEADS + h) * SUBLANES
                c0 = (h * 2 + half) * LANES
                o_ref[pr * SUBLANES:(pr + 1) * SUBLANES, c0:c0 + LANES] = o[r0:r0 + SUBLANES]


def _flat_cache(c):
    c = c.reshape(DEPTH, DEC_BATCH, N_MEM, CA_HEADS, 2, LANES)
    return c.transpose(0, 1, 2, 4, 3, 5).reshape(DEPTH, DEC_BATCH, KV_ROWS, LANES)


def _unflat_cache(f):
    f = f.reshape(DEPTH, BATCH, N_MEM, 2, CA_HEADS, LANES)
    return f.transpose(0, 1, 2, 4, 3, 5).reshape(DEPTH, BATCH, N_MEM, CA_HEADS, CA_HEAD_DIM)


def _attn_s(q, cache_k, cache_v, layer):
    bt = ATT_S_BT
    qs = lambda: pl.BlockSpec((bt * DEC_SEQ, D_MODEL), lambda i: (i, 0))
    kv = lambda: pl.BlockSpec((1, bt, KV_ROWS, LANES), lambda i: (layer, i, 0, 0))
    return pl.pallas_call(
        _attn_s_kernel,
        grid=(DEC_BATCH // bt,),
        in_specs=[qs(), kv(), kv()],
        out_specs=qs(),
        out_shape=jax.ShapeDtypeStruct((T_S, D_MODEL), F32),
        compiler_params=pltpu.CompilerParams(vmem_limit_bytes=VMEM_LIMIT),
        name="cross_attn_sample",
    )(q, cache_k, cache_v)


def _rope_tables(pos):
    inv_freq = ROPE_BASE ** (-jnp.arange(0, RET_DK, 2, dtype=F32) / RET_DK)
    ang = pos[:, None] * inv_freq[None, :]
    cos, sin = jnp.cos(ang), jnp.sin(ang)
    return jnp.concatenate([cos, cos], axis=-1), jnp.concatenate([-sin, sin], axis=-1)


def _decay_tables(c):
    log_gamma = jnp.log1p(-jnp.exp2(-5.0 - jnp.arange(RET_HEADS, dtype=F32)))
    idx = jnp.arange(c, dtype=F32)
    rel = idx[:, None] - idx[None, :]
    decay = jnp.where(rel[None] >= 0,
                      jnp.exp(log_gamma[:, None, None] * jnp.maximum(rel, 0.0)[None]), 0.0)
    q_dec = jnp.exp(log_gamma[:, None] * (idx[None, :] + 1.0))
    k_dec = jnp.exp(log_gamma[:, None] * (c - 1.0 - idx[None, :]))
    chunk_dec = jnp.exp(log_gamma * c)
    return decay, q_dec, k_dec, chunk_dec


def _prompt_tables():
    cos2, sin2 = _rope_tables(jnp.arange(SEQ, dtype=F32))
    decay, q_dec, k_dec, chunk_dec = _decay_tables(RET_CHUNK)
    lanes = (RET_HEADS, RET_CHUNK, RET_DK)
    return (cos2, sin2, decay,
            jnp.broadcast_to(q_dec[:, :, None], lanes),
            jnp.broadcast_to(k_dec[:, :, None], lanes),
            jnp.broadcast_to(chunk_dec[:, None, None], lanes))


def _sample_tables():
    slab = lambda a: jnp.concatenate([a] * PAIR, axis=-2)
    cos2, sin2 = _rope_tables(PAST_LEN + jnp.arange(DEC_SEQ, dtype=F32))
    decay, q_dec, k_dec, chunk_dec = _decay_tables(DEC_SEQ)
    rows = (RET_HEADS, DEC_SEQ, RET_DK)
    dec = jnp.broadcast_to(jnp.swapaxes(decay, 1, 2)[:, :, :, None], (RET_HEADS, DEC_SEQ, DEC_SEQ, RET_DK))
    return (slab(cos2), slab(sin2), slab(dec),
            slab(jnp.broadcast_to(q_dec[:, :, None], rows)),
            slab(jnp.broadcast_to(k_dec[:, :, None], rows)),
            jnp.broadcast_to(chunk_dec[:, None, None], (RET_HEADS, RET_DK, RET_DV)))


def kernel(x_prompt, x_sample, state_conv, state_ret, cache_mem_k, cache_mem_v, mem_prompt, g_ffn1, w1_ffn1, w3_ffn1, w2_ffn1, g_mix, w_in, conv_w, conv_b, conv_ln_g, conv_ln_b, ret_gn_g, w_out, g_ca, g_mem, w_cq, w_ck, w_cv, w_co, g_ffn2, w1_ffn2, w3_ffn2, w2_ffn2, g_final):
    vec = lambda g: g.reshape(DEPTH, 1, -1)
    w1a, w3a, w2a, w1b, w3b, w2b = w1_ffn1, w3_ffn1, w2_ffn1, w1_ffn2, w3_ffn2, w2_ffn2
    win, wout, wcq, wck, wcv, wco = w_in, w_out, w_cq, w_ck, w_cv, w_co
    g1, gm, gca, gmem, g2 = map(vec, (g_ffn1, g_mix, g_ca, g_mem, g_ffn2))
    cb, clg, clb, gn = map(vec, (conv_b, conv_ln_g, conv_ln_b, ret_gn_g))
    gf = g_final.reshape(1, D_MODEL)
    cache_k = _flat_cache(cache_mem_k)
    cache_v = _flat_cache(cache_mem_v)
    mem = mem_prompt.reshape(BATCH * N_MEM, D_MODEL)
    tabs_p = _prompt_tables()
    tabs_s = _sample_tables()
    cw8 = conv_w.reshape(DEPTH, CONV_K, CONV_LT, LANES).transpose(0, 2, 1, 3)
    cw8 = jnp.broadcast_to(cw8[:, :, :, None, :], (DEPTH, CONV_LT, CONV_K, SUBLANES, LANES))

    state_t = jnp.swapaxes(state_conv, 1, 2)

    xp = x_prompt.reshape(T_P, D_MODEL)
    xs = x_sample.reshape(T_S, D_MODEL)
    conv_p = ret_p = conv_s = ret_s = None
    mem_p = (None, None)
    for l in range(DEPTH):
        final = l == DEPTH - 1
        memk_f, memv_f, mk, mv = _memkv(mem, l, gmem, wck, wcv, mem_p)
        mem_p = (memk_f, memv_f)
        x1, u, r, *mix_w = _ffn_inproj(xp, l, g1, gm, (w1a, w3a, w2a, win), cast=True)
        cv, conv_p = _conv_p(u, l, cw8, cb, clg, clb, conv_p)
        rt, ret_p = _ret_p(r, l, tabs_p, gn, ret_p)
        xp, wout_b, wcq_b, wco_b, w1_b, w3_b, w2_b = _post_p(
            x1, cv, rt, mk, mv, l, wout, gca, wcq, wco, g2, w1b, w3b, w2b, gf, final)

        x1, u, r = _ffn_inproj(xs, l, g1, gm, mix_w, cast=False)
        ut = jnp.swapaxes(u.reshape(DEC_BATCH, DEC_SEQ, CONV_CH), 0, 1)
        cv, conv_s = _conv_s(state_t, ut, l, conv_w, cb, clg, clb, conv_s)
        cv = jnp.swapaxes(cv, 0, 1).reshape(T_S, CONV_CH)
        rt, ret_s = _ret_s(r, state_ret, l, tabs_s, gn, ret_s)
        x2, q = _post_a_s(x1, cv, rt, l, wout_b, gca, wcq_b)
        o = _attn_s(q, cache_k, cache_v, l)
        xs = _post_b_s(x2, o, l, wco_b, g2, w1_b, w3_b, w2_b, gf, final)

    return (xp.reshape(BATCH, SEQ, D_MODEL), xs.reshape(DEC_BATCH, DEC_SEQ, D_MODEL),
            conv_p, ret_p, _unflat_cache(mem_p[0]), _unflat_cache(mem_p[1]),
            jnp.swapaxes(conv_s, 1, 2), ret_s)
```

```python
import functools

import jax
import jax.numpy as jnp
from jax import lax
from jax.experimental import pallas as pl
from jax.experimental.pallas import tpu as pltpu

F32 = jnp.float32
BF16 = jnp.bfloat16

D_MODEL = 1024
BATCH = 8
SEQ = 2048
DEPTH = 2
DEC_BATCH = 128
DEC_SEQ = 4
PAST_LEN = 16384
CONV_CH = 512
CONV_K = 31
RET_HEADS = 4
RET_DK = 128
RET_DV = 128
RET_WIDTH = 512
MIX_IN = 3072
RET_CHUNK = 128
ROPE_BASE = 10000.0
D_FF = 4096
N_MEM = 256
CA_HEADS = 4
CA_HEAD_DIM = 256
EPS = 1e-6
GN_EPS = 1e-5

T_P = BATCH * SEQ
T_S = DEC_BATCH * DEC_SEQ

TM = 512
TF = 1024
CONV_TT = 512
CONV_RC = 64
CONV_HALO = 32
SUBLANES = 8
LANES = 128
CONV_ROWS = CONV_HALO + CONV_TT
CONV_LT = CONV_CH // LANES
CONV_NORM_UNROLL = 4
RET_CPS = 4
RET_PB = 2
PAIR = SUBLANES // DEC_SEQ
RET_S_BT = 8
ATT_S_BT = 8
KV_ROWS = N_MEM * 2 * CA_HEADS
NEG_INF = float("-inf")
CONV_S_BT = 32
STAGE_BYTES = 512 * 1024
STAGE_SLOTS = 8
VMEM_LIMIT = 60 * 1024 * 1024


def _dot(a, b):
    return jnp.dot(a, b, preferred_element_type=F32)


def _dot_nt(a, b):
    return lax.dot_general(a, b, (((1,), (1,)), ((), ())), preferred_element_type=F32)


def _rms(x, g):
    return x * lax.rsqrt(jnp.mean(x * x, axis=-1, keepdims=True) + EPS) * g


def _silu(x):
    return x * jax.nn.sigmoid(x)


def _ffn(x, g_ref, w1_ref, w3_ref, w2_ref):
    xn = _rms(x, g_ref[...]).astype(BF16)
    acc = None
    for c in range(D_FF // TF):
        sl = slice(c * TF, (c + 1) * TF)
        h1 = _dot(xn, w1_ref[:, sl])
        h3 = _dot(xn, w3_ref[:, sl])
        a = (_silu(h1) * h3).astype(BF16)
        part = _dot(a, w2_ref[sl, :])
        acc = part if acc is None else acc + part
    return x + 0.5 * acc


def _softmax_rows(s):
    m = jnp.max(s, axis=-1, keepdims=True)
    e = jnp.exp(s - m)
    return e * (1.0 / jnp.sum(e, axis=-1, keepdims=True))


def _group_norm_gate(o, gate, gn):
    mu = jnp.mean(o, axis=-1, keepdims=True)
    d = o - mu
    var = jnp.mean(d * d, axis=-1, keepdims=True)
    return _silu(gate) * (d * lax.rsqrt(var + GN_EPS) * gn)


def _rotary(t, cos2, sin2):
    return t * cos2 + pltpu.roll(t, RET_DK // 2, 1) * sin2


def _ffn_inproj_kernel(x_ref, g1_ref, w1_ref, w3_ref, w2_ref, gm_ref, win_ref,
                       x1_ref, u_ref, r_ref):
    x1 = _ffn(x_ref[...], g1_ref, w1_ref, w3_ref, w2_ref)
    x1_ref[...] = x1
    h = _rms(x1, gm_ref[...]).astype(BF16)
    proj = _dot(h, win_ref[...])
    u_ref[...] = proj[:, :CONV_CH] * jax.nn.sigmoid(proj[:, CONV_CH:2 * CONV_CH])
    r_ref[...] = proj[:, 2 * CONV_CH:]


def _resident(shape, layer):
    nd = len(shape)
    return pl.BlockSpec((None,) + tuple(shape), lambda *_: (layer,) + (0,) * nd,
                        pipeline_mode=pl.Buffered(1))


def _whole(a):
    return pl.BlockSpec(a.shape, lambda *_: (0,) * a.ndim, pipeline_mode=pl.Buffered(1))


def _stream_cast(w_hbm, layer, dst_ref):
    rows, cols = dst_ref.shape
    ch = 1 << ((STAGE_BYTES // (4 * cols)).bit_length() - 1)
    assert ch % SUBLANES == 0 and rows % ch == 0
    n = rows // ch

    ahead = min(STAGE_SLOTS - 1, n)

    def run(stage, sem):
        def chunk_copy(k):
            slot = lax.rem(k, STAGE_SLOTS)
            return pltpu.make_async_copy(w_hbm.at[layer, pl.ds(k * ch, ch), :], stage.at[slot], sem.at[slot])

        for k in range(ahead):
            chunk_copy(k).start()

        def body(k, carry):
            @pl.when(k + ahead < n)
            def _():
                chunk_copy(k + ahead).start()

            chunk_copy(k).wait()
            dst_ref[pl.ds(pl.multiple_of(k * ch, ch), ch), :] = stage[lax.rem(k, STAGE_SLOTS)].astype(BF16)
            return carry

        lax.fori_loop(0, n, body, 0)

    pl.run_scoped(run, pltpu.VMEM((STAGE_SLOTS, ch, cols), F32), pltpu.SemaphoreType.DMA((STAGE_SLOTS,)))


def _weights_call(body, layer, weights, emit=False, **kw):
    idx = sorted(weights)
    nw = len(idx)
    in_specs = list(kw.pop("in_specs"))
    for i in idx:
        in_specs[i] = pl.BlockSpec(memory_space=pl.ANY)
    scratch = list(kw.pop("scratch_shapes", ())) + [pltpu.VMEM(weights[i], BF16) for i in idx]
    out_specs, out_shape = list(kw.pop("out_specs")), list(kw.pop("out_shape"))
    n_io = len(in_specs) + len(out_specs)
    if emit:
        assert kw["grid"][0] >= 2
        out_specs += [pl.BlockSpec(memory_space=pl.ANY)] * nw
        out_shape += [jax.ShapeDtypeStruct(weights[i], BF16) for i in idx]
        scratch.append(pltpu.SemaphoreType.DMA((nw,)))

    def with_weights(*refs):
        refs = list(refs)
        sem = refs.pop() if emit else None
        slabs = refs[len(refs) - nw:]
        del refs[len(refs) - nw:]
        copies = []
        if emit:
            copies = [pltpu.make_async_copy(slab, out, sem.at[k])
                      for k, (slab, out) in enumerate(zip(slabs, refs[n_io:n_io + nw]))]
            del refs[n_io:n_io + nw]

        @pl.when(pl.program_id(0) == 0)
        def _():
            for i, slab in zip(idx, slabs):
                _stream_cast(refs[i], layer, slab)

        if emit:
            @pl.when(pl.program_id(0) == 1)
            def _():
                for c in copies:
                    c.start()

        for i, slab in zip(idx, slabs):
            refs[i] = slab
        body(*refs)

        if emit:
            @pl.when(pl.program_id(0) == pl.num_programs(0) - 1)
            def _():
                for c in copies:
                    c.wait()

    return pl.pallas_call(with_weights, in_specs=in_specs, out_specs=out_specs, out_shape=out_shape,
                          scratch_shapes=scratch, **kw)


def _layered_call(body, n_in, layer, prev, make_call=pl.pallas_call, **kw):
    idx = sorted(prev)
    if all(prev[o] is None for o in idx):
        out_specs = list(kw.pop("out_specs"))
        for o in idx:
            spec = out_specs[o]
            out_specs[o] = pl.BlockSpec(
                (DEPTH,) + tuple(spec.block_shape[1:]),
                lambda *a, index_map=spec.index_map: (0,) + tuple(index_map(*a))[1:])

        def with_fill(*refs):
            refs = list(refs)
            for o in idx:
                full = refs[n_in + o]
                for d in range(DEPTH):
                    if d != layer:
                        full[d] = jnp.zeros(full.shape[1:], full.dtype)
                refs[n_in + o] = full.at[pl.ds(layer, 1)]
            return body(*refs)

        return make_call(with_fill, out_specs=out_specs, **kw)

    kw["in_specs"] = list(kw["in_specs"]) + [pl.BlockSpec(memory_space=pl.ANY)] * len(idx)

    def with_aliased(*refs):
        return body(*refs[:n_in], *refs[n_in + len(idx):])

    call = make_call(with_aliased, input_output_aliases={n_in + k: o for k, o in enumerate(idx)}, **kw)
    return lambda *args: call(*args, *[prev[o] for o in idx])


def _ffn_inproj(x, layer, g1, gm, w, cast):
    t = x.shape[0]
    row = lambda n: pl.BlockSpec((TM, n), lambda i: (i, 0))
    kw = dict(
        grid=(t // TM,),
        out_specs=[row(D_MODEL), row(CONV_CH), row(MIX_IN - 2 * CONV_CH)],
        out_shape=[jax.ShapeDtypeStruct((t, D_MODEL), F32),
                   jax.ShapeDtypeStruct((t, CONV_CH), F32),
                   jax.ShapeDtypeStruct((t, MIX_IN - 2 * CONV_CH), F32)],
        compiler_params=pltpu.CompilerParams(vmem_limit_bytes=VMEM_LIMIT),
        name="ffn_inproj")
    vec = _resident((1, D_MODEL), layer)
    args = (x, g1, w[0], w[1], w[2], gm, w[3])
    if not cast:
        return pl.pallas_call(
            _ffn_inproj_kernel,
            in_specs=[row(D_MODEL), vec, _whole(w[0]), _whole(w[1]), _whole(w[2]), vec, _whole(w[3])], **kw)(*args)
    return _weights_call(
        _ffn_inproj_kernel, layer,
        {2: (D_MODEL, D_FF), 3: (D_MODEL, D_FF), 4: (D_FF, D_MODEL), 6: (D_MODEL, MIX_IN)}, emit=True,
        in_specs=[row(D_MODEL), vec, None, None, None, vec, None], **kw)(*args)


def _layer_norm_silu(y, lg, lb):
    mu = jnp.mean(y, axis=-1, keepdims=True)
    d = y - mu
    var = jnp.mean(d * d, axis=-1, keepdims=True)
    return _silu(d * lax.rsqrt(var + EPS) * lg + lb)


def _conv_p_kernel(u_ref, w_ref, b_ref, lg_ref, lb_ref, y_ref, nb_ref, win_ref, acc_ref):
    j = pl.program_id(1)
    for t in range(CONV_LT):
        @pl.when(j == 0)
        def _():
            win_ref[0, t, 0:CONV_HALO, :] = jnp.zeros((CONV_HALO, LANES), F32)

        @pl.when(j > 0)
        def _():
            win_ref[0, t, 0:CONV_HALO, :] = win_ref[0, t, CONV_TT:CONV_TT + CONV_HALO, :]

        win_ref[0, t, CONV_HALO:CONV_ROWS, :] = u_ref[:, t * LANES:(t + 1) * LANES]
        win_ref[0, t, CONV_ROWS:CONV_ROWS + SUBLANES, :] = jnp.zeros((SUBLANES, LANES), F32)
        for sh in range(1, SUBLANES):
            win_ref[sh, t, 0:CONV_ROWS, :] = win_ref[0, t, sh:sh + CONV_ROWS, :]

    first = CONV_HALO - (CONV_K - 1)
    n_rc = CONV_TT // CONV_RC

    def taps(idx, carry):
        t = idx // n_rc
        r0 = pl.multiple_of((idx % n_rc) * CONV_RC, CONV_RC)
        accs = [None] * (CONV_RC // SUBLANES)
        for k in range(CONV_K):
            sh = (first + k) % SUBLANES
            wk = w_ref[t, k]
            for i in range(CONV_RC // SUBLANES):
                a0 = pl.multiple_of(r0 + (first + k - sh) + i * SUBLANES, SUBLANES)
                term = wk * win_ref[sh, t, pl.ds(a0, SUBLANES), :]
                accs[i] = term if accs[i] is None else accs[i] + term
        acc_ref[t, pl.ds(r0, CONV_RC), :] = jnp.concatenate(accs, axis=0)
        return carry

    lax.fori_loop(0, CONV_LT * n_rc, taps, 0)

    def norm(c, carry):
        r0 = pl.multiple_of(c * CONV_RC, CONV_RC)
        y = jnp.concatenate([acc_ref[t, pl.ds(r0, CONV_RC), :] for t in range(CONV_LT)], axis=-1) + b_ref[...]
        y_ref[pl.ds(r0, CONV_RC), :] = _layer_norm_silu(y, lg_ref[...], lb_ref[...])
        return carry

    lax.fori_loop(0, n_rc, norm, 0, unroll=CONV_NORM_UNROLL)

    @pl.when(j == pl.num_programs(1) - 1)
    def _():
        nb_ref[0, 0] = u_ref[CONV_TT - (CONV_K - 1):CONV_TT, :]


def _conv_p(u, layer, conv_w, conv_b, ln_g, ln_b, prev_nb):
    nt = SEQ // CONV_TT
    vec = lambda: pl.BlockSpec((None, 1, CONV_CH), lambda b, j: (layer, 0, 0))
    return _layered_call(
        _conv_p_kernel, 5, layer, {1: prev_nb},
        grid=(BATCH, nt),
        in_specs=[pl.BlockSpec((CONV_TT, CONV_CH), lambda b, j: (b * nt + j, 0)),
                  pl.BlockSpec((None, CONV_LT, CONV_K, SUBLANES, LANES), lambda b, j: (layer, 0, 0, 0, 0)),
                  vec(), vec(), vec()],
        out_specs=[pl.BlockSpec((CONV_TT, CONV_CH), lambda b, j: (b * nt + j, 0)),
                   pl.BlockSpec((1, 1, CONV_K - 1, CONV_CH), lambda b, j: (layer, b, 0, 0))],
        out_shape=[jax.ShapeDtypeStruct((T_P, CONV_CH), F32),
                   jax.ShapeDtypeStruct((DEPTH, BATCH, CONV_K - 1, CONV_CH), F32)],
        scratch_shapes=[pltpu.VMEM((SUBLANES, CONV_LT, CONV_ROWS + SUBLANES, LANES), F32),
                        pltpu.VMEM((CONV_LT, CONV_TT, LANES), F32)],
        name="conv_prompt",
    )(u, conv_w, conv_b, ln_g, ln_b)


def _conv_s_kernel(st_ref, ut_ref, w_ref, b_ref, lg_ref, lb_ref, y_ref, nb_ref):
    keep = CONV_K - 1 - DEC_SEQ
    for t in range(DEC_SEQ):
        y = None
        for j in range(t, CONV_K - 1):
            term = st_ref[0, j] * w_ref[j - t:j - t + 1, :]
            y = term if y is None else y + term
        for j in range(t + 1):
            k = CONV_K - 1 - t + j
            y = y + ut_ref[j] * w_ref[k:k + 1, :]
        y_ref[t] = _layer_norm_silu(y + b_ref[...], lg_ref[...], lb_ref[...])
    for j in range(keep):
        nb_ref[0, j] = st_ref[0, j + DEC_SEQ]
    for t in range(DEC_SEQ):
        nb_ref[0, keep + t] = ut_ref[t]


def _conv_s(state_t, ut, layer, conv_w, conv_b, ln_g, ln_b, prev_nb):
    bt = CONV_S_BT
    vec = lambda: pl.BlockSpec((None, 1, CONV_CH), lambda i: (layer, 0, 0))
    return _layered_call(
        _conv_s_kernel, 6, layer, {1: prev_nb},
        grid=(DEC_BATCH // bt,),
        in_specs=[pl.BlockSpec((1, CONV_K - 1, bt, CONV_CH), lambda i: (layer, 0, i, 0)),
                  pl.BlockSpec((DEC_SEQ, bt, CONV_CH), lambda i: (0, i, 0)),
                  pl.BlockSpec((None, CONV_K, CONV_CH), lambda i: (layer, 0, 0)),
                  vec(), vec(), vec()],
        out_specs=[pl.BlockSpec((DEC_SEQ, bt, CONV_CH), lambda i: (0, i, 0)),
                   pl.BlockSpec((1, CONV_K - 1, bt, CONV_CH), lambda i: (layer, 0, i, 0))],
        out_shape=[jax.ShapeDtypeStruct((DEC_SEQ, DEC_BATCH, CONV_CH), F32),
                   jax.ShapeDtypeStruct((DEPTH, CONV_K - 1, DEC_BATCH, CONV_CH), F32)],
        name="conv_sample",
    )(state_t, ut, conv_w, conv_b, ln_g, ln_b)


def _ret_p_kernel(q_ref, k_ref, v_ref, g_ref, cos_ref, sin_ref, dec_ref, qd_ref, kd_ref,
                  cd_ref, gn_ref, o_ref, ns_ref, s_ref):
    j = pl.program_id(1)

    @pl.when(j == 0)
    def _():
        s_ref[...] = jnp.zeros_like(s_ref)

    units = [(e, h, slice(h * RET_DK, (h + 1) * RET_DK)) for h in range(RET_HEADS) for e in range(RET_PB)]
    chunks = []
    for c in range(RET_CPS):
        rows = slice(c * RET_CHUNK, (c + 1) * RET_CHUNK)
        cos2 = cos_ref[rows, :]
        sin2 = sin_ref[rows, :]
        qs = [_rotary(q_ref[e, rows, sl], cos2, sin2).astype(BF16) for e, h, sl in units]
        ks = [_rotary(k_ref[e, rows, sl], cos2, sin2) * (RET_DK ** -0.5) for e, h, sl in units]
        vs = [v_ref[e, rows, sl].astype(BF16) for e, h, sl in units]
        scores = [(_dot_nt(qb, kh.astype(BF16)) * dec_ref[h]).astype(BF16)
                  for qb, kh, (e, h, sl) in zip(qs, ks, units)]
        intra = [_dot(sc, vb) for sc, vb in zip(scores, vs)]
        upd = [lax.dot_general((kh * kd_ref[h]).astype(BF16), vb, (((0,), (0,)), ((), ())),
                               preferred_element_type=F32) for kh, vb, (e, h, sl) in zip(ks, vs, units)]
        chunks.append((rows, qs, intra, upd))

    states = [s_ref[e, h] for e, h, sl in units]
    outs = []
    for rows, qs, intra, upd in chunks:
        outs.append([a + _dot(qb, s_h.astype(BF16)) * qd_ref[h]
                     for a, qb, s_h, (e, h, sl) in zip(intra, qs, states, units)])
        states = [s_h * cd_ref[h] + u for s_h, u, (e, h, sl) in zip(states, upd, units)]

    for (rows, _, _, _), out in zip(chunks, outs):
        for o, (e, h, sl) in zip(out, units):
            o_ref[e, rows, sl] = _group_norm_gate(o, g_ref[e, rows, sl], gn_ref[:, sl])
    for s_h, (e, h, sl) in zip(states, units):
        s_ref[e, h] = s_h

    @pl.when(j == pl.num_programs(1) - 1)
    def _():
        ns_ref[0] = s_ref[...]


def _ret_p(r, layer, tabs, gn, prev_ns):
    cos2, sin2, dec, qd, kd, cd = tabs
    rows = RET_CPS * RET_CHUNK
    r3 = r.reshape(BATCH, SEQ, r.shape[-1])
    col = lambda c: pl.BlockSpec((RET_PB, rows, RET_WIDTH), lambda b, j: (b, j, c))
    const = lambda: pl.BlockSpec((RET_HEADS, RET_CHUNK, RET_DK), lambda b, j: (0, 0, 0))
    tab = lambda: pl.BlockSpec((rows, RET_DK), lambda b, j: (j, 0))
    out, ns = _layered_call(
        _ret_p_kernel, 11, layer, {1: prev_ns},
        grid=(BATCH // RET_PB, SEQ // rows),
        in_specs=[col(0), col(1), col(2), col(3), tab(), tab(), const(), const(), const(), const(),
                  pl.BlockSpec((None, 1, RET_WIDTH), lambda b, j: (layer, 0, 0))],
        out_specs=[pl.BlockSpec((RET_PB, rows, RET_WIDTH), lambda b, j: (b, j, 0)),
                   pl.BlockSpec((1, RET_PB, RET_HEADS, RET_DK, RET_DV), lambda b, j: (layer, b, 0, 0, 0))],
        out_shape=[jax.ShapeDtypeStruct((BATCH, SEQ, RET_WIDTH), F32),
                   jax.ShapeDtypeStruct((DEPTH, BATCH, RET_HEADS, RET_DK, RET_DV), F32)],
        scratch_shapes=[pltpu.VMEM((RET_PB, RET_HEADS, RET_DK, RET_DV), F32)],
        name="retention_prompt",
    )(r3, r3, r3, r3, cos2, sin2, dec, qd, kd, cd, gn)
    return out.reshape(T_P, RET_WIDTH), ns


def _ret_s_kernel(q_ref, k_ref, v_ref, g_ref, st_ref, cos_ref, sin_ref, dec_ref, qd_ref, kd_ref,
                  cd_ref, gn_ref, o_ref, ns_ref):
    cos2 = cos_ref[...]
    sin2 = sin_ref[...]
    zpad = jnp.zeros((RET_DK - SUBLANES, RET_DK), F32)
    second = lax.broadcasted_iota(jnp.int32, (SUBLANES, RET_DK), 0) >= DEC_SEQ

    def pick(a, jj):
        return jnp.where(second, a[DEC_SEQ + jj:DEC_SEQ + jj + 1, :], a[jj:jj + 1, :])

    def body(p, carry):
        rows = pl.ds(pl.multiple_of(p * SUBLANES, SUBLANES), SUBLANES)
        for h in range(RET_HEADS):
            sl = slice(h * RET_DK, (h + 1) * RET_DK)
            qh = _rotary(q_ref[rows, sl], cos2, sin2)
            kh = _rotary(k_ref[rows, sl], cos2, sin2) * (RET_DK ** -0.5)
            vh = v_ref[rows, sl]
            qb = qh.astype(BF16)
            kdec = kh * kd_ref[h]
            vpad = jnp.concatenate([vh, zpad], axis=0).astype(BF16)
            from_state = []
            for x in range(PAIR):
                s_x = st_ref[0, p * PAIR + x, h]
                from_state.append(_dot(qb, s_x.astype(BF16)))
                mine = second if x else jnp.logical_not(second)
                kx = jnp.concatenate([jnp.where(mine, kdec, 0.0), zpad], axis=0)
                upd = _dot(kx.T.astype(BF16), vpad)
                ns_ref[0, p * PAIR + x, h] = s_x * cd_ref[h] + upd
            o = jnp.where(second, from_state[1], from_state[0]) * qd_ref[h]
            for jj in range(DEC_SEQ):
                sj = jnp.sum(qh * pick(kh, jj), axis=-1, keepdims=True)
                o = o + (sj * dec_ref[h, jj]) * pick(vh, jj)
            o_ref[rows, sl] = _group_norm_gate(o, g_ref[rows, sl], gn_ref[:, sl])
        return carry

    lax.fori_loop(0, RET_S_BT // PAIR, body, 0, unroll=2)


def _ret_s(r, state_ret, layer, tabs, gn, prev_ns):
    cos2, sin2, dec, qd, kd, cd = tabs
    bt = RET_S_BT
    col = lambda c: pl.BlockSpec((bt * DEC_SEQ, RET_WIDTH), lambda i: (i, c))
    full = lambda a: pl.BlockSpec(a.shape, lambda i: (0,) * a.ndim)
    st = lambda: pl.BlockSpec((1, bt, RET_HEADS, RET_DK, RET_DV), lambda i: (layer, i, 0, 0, 0))
    return _layered_call(
        _ret_s_kernel, 12, layer, {1: prev_ns},
        grid=(DEC_BATCH // bt,),
        in_specs=[col(0), col(1), col(2), col(3), st(), full(cos2), full(sin2), full(dec), full(qd),
                  full(kd), full(cd), pl.BlockSpec((None, 1, RET_WIDTH), lambda i: (layer, 0, 0))],
        out_specs=[pl.BlockSpec((bt * DEC_SEQ, RET_WIDTH), lambda i: (i, 0)), st()],
        out_shape=[jax.ShapeDtypeStruct((T_S, RET_WIDTH), F32),
                   jax.ShapeDtypeStruct((DEPTH, DEC_BATCH, RET_HEADS, RET_DK, RET_DV), F32)],
        name="retention_sample",
    )(r, r, r, r, state_ret, cos2, sin2, dec, qd, kd, cd, gn)


def _memkv_kernel(m_ref, g_ref, wk_ref, wv_ref, kf_ref, vf_ref, kb_ref, vb_ref):
    mn = _rms(m_ref[...], g_ref[...]).astype(BF16)
    for w_ref, flat_ref, b_ref in ((wk_ref, kf_ref, kb_ref), (wv_ref, vf_ref, vb_ref)):
        y = _dot(mn, w_ref[...])
        b_ref[...] = y.astype(BF16)
        for h in range(CA_HEADS):
            for half in range(2):
                c0 = (h * 2 + half) * LANES
                flat_ref[0, pl.ds(half * CA_HEADS + h, N_MEM, stride=2 * CA_HEADS), :] = y[:, c0:c0 + LANES]


def _memkv(mem, layer, g, wk, wv, prev):
    t = mem.shape[0]
    row = lambda: pl.BlockSpec((N_MEM, D_MODEL), lambda i: (i, 0))
    flat = lambda: pl.BlockSpec((1, KV_ROWS, LANES), lambda i: (layer, i, 0))
    flat_shape = jax.ShapeDtypeStruct((DEPTH, BATCH * KV_ROWS, LANES), F32)
    square = (D_MODEL, D_MODEL)
    return _layered_call(
        _memkv_kernel, 4, layer, {0: prev[0], 1: prev[1]},
        make_call=functools.partial(_weights_call, layer=layer, weights={2: square, 3: square}),
        grid=(t // N_MEM,),
        in_specs=[row(), _resident((1, D_MODEL), layer), None, None],
        out_specs=[flat(), flat(), row(), row()],
        out_shape=[flat_shape, flat_shape, jax.ShapeDtypeStruct((t, D_MODEL), BF16),
                   jax.ShapeDtypeStruct((t, D_MODEL), BF16)],
        name="memory_kv",
    )(mem, g, wk, wv)


def _mix_out_q(x1, conv, ret, wo_ref, gca_ref, wq_ref):
    mix = jnp.concatenate([conv, ret], axis=-1).astype(BF16)
    x2 = x1 + _dot(mix, wo_ref[...])
    q = _dot(_rms(x2, gca_ref[...]).astype(BF16), wq_ref[...])
    return x2, q


def _finish(x2, o, wco_ref, g2_ref, w1_ref, w3_ref, w2_ref, gf_ref, final):
    x3 = x2 + _dot(o, wco_ref[...])
    x4 = _ffn(x3, g2_ref, w1_ref, w3_ref, w2_ref)
    return _rms(x4, gf_ref[...]) if final else x4


def _post_p_kernel(x1_ref, conv_ref, ret_ref, mk_ref, mv_ref, wo_ref, gca_ref, wq_ref, wco_ref,
                   g2_ref, w1_ref, w3_ref, w2_ref, gf_ref, y_ref, *, final):
    x2, q = _mix_out_q(x1_ref[...], conv_ref[...], ret_ref[...], wo_ref, gca_ref, wq_ref)
    cols = [slice(h * CA_HEAD_DIM, (h + 1) * CA_HEAD_DIM) for h in range(CA_HEADS)]
    qb = q.astype(BF16)
    scores = [_dot_nt(qb[:, sl], mk_ref[:, sl]) * (CA_HEAD_DIM ** -0.5) for sl in cols]
    probs = [_softmax_rows(s).astype(BF16) for s in scores]
    heads = [_dot(p, mv_ref[:, sl]) for p, sl in zip(probs, cols)]
    o = jnp.concatenate(heads, axis=-1).astype(BF16)
    y_ref[...] = _finish(x2, o, wco_ref, g2_ref, w1_ref, w3_ref, w2_ref, gf_ref, final)


def _post_p(x1, conv, ret, mk, mv, layer, wo, gca, wq, wco, g2, w1, w3, w2, gf, final):
    row = lambda n: pl.BlockSpec((TM, n), lambda i: (i, 0))
    mem = lambda: pl.BlockSpec((N_MEM, D_MODEL), lambda i: (i // (SEQ // TM), 0))
    square = (D_MODEL, D_MODEL)
    return _weights_call(
        functools.partial(_post_p_kernel, final=final), layer,
        {5: square, 7: square, 8: square, 10: (D_MODEL, D_FF), 11: (D_MODEL, D_FF), 12: (D_FF, D_MODEL)},
        emit=True,
        grid=(T_P // TM,),
        in_specs=[row(D_MODEL), row(CONV_CH), row(RET_WIDTH), mem(), mem(),
                  None, _resident((1, D_MODEL), layer), None, None, _resident((1, D_MODEL), layer),
                  None, None, None,
                  pl.BlockSpec((1, D_MODEL), lambda i: (0, 0))],
        out_specs=[row(D_MODEL)],
        out_shape=[jax.ShapeDtypeStruct((T_P, D_MODEL), F32)],
        compiler_params=pltpu.CompilerParams(vmem_limit_bytes=VMEM_LIMIT),
        name="post_prompt",
    )(x1, conv, ret, mk, mv, wo, gca, wq, wco, g2, w1, w3, w2, gf)


def _post_a_s_kernel(x1_ref, conv_ref, ret_ref, wo_ref, gca_ref, wq_ref, x2_ref, q_ref):
    x2, q = _mix_out_q(x1_ref[...], conv_ref[...], ret_ref[...], wo_ref, gca_ref, wq_ref)
    x2_ref[...] = x2
    q_ref[...] = q


def _post_a_s(x1, conv, ret, layer, wo, gca, wq):
    row = lambda n: pl.BlockSpec((TM, n), lambda i: (i, 0))
    return pl.pallas_call(
        _post_a_s_kernel,
        grid=(T_S // TM,),
        in_specs=[row(D_MODEL), row(CONV_CH), row(RET_WIDTH), _whole(wo), _resident((1, D_MODEL), layer),
                  _whole(wq)],
        out_specs=[row(D_MODEL), row(D_MODEL)],
        out_shape=[jax.ShapeDtypeStruct((T_S, D_MODEL), F32)] * 2,
        name="mix_out_q_sample",
    )(x1, conv, ret, wo, gca, wq)


def _post_b_s_kernel(x2_ref, o_ref, wco_ref, g2_ref, w1_ref, w3_ref, w2_ref, gf_ref, y_ref, *, final):
    y_ref[...] = _finish(x2_ref[...], o_ref[...].astype(BF16), wco_ref, g2_ref, w1_ref, w3_ref, w2_ref,
                         gf_ref, final)


def _post_b_s(x2, o, layer, wco, g2, w1, w3, w2, gf, final):
    row = lambda: pl.BlockSpec((TM, D_MODEL), lambda i: (i, 0))
    return pl.pallas_call(
        functools.partial(_post_b_s_kernel, final=final),
        grid=(T_S // TM,),
        in_specs=[row(), row(), _whole(wco), _resident((1, D_MODEL), layer), _whole(w1), _whole(w3), _whole(w2),
                  pl.BlockSpec((1, D_MODEL), lambda i: (0, 0))],
        out_specs=row(),
        out_shape=jax.ShapeDtypeStruct((T_S, D_MODEL), F32),
        compiler_params=pltpu.CompilerParams(vmem_limit_bytes=VMEM_LIMIT),
        name="attn_out_ffn_sample",
    )(x2, o, wco, g2, w1, w3, w2, gf)


def _attn_s_kernel(q_ref, k_ref, v_ref, o_ref):
    nh = CA_HEADS * SUBLANES
    lane = lax.broadcasted_iota(jnp.int32, (nh, KV_ROWS), 1)
    row = lax.broadcasted_iota(jnp.int32, (nh, KV_ROWS), 0)
    valid = (lane % (2 * CA_HEADS)) == (row // SUBLANES)
    second = lax.broadcasted_iota(jnp.int32, (2 * nh, LANES), 0) % SUBLANES >= DEC_SEQ
    batches = range(ATT_S_BT)
    qts = []
    for pr in range(ATT_S_BT // PAIR):
        q = q_ref[pr * SUBLANES:(pr + 1) * SUBLANES, :]
        blocks = [q[:, (h * 2 + half) * LANES:(h * 2 + half + 1) * LANES]
                  for half in range(2) for h in range(CA_HEADS)]
        qts.append(jnp.concatenate(blocks, axis=0).astype(BF16))
    sts = [_dot_nt(qts[b // PAIR], k_ref[0, b].astype(BF16)) for b in batches]
    ws = []
    for st in sts:
        s = (st[:nh] + pltpu.roll(st[nh:], KV_ROWS - CA_HEADS, 1)) * (CA_HEAD_DIM ** -0.5)
        p = _softmax_rows(jnp.where(valid, s, NEG_INF))
        ws.append(jnp.concatenate([p, pltpu.roll(p, CA_HEADS, 1)], axis=0).astype(BF16))
    outs = [_dot(ws[b], v_ref[0, b].astype(BF16)) for b in batches]
    for pr in range(ATT_S_BT // PAIR):
        o = jnp.where(second, outs[pr * PAIR + 1], outs[pr * PAIR])
        for half in range(2):
            for h in range(CA_HEADS):
                r0 = (half * CA_HEADS + h) * SUBLANES
                c0 = (h * 2 + half) * LANES
                o_ref[pr * SUBLANES:(pr + 1) * SUBLANES, c0:c0 + LANES] = o[r0:r0 + SUBLANES]


def _flat_cache(c):
    c = c.reshape(DEPTH, DEC_BATCH, N_MEM, CA_HEADS, 2, LANES)
    return c.transpose(0, 1, 2, 4, 3, 5).reshape(DEPTH, DEC_BATCH, KV_ROWS, LANES)


def _unflat_cache(f):
    f = f.reshape(DEPTH, BATCH, N_MEM, 2, CA_HEADS, LANES)
    return f.transpose(0, 1, 2, 4, 3, 5).reshape(DEPTH, BATCH, N_MEM, CA_HEADS, CA_HEAD_DIM)


def _attn_s(q, cache_k, cache_v, layer):
    bt = ATT_S_BT
    qs = lambda: pl.BlockSpec((bt * DEC_SEQ, D_MODEL), lambda i: (i, 0))
    kv = lambda: pl.BlockSpec((1, bt, KV_ROWS, LANES), lambda i: (layer, i, 0, 0))
    return pl.pallas_call(
        _attn_s_kernel,
        grid=(DEC_BATCH // bt,),
        in_specs=[qs(), kv(), kv()],
        out_specs=qs(),
        out_shape=jax.ShapeDtypeStruct((T_S, D_MODEL), F32),
        compiler_params=pltpu.CompilerParams(vmem_limit_bytes=VMEM_LIMIT),
        name="cross_attn_sample",
    )(q, cache_k, cache_v)


def _rope_tables(pos):
    inv_freq = ROPE_BASE ** (-jnp.arange(0, RET_DK, 2, dtype=F32) / RET_DK)
    ang = pos[:, None] * inv_freq[None, :]
    cos, sin = jnp.cos(ang), jnp.sin(ang)
    return jnp.concatenate([cos, cos], axis=-1), jnp.concatenate([-sin, sin], axis=-1)


def _decay_tables(c):
    log_gamma = jnp.log1p(-jnp.exp2(-5.0 - jnp.arange(RET_HEADS, dtype=F32)))
    idx = jnp.arange(c, dtype=F32)
    rel = idx[:, None] - idx[None, :]
    decay = jnp.where(rel[None] >= 0,
                      jnp.exp(log_gamma[:, None, None] * jnp.maximum(rel, 0.0)[None]), 0.0)
    q_dec = jnp.exp(log_gamma[:, None] * (idx[None, :] + 1.0))
    k_dec = jnp.exp(log_gamma[:, None] * (c - 1.0 - idx[None, :]))
    chunk_dec = jnp.exp(log_gamma * c)
    return decay, q_dec, k_dec, chunk_dec


def _prompt_tables():
    cos2, sin2 = _rope_tables(jnp.arange(SEQ, dtype=F32))
    decay, q_dec, k_dec, chunk_dec = _decay_tables(RET_CHUNK)
    lanes = (RET_HEADS, RET_CHUNK, RET_DK)
    return (cos2, sin2, decay,
            jnp.broadcast_to(q_dec[:, :, None], lanes),
            jnp.broadcast_to(k_dec[:, :, None], lanes),
            jnp.broadcast_to(chunk_dec[:, None, None], lanes))


def _sample_tables():
    slab = lambda a: jnp.concatenate([a] * PAIR, axis=-2)
    cos2, sin2 = _rope_tables(PAST_LEN + jnp.arange(DEC_SEQ, dtype=F32))
    decay, q_dec, k_dec, chunk_dec = _decay_tables(DEC_SEQ)
    rows = (RET_HEADS, DEC_SEQ, RET_DK)
    dec = jnp.broadcast_to(jnp.swapaxes(decay, 1, 2)[:, :, :, None], (RET_HEADS, DEC_SEQ, DEC_SEQ, RET_DK))
    return (slab(cos2), slab(sin2), slab(dec),
            slab(jnp.broadcast_to(q_dec[:, :, None], rows)),
            slab(jnp.broadcast_to(k_dec[:, :, None], rows)),
            jnp.broadcast_to(chunk_dec[:, None, None], (RET_HEADS, RET_DK, RET_DV)))


def kernel(x_prompt, x_sample, state_conv, state_ret, cache_mem_k, cache_mem_v, mem_prompt, g_ffn1, w1_ffn1, w3_ffn1, w2_ffn1, g_mix, w_in, conv_w, conv_b, conv_ln_g, conv_ln_b, ret_gn_g, w_out, g_ca, g_mem, w_cq, w_ck, w_cv, w_co, g_ffn2, w1_ffn2, w3_ffn2, w2_ffn2, g_final):
    vec = lambda g: g.reshape(DEPTH, 1, -1)
    w1a, w3a, w2a, w1b, w3b, w2b = w1_ffn1, w3_ffn1, w2_ffn1, w1_ffn2, w3_ffn2, w2_ffn2
    win, wout, wcq, wck, wcv, wco = w_in, w_out, w_cq, w_ck, w_cv, w_co
    g1, gm, gca, gmem, g2 = map(vec, (g_ffn1, g_mix, g_ca, g_mem, g_ffn2))
    cb, clg, clb, gn = map(vec, (conv_b, conv_ln_g, conv_ln_b, ret_gn_g))
    gf = g_final.reshape(1, D_MODEL)
    cache_k = _flat_cache(cache_mem_k)
    cache_v = _flat_cache(cache_mem_v)
    mem = mem_prompt.reshape(BATCH * N_MEM, D_MODEL)
    tabs_p = _prompt_tables()
    tabs_s = _sample_tables()
    cw8 = conv_w.reshape(DEPTH, CONV_K, CONV_LT, LANES).transpose(0, 2, 1, 3)
    cw8 = jnp.broadcast_to(cw8[:, :, :, None, :], (DEPTH, CONV_LT, CONV_K, SUBLANES, LANES))

    state_t = jnp.swapaxes(state_conv, 1, 2)

    xp = x_prompt.reshape(T_P, D_MODEL)
    xs = x_sample.reshape(T_S, D_MODEL)
    conv_p = ret_p = conv_s = ret_s = None
    mem_p = (None, None)
    for l in range(DEPTH):
        final = l == DEPTH - 1
        memk_f, memv_f, mk, mv = _memkv(mem, l, gmem, wck, wcv, mem_p)
        mem_p = (memk_f, memv_f)
        x1, u, r, *mix_w = _ffn_inproj(xp, l, g1, gm, (w1a, w3a, w2a, win), cast=True)
        cv, conv_p = _conv_p(u, l, cw8, cb, clg, clb, conv_p)
        rt, ret_p = _ret_p(r, l, tabs_p, gn, ret_p)
        xp, wout_b, wcq_b, wco_b, w1_b, w3_b, w2_b = _post_p(
            x1, cv, rt, mk, mv, l, wout, gca, wcq, wco, g2, w1b, w3b, w2b, gf, final)

        x1, u, r = _ffn_inproj(xs, l, g1, gm, mix_w, cast=False)
        ut = jnp.swapaxes(u.reshape(DEC_BATCH, DEC_SEQ, CONV_CH), 0, 1)
        cv, conv_s = _conv_s(state_t, ut, l, conv_w, cb, clg, clb, conv_s)
        cv = jnp.swapaxes(cv, 0, 1).reshape(T_S, CONV_CH)
        rt, ret_s = _ret_s(r, state_ret, l, tabs_s, gn, ret_s)
        x2, q = _post_a_s(x1, cv, rt, l, wout_b, gca, wcq_b)
        o = _attn_s(q, cache_k, cache_v, l)
        xs = _post_b_s(x2, o, l, wco_b, g2, w1_b, w3_b, w2_b, gf, final)

    return (xp.reshape(BATCH, SEQ, D_MODEL), xs.reshape(DEC_BATCH, DEC_SEQ, D_MODEL),
            conv_p, ret_p, _unflat_cache(mem_p[0]), _unflat_cache(mem_p[1]),
            jnp.swapaxes(conv_s, 1, 2), ret_s)
```

```python
import functools

import jax
import jax.numpy as jnp
from jax import lax
from jax.experimental import pallas as pl
from jax.experimental.pallas import tpu as pltpu

F32 = jnp.float32
BF16 = jnp.bfloat16

D_MODEL = 1024
BATCH = 8
SEQ = 2048
DEPTH = 2
DEC_BATCH = 128
DEC_SEQ = 4
PAST_LEN = 16384
CONV_CH = 512
CONV_K = 31
RET_HEADS = 4
RET_DK = 128
RET_DV = 128
RET_WIDTH = 512
MIX_IN = 3072
RET_CHUNK = 128
ROPE_BASE = 10000.0
D_FF = 4096
N_MEM = 256
CA_HEADS = 4
CA_HEAD_DIM = 256
EPS = 1e-6
GN_EPS = 1e-5

T_P = BATCH * SEQ
T_S = DEC_BATCH * DEC_SEQ

TM = 512
TF = 1024
CONV_TT = 512
CONV_RC = 128
CONV_HALO = 32
SUBLANES = 8
LANES = 128
CONV_ROWS = CONV_HALO + CONV_TT
CONV_LT = CONV_CH // LANES
CONV_NORM_UNROLL = 4
RET_CPS = 4
RET_PB = 2
PAIR = SUBLANES // DEC_SEQ
RET_S_BT = 8
ATT_S_BT = 8
KV_ROWS = N_MEM * 2 * CA_HEADS
NEG_INF = float("-inf")
CONV_S_BT = 32
STAGE_BYTES = 512 * 1024
STAGE_SLOTS = 8
VMEM_LIMIT = 60 * 1024 * 1024


def _dot(a, b):
    return jnp.dot(a, b, preferred_element_type=F32)


def _dot_nt(a, b):
    return lax.dot_general(a, b, (((1,), (1,)), ((), ())), preferred_element_type=F32)


def _rms(x, g):
    return x * lax.rsqrt(jnp.mean(x * x, axis=-1, keepdims=True) + EPS) * g


def _silu(x):
    return x * jax.nn.sigmoid(x)


def _ffn(x, g_ref, w1_ref, w3_ref, w2_ref):
    xn = _rms(x, g_ref[...]).astype(BF16)
    acc = None
    for c in range(D_FF // TF):
        sl = slice(c * TF, (c + 1) * TF)
        h1 = _dot(xn, w1_ref[:, sl])
        h3 = _dot(xn, w3_ref[:, sl])
        a = (_silu(h1) * h3).astype(BF16)
        part = _dot(a, w2_ref[sl, :])
        acc = part if acc is None else acc + part
    return x + 0.5 * acc


def _softmax_rows(s):
    m = jnp.max(s, axis=-1, keepdims=True)
    e = jnp.exp(s - m)
    return e * (1.0 / jnp.sum(e, axis=-1, keepdims=True))


def _group_norm_gate(o, gate, gn):
    mu = jnp.mean(o, axis=-1, keepdims=True)
    d = o - mu
    var = jnp.mean(d * d, axis=-1, keepdims=True)
    return _silu(gate) * (d * lax.rsqrt(var + GN_EPS) * gn)


def _rotary(t, cos2, sin2):
    return t * cos2 + pltpu.roll(t, RET_DK // 2, 1) * sin2


def _ffn_inproj_kernel(x_ref, g1_ref, w1_ref, w3_ref, w2_ref, gm_ref, win_ref,
                       x1_ref, u_ref, r_ref):
    x1 = _ffn(x_ref[...], g1_ref, w1_ref, w3_ref, w2_ref)
    x1_ref[...] = x1
    h = _rms(x1, gm_ref[...]).astype(BF16)
    proj = _dot(h, win_ref[...])
    u_ref[...] = proj[:, :CONV_CH] * jax.nn.sigmoid(proj[:, CONV_CH:2 * CONV_CH])
    r_ref[...] = proj[:, 2 * CONV_CH:]


def _resident(shape, layer):
    nd = len(shape)
    return pl.BlockSpec((None,) + tuple(shape), lambda *_: (layer,) + (0,) * nd,
                        pipeline_mode=pl.Buffered(1))


def _whole(a):
    return pl.BlockSpec(a.shape, lambda *_: (0,) * a.ndim, pipeline_mode=pl.Buffered(1))


def _stream_cast(w_hbm, layer, dst_ref):
    rows, cols = dst_ref.shape
    ch = 1 << ((STAGE_BYTES // (4 * cols)).bit_length() - 1)
    assert ch % SUBLANES == 0 and rows % ch == 0
    n = rows // ch

    ahead = min(STAGE_SLOTS - 1, n)

    def run(stage, sem):
        def chunk_copy(k):
            slot = lax.rem(k, STAGE_SLOTS)
            return pltpu.make_async_copy(w_hbm.at[layer, pl.ds(k * ch, ch), :], stage.at[slot], sem.at[slot])

        for k in range(ahead):
            chunk_copy(k).start()

        def body(k, carry):
            @pl.when(k + ahead < n)
            def _():
                chunk_copy(k + ahead).start()

            chunk_copy(k).wait()
            dst_ref[pl.ds(pl.multiple_of(k * ch, ch), ch), :] = stage[lax.rem(k, STAGE_SLOTS)].astype(BF16)
            return carry

        lax.fori_loop(0, n, body, 0)

    pl.run_scoped(run, pltpu.VMEM((STAGE_SLOTS, ch, cols), F32), pltpu.SemaphoreType.DMA((STAGE_SLOTS,)))


def _weights_call(body, layer, weights, emit=False, **kw):
    idx = sorted(weights)
    nw = len(idx)
    in_specs = list(kw.pop("in_specs"))
    for i in idx:
        in_specs[i] = pl.BlockSpec(memory_space=pl.ANY)
    scratch = list(kw.pop("scratch_shapes", ())) + [pltpu.VMEM(weights[i], BF16) for i in idx]
    out_specs, out_shape = list(kw.pop("out_specs")), list(kw.pop("out_shape"))
    n_io = len(in_specs) + len(out_specs)
    if emit:
        assert kw["grid"][0] >= 2
        out_specs += [pl.BlockSpec(memory_space=pl.ANY)] * nw
        out_shape += [jax.ShapeDtypeStruct(weights[i], BF16) for i in idx]
        scratch.append(pltpu.SemaphoreType.DMA((nw,)))

    def with_weights(*refs):
        refs = list(refs)
        sem = refs.pop() if emit else None
        slabs = refs[len(refs) - nw:]
        del refs[len(refs) - nw:]
        copies = []
        if emit:
            copies = [pltpu.make_async_copy(slab, out, sem.at[k])
                      for k, (slab, out) in enumerate(zip(slabs, refs[n_io:n_io + nw]))]
            del refs[n_io:n_io + nw]

        @pl.when(pl.program_id(0) == 0)
        def _():
            for i, slab in zip(idx, slabs):
                _stream_cast(refs[i], layer, slab)

        if emit:
            @pl.when(pl.program_id(0) == 1)
            def _():
                for c in copies:
                    c.start()

        for i, slab in zip(idx, slabs):
            refs[i] = slab
        body(*refs)

        if emit:
            @pl.when(pl.program_id(0) == pl.num_programs(0) - 1)
            def _():
                for c in copies:
                    c.wait()

    return pl.pallas_call(with_weights, in_specs=in_specs, out_specs=out_specs, out_shape=out_shape,
                          scratch_shapes=scratch, **kw)


def _layered_call(body, n_in, layer, prev, make_call=pl.pallas_call, **kw):
    idx = sorted(prev)
    if all(prev[o] is None for o in idx):
        out_specs = list(kw.pop("out_specs"))
        for o in idx:
            spec = out_specs[o]
            out_specs[o] = pl.BlockSpec(
                (DEPTH,) + tuple(spec.block_shape[1:]),
                lambda *a, index_map=spec.index_map: (0,) + tuple(index_map(*a))[1:])

        def with_fill(*refs):
            refs = list(refs)
            for o in idx:
                full = refs[n_in + o]
                for d in range(DEPTH):
                    if d != layer:
                        full[d] = jnp.zeros(full.shape[1:], full.dtype)
                refs[n_in + o] = full.at[pl.ds(layer, 1)]
            return body(*refs)

        return make_call(with_fill, out_specs=out_specs, **kw)

    kw["in_specs"] = list(kw["in_specs"]) + [pl.BlockSpec(memory_space=pl.ANY)] * len(idx)

    def with_aliased(*refs):
        return body(*refs[:n_in], *refs[n_in + len(idx):])

    call = make_call(with_aliased, input_output_aliases={n_in + k: o for k, o in enumerate(idx)}, **kw)
    return lambda *args: call(*args, *[prev[o] for o in idx])


def _ffn_inproj(x, layer, g1, gm, w, cast):
    t = x.shape[0]
    row = lambda n: pl.BlockSpec((TM, n), lambda i: (i, 0))
    kw = dict(
        grid=(t // TM,),
        out_specs=[row(D_MODEL), row(CONV_CH), row(MIX_IN - 2 * CONV_CH)],
        out_shape=[jax.ShapeDtypeStruct((t, D_MODEL), F32),
                   jax.ShapeDtypeStruct((t, CONV_CH), F32),
                   jax.ShapeDtypeStruct((t, MIX_IN - 2 * CONV_CH), F32)],
        compiler_params=pltpu.CompilerParams(vmem_limit_bytes=VMEM_LIMIT),
        name="ffn_inproj")
    vec = _resident((1, D_MODEL), layer)
    args = (x, g1, w[0], w[1], w[2], gm, w[3])
    if not cast:
        return pl.pallas_call(
            _ffn_inproj_kernel,
            in_specs=[row(D_MODEL), vec, _whole(w[0]), _whole(w[1]), _whole(w[2]), vec, _whole(w[3])], **kw)(*args)
    return _weights_call(
        _ffn_inproj_kernel, layer,
        {2: (D_MODEL, D_FF), 3: (D_MODEL, D_FF), 4: (D_FF, D_MODEL), 6: (D_MODEL, MIX_IN)}, emit=True,
        in_specs=[row(D_MODEL), vec, None, None, None, vec, None], **kw)(*args)


def _layer_norm_silu(y, lg, lb):
    mu = jnp.mean(y, axis=-1, keepdims=True)
    d = y - mu
    var = jnp.mean(d * d, axis=-1, keepdims=True)
    return _silu(d * lax.rsqrt(var + EPS) * lg + lb)


def _conv_p_kernel(u_ref, w_ref, b_ref, lg_ref, lb_ref, y_ref, nb_ref, win_ref, acc_ref):
    j = pl.program_id(1)
    for t in range(CONV_LT):
        @pl.when(j == 0)
        def _():
            win_ref[0, t, 0:CONV_HALO, :] = jnp.zeros((CONV_HALO, LANES), F32)

        @pl.when(j > 0)
        def _():
            win_ref[0, t, 0:CONV_HALO, :] = win_ref[0, t, CONV_TT:CONV_TT + CONV_HALO, :]

        win_ref[0, t, CONV_HALO:CONV_ROWS, :] = u_ref[:, t * LANES:(t + 1) * LANES]
        win_ref[0, t, CONV_ROWS:CONV_ROWS + SUBLANES, :] = jnp.zeros((SUBLANES, LANES), F32)
        for sh in range(1, SUBLANES):
            win_ref[sh, t, 0:CONV_ROWS, :] = win_ref[0, t, sh:sh + CONV_ROWS, :]

    first = CONV_HALO - (CONV_K - 1)
    n_rc = CONV_TT // CONV_RC

    def taps(idx, carry):
        t = idx // n_rc
        r0 = pl.multiple_of((idx % n_rc) * CONV_RC, CONV_RC)
        accs = [None] * (CONV_RC // SUBLANES)
        for k in range(CONV_K):
            sh = (first + k) % SUBLANES
            wk = w_ref[t, k]
            for i in range(CONV_RC // SUBLANES):
                a0 = pl.multiple_of(r0 + (first + k - sh) + i * SUBLANES, SUBLANES)
                term = wk * win_ref[sh, t, pl.ds(a0, SUBLANES), :]
                accs[i] = term if accs[i] is None else accs[i] + term
        acc_ref[t, pl.ds(r0, CONV_RC), :] = jnp.concatenate(accs, axis=0)
        return carry

    lax.fori_loop(0, CONV_LT * n_rc, taps, 0)

    def norm(c, carry):
        r0 = pl.multiple_of(c * CONV_RC, CONV_RC)
        y = jnp.concatenate([acc_ref[t, pl.ds(r0, CONV_RC), :] for t in range(CONV_LT)], axis=-1) + b_ref[...]
        y_ref[pl.ds(r0, CONV_RC), :] = _layer_norm_silu(y, lg_ref[...], lb_ref[...])
        return carry

    lax.fori_loop(0, n_rc, norm, 0, unroll=CONV_NORM_UNROLL)

    @pl.when(j == pl.num_programs(1) - 1)
    def _():
        nb_ref[0, 0] = u_ref[CONV_TT - (CONV_K - 1):CONV_TT, :]


def _conv_p(u, layer, conv_w, conv_b, ln_g, ln_b, prev_nb):
    nt = SEQ // CONV_TT
    vec = lambda: pl.BlockSpec((None, 1, CONV_CH), lambda b, j: (layer, 0, 0))
    return _layered_call(
        _conv_p_kernel, 5, layer, {1: prev_nb},
        grid=(BATCH, nt),
        in_specs=[pl.BlockSpec((CONV_TT, CONV_CH), lambda b, j: (b * nt + j, 0)),
                  pl.BlockSpec((None, CONV_LT, CONV_K, SUBLANES, LANES), lambda b, j: (layer, 0, 0, 0, 0)),
                  vec(), vec(), vec()],
        out_specs=[pl.BlockSpec((CONV_TT, CONV_CH), lambda b, j: (b * nt + j, 0)),
                   pl.BlockSpec((1, 1, CONV_K - 1, CONV_CH), lambda b, j: (layer, b, 0, 0))],
        out_shape=[jax.ShapeDtypeStruct((T_P, CONV_CH), F32),
                   jax.ShapeDtypeStruct((DEPTH, BATCH, CONV_K - 1, CONV_CH), F32)],
        scratch_shapes=[pltpu.VMEM((SUBLANES, CONV_LT, CONV_ROWS + SUBLANES, LANES), F32),
                        pltpu.VMEM((CONV_LT, CONV_TT, LANES), F32)],
        name="conv_prompt",
    )(u, conv_w, conv_b, ln_g, ln_b)


def _conv_s_kernel(st_ref, ut_ref, w_ref, b_ref, lg_ref, lb_ref, y_ref, nb_ref):
    keep = CONV_K - 1 - DEC_SEQ
    for t in range(DEC_SEQ):
        y = None
        for j in range(t, CONV_K - 1):
            term = st_ref[0, j] * w_ref[j - t:j - t + 1, :]
            y = term if y is None else y + term
        for j in range(t + 1):
            k = CONV_K - 1 - t + j
            y = y + ut_ref[j] * w_ref[k:k + 1, :]
        y_ref[t] = _layer_norm_silu(y + b_ref[...], lg_ref[...], lb_ref[...])
    for j in range(keep):
        nb_ref[0, j] = st_ref[0, j + DEC_SEQ]
    for t in range(DEC_SEQ):
        nb_ref[0, keep + t] = ut_ref[t]


def _conv_s(state_t, ut, layer, conv_w, conv_b, ln_g, ln_b, prev_nb):
    bt = CONV_S_BT
    vec = lambda: pl.BlockSpec((None, 1, CONV_CH), lambda i: (layer, 0, 0))
    return _layered_call(
        _conv_s_kernel, 6, layer, {1: prev_nb},
        grid=(DEC_BATCH // bt,),
        in_specs=[pl.BlockSpec((1, CONV_K - 1, bt, CONV_CH), lambda i: (layer, 0, i, 0)),
                  pl.BlockSpec((DEC_SEQ, bt, CONV_CH), lambda i: (0, i, 0)),
                  pl.BlockSpec((None, CONV_K, CONV_CH), lambda i: (layer, 0, 0)),
                  vec(), vec(), vec()],
        out_specs=[pl.BlockSpec((DEC_SEQ, bt, CONV_CH), lambda i: (0, i, 0)),
                   pl.BlockSpec((1, CONV_K - 1, bt, CONV_CH), lambda i: (layer, 0, i, 0))],
        out_shape=[jax.ShapeDtypeStruct((DEC_SEQ, DEC_BATCH, CONV_CH), F32),
                   jax.ShapeDtypeStruct((DEPTH, CONV_K - 1, DEC_BATCH, CONV_CH), F32)],
        name="conv_sample",
    )(state_t, ut, conv_w, conv_b, ln_g, ln_b)


def _ret_p_kernel(q_ref, k_ref, v_ref, g_ref, cos_ref, sin_ref, dec_ref, qd_ref, kd_ref,
                  cd_ref, gn_ref, o_ref, ns_ref, s_ref):
    j = pl.program_id(1)

    @pl.when(j == 0)
    def _():
        s_ref[...] = jnp.zeros_like(s_ref)

    units = [(e, h, slice(h * RET_DK, (h + 1) * RET_DK)) for h in range(RET_HEADS) for e in range(RET_PB)]
    chunks = []
    for c in range(RET_CPS):
        rows = slice(c * RET_CHUNK, (c + 1) * RET_CHUNK)
        cos2 = cos_ref[rows, :]
        sin2 = sin_ref[rows, :]
        qs = [_rotary(q_ref[e, rows, sl], cos2, sin2).astype(BF16) for e, h, sl in units]
        ks = [_rotary(k_ref[e, rows, sl], cos2, sin2) * (RET_DK ** -0.5) for e, h, sl in units]
        vs = [v_ref[e, rows, sl].astype(BF16) for e, h, sl in units]
        scores = [(_dot_nt(qb, kh.astype(BF16)) * dec_ref[h]).astype(BF16)
                  for qb, kh, (e, h, sl) in zip(qs, ks, units)]
        intra = [_dot(sc, vb) for sc, vb in zip(scores, vs)]
        upd = [lax.dot_general((kh * kd_ref[h]).astype(BF16), vb, (((0,), (0,)), ((), ())),
                               preferred_element_type=F32) for kh, vb, (e, h, sl) in zip(ks, vs, units)]
        chunks.append((rows, qs, intra, upd))

    states = [s_ref[e, h] for e, h, sl in units]
    outs = []
    for rows, qs, intra, upd in chunks:
        outs.append([a + _dot(qb, s_h.astype(BF16)) * qd_ref[h]
                     for a, qb, s_h, (e, h, sl) in zip(intra, qs, states, units)])
        states = [s_h * cd_ref[h] + u for s_h, u, (e, h, sl) in zip(states, upd, units)]

    for (rows, _, _, _), out in zip(chunks, outs):
        for o, (e, h, sl) in zip(out, units):
            o_ref[e, rows, sl] = _group_norm_gate(o, g_ref[e, rows, sl], gn_ref[:, sl])
    for s_h, (e, h, sl) in zip(states, units):
        s_ref[e, h] = s_h

    @pl.when(j == pl.num_programs(1) - 1)
    def _():
        ns_ref[0] = s_ref[...]


def _ret_p(r, layer, tabs, gn, prev_ns):
    cos2, sin2, dec, qd, kd, cd = tabs
    rows = RET_CPS * RET_CHUNK
    r3 = r.reshape(BATCH, SEQ, r.shape[-1])
    col = lambda c: pl.BlockSpec((RET_PB, rows, RET_WIDTH), lambda b, j: (b, j, c))
    const = lambda: pl.BlockSpec((RET_HEADS, RET_CHUNK, RET_DK), lambda b, j: (0, 0, 0))
    tab = lambda: pl.BlockSpec((rows, RET_DK), lambda b, j: (j, 0))
    out, ns = _layered_call(
        _ret_p_kernel, 11, layer, {1: prev_ns},
        grid=(BATCH // RET_PB, SEQ // rows),
        in_specs=[col(0), col(1), col(2), col(3), tab(), tab(), const(), const(), const(), const(),
                  pl.BlockSpec((None, 1, RET_WIDTH), lambda b, j: (layer, 0, 0))],
        out_specs=[pl.BlockSpec((RET_PB, rows, RET_WIDTH), lambda b, j: (b, j, 0)),
                   pl.BlockSpec((1, RET_PB, RET_HEADS, RET_DK, RET_DV), lambda b, j: (layer, b, 0, 0, 0))],
        out_shape=[jax.ShapeDtypeStruct((BATCH, SEQ, RET_WIDTH), F32),
                   jax.ShapeDtypeStruct((DEPTH, BATCH, RET_HEADS, RET_DK, RET_DV), F32)],
        scratch_shapes=[pltpu.VMEM((RET_PB, RET_HEADS, RET_DK, RET_DV), F32)],
        name="retention_prompt",
    )(r3, r3, r3, r3, cos2, sin2, dec, qd, kd, cd, gn)
    return out.reshape(T_P, RET_WIDTH), ns


def _ret_s_kernel(q_ref, k_ref, v_ref, g_ref, st_ref, cos_ref, sin_ref, dec_ref, qd_ref, kd_ref,
                  cd_ref, gn_ref, o_ref, ns_ref):
    cos2 = cos_ref[...]
    sin2 = sin_ref[...]
    zpad = jnp.zeros((RET_DK - SUBLANES, RET_DK), F32)
    second = lax.broadcasted_iota(jnp.int32, (SUBLANES, RET_DK), 0) >= DEC_SEQ

    def pick(a, jj):
        return jnp.where(second, a[DEC_SEQ + jj:DEC_SEQ + jj + 1, :], a[jj:jj + 1, :])

    def body(p, carry):
        rows = pl.ds(pl.multiple_of(p * SUBLANES, SUBLANES), SUBLANES)
        for h in range(RET_HEADS):
            sl = slice(h * RET_DK, (h + 1) * RET_DK)
            qh = _rotary(q_ref[rows, sl], cos2, sin2)
            kh = _rotary(k_ref[rows, sl], cos2, sin2) * (RET_DK ** -0.5)
            vh = v_ref[rows, sl]
            qb = qh.astype(BF16)
            kdec = kh * kd_ref[h]
            vpad = jnp.concatenate([vh, zpad], axis=0).astype(BF16)
            from_state = []
            for x in range(PAIR):
                s_x = st_ref[0, p * PAIR + x, h]
                from_state.append(_dot(qb, s_x.astype(BF16)))
                mine = second if x else jnp.logical_not(second)
                kx = jnp.concatenate([jnp.where(mine, kdec, 0.0), zpad], axis=0)
                upd = _dot(kx.T.astype(BF16), vpad)
                ns_ref[0, p * PAIR + x, h] = s_x * cd_ref[h] + upd
            o = jnp.where(second, from_state[1], from_state[0]) * qd_ref[h]
            for jj in range(DEC_SEQ):
                sj = jnp.sum(qh * pick(kh, jj), axis=-1, keepdims=True)
                o = o + (sj * dec_ref[h, jj]) * pick(vh, jj)
            o_ref[rows, sl] = _group_norm_gate(o, g_ref[rows, sl], gn_ref[:, sl])
        return carry

    lax.fori_loop(0, RET_S_BT // PAIR, body, 0, unroll=2)


def _ret_s(r, state_ret, layer, tabs, gn, prev_ns):
    cos2, sin2, dec, qd, kd, cd = tabs
    bt = RET_S_BT
    col = lambda c: pl.BlockSpec((bt * DEC_SEQ, RET_WIDTH), lambda i: (i, c))
    full = lambda a: pl.BlockSpec(a.shape, lambda i: (0,) * a.ndim)
    st = lambda: pl.BlockSpec((1, bt, RET_HEADS, RET_DK, RET_DV), lambda i: (layer, i, 0, 0, 0))
    return _layered_call(
        _ret_s_kernel, 12, layer, {1: prev_ns},
        grid=(DEC_BATCH // bt,),
        in_specs=[col(0), col(1), col(2), col(3), st(), full(cos2), full(sin2), full(dec), full(qd),
                  full(kd), full(cd), pl.BlockSpec((None, 1, RET_WIDTH), lambda i: (layer, 0, 0))],
        out_specs=[pl.BlockSpec((bt * DEC_SEQ, RET_WIDTH), lambda i: (i, 0)), st()],
        out_shape=[jax.ShapeDtypeStruct((T_S, RET_WIDTH), F32),
                   jax.ShapeDtypeStruct((DEPTH, DEC_BATCH, RET_HEADS, RET_DK, RET_DV), F32)],
        name="retention_sample",
    )(r, r, r, r, state_ret, cos2, sin2, dec, qd, kd, cd, gn)


def _memkv_kernel(m_ref, g_ref, wk_ref, wv_ref, kf_ref, vf_ref, kb_ref, vb_ref):
    mn = _rms(m_ref[...], g_ref[...]).astype(BF16)
    for w_ref, flat_ref, b_ref in ((wk_ref, kf_ref, kb_ref), (wv_ref, vf_ref, vb_ref)):
        y = _dot(mn, w_ref[...])
        b_ref[...] = y.astype(BF16)
        for h in range(CA_HEADS):
            for half in range(2):
                c0 = (h * 2 + half) * LANES
                flat_ref[0, pl.ds(half * CA_HEADS + h, N_MEM, stride=2 * CA_HEADS), :] = y[:, c0:c0 + LANES]


def _memkv(mem, layer, g, wk, wv, prev):
    t = mem.shape[0]
    row = lambda: pl.BlockSpec((N_MEM, D_MODEL), lambda i: (i, 0))
    flat = lambda: pl.BlockSpec((1, KV_ROWS, LANES), lambda i: (layer, i, 0))
    flat_shape = jax.ShapeDtypeStruct((DEPTH, BATCH * KV_ROWS, LANES), F32)
    square = (D_MODEL, D_MODEL)
    return _layered_call(
        _memkv_kernel, 4, layer, {0: prev[0], 1: prev[1]},
        make_call=functools.partial(_weights_call, layer=layer, weights={2: square, 3: square}),
        grid=(t // N_MEM,),
        in_specs=[row(), _resident((1, D_MODEL), layer), None, None],
        out_specs=[flat(), flat(), row(), row()],
        out_shape=[flat_shape, flat_shape, jax.ShapeDtypeStruct((t, D_MODEL), BF16),
                   jax.ShapeDtypeStruct((t, D_MODEL), BF16)],
        name="memory_kv",
    )(mem, g, wk, wv)


def _mix_out_q(x1, conv, ret, wo_ref, gca_ref, wq_ref):
    mix = jnp.concatenate([conv, ret], axis=-1).astype(BF16)
    x2 = x1 + _dot(mix, wo_ref[...])
    q = _dot(_rms(x2, gca_ref[...]).astype(BF16), wq_ref[...])
    return x2, q


def _finish(x2, o, wco_ref, g2_ref, w1_ref, w3_ref, w2_ref, gf_ref, final):
    x3 = x2 + _dot(o, wco_ref[...])
    x4 = _ffn(x3, g2_ref, w1_ref, w3_ref, w2_ref)
    return _rms(x4, gf_ref[...]) if final else x4


def _post_p_kernel(x1_ref, conv_ref, ret_ref, mk_ref, mv_ref, wo_ref, gca_ref, wq_ref, wco_ref,
                   g2_ref, w1_ref, w3_ref, w2_ref, gf_ref, y_ref, *, final):
    x2, q = _mix_out_q(x1_ref[...], conv_ref[...], ret_ref[...], wo_ref, gca_ref, wq_ref)
    cols = [slice(h * CA_HEAD_DIM, (h + 1) * CA_HEAD_DIM) for h in range(CA_HEADS)]
    qb = q.astype(BF16)
    scores = [_dot_nt(qb[:, sl], mk_ref[:, sl]) * (CA_HEAD_DIM ** -0.5) for sl in cols]
    probs = [_softmax_rows(s).astype(BF16) for s in scores]
    heads = [_dot(p, mv_ref[:, sl]) for p, sl in zip(probs, cols)]
    o = jnp.concatenate(heads, axis=-1).astype(BF16)
    y_ref[...] = _finish(x2, o, wco_ref, g2_ref, w1_ref, w3_ref, w2_ref, gf_ref, final)


def _post_p(x1, conv, ret, mk, mv, layer, wo, gca, wq, wco, g2, w1, w3, w2, gf, final):
    row = lambda n: pl.BlockSpec((TM, n), lambda i: (i, 0))
    mem = lambda: pl.BlockSpec((N_MEM, D_MODEL), lambda i: (i // (SEQ // TM), 0))
    square = (D_MODEL, D_MODEL)
    return _weights_call(
        functools.partial(_post_p_kernel, final=final), layer,
        {5: square, 7: square, 8: square, 10: (D_MODEL, D_FF), 11: (D_MODEL, D_FF), 12: (D_FF, D_MODEL)},
        emit=True,
        grid=(T_P // TM,),
        in_specs=[row(D_MODEL), row(CONV_CH), row(RET_WIDTH), mem(), mem(),
                  None, _resident((1, D_MODEL), layer), None, None, _resident((1, D_MODEL), layer),
                  None, None, None,
                  pl.BlockSpec((1, D_MODEL), lambda i: (0, 0))],
        out_specs=[row(D_MODEL)],
        out_shape=[jax.ShapeDtypeStruct((T_P, D_MODEL), F32)],
        compiler_params=pltpu.CompilerParams(vmem_limit_bytes=VMEM_LIMIT),
        name="post_prompt",
    )(x1, conv, ret, mk, mv, wo, gca, wq, wco, g2, w1, w3, w2, gf)


def _post_a_s_kernel(x1_ref, conv_ref, ret_ref, wo_ref, gca_ref, wq_ref, x2_ref, q_ref):
    x2, q = _mix_out_q(x1_ref[...], conv_ref[...], ret_ref[...], wo_ref, gca_ref, wq_ref)
    x2_ref[...] = x2
    q_ref[...] = q


def _post_a_s(x1, conv, ret, layer, wo, gca, wq):
    row = lambda n: pl.BlockSpec((TM, n), lambda i: (i, 0))
    return pl.pallas_call(
        _post_a_s_kernel,
        grid=(T_S // TM,),
        in_specs=[row(D_MODEL), row(CONV_CH), row(RET_WIDTH), _whole(wo), _resident((1, D_MODEL), layer),
                  _whole(wq)],
        out_specs=[row(D_MODEL), row(D_MODEL)],
        out_shape=[jax.ShapeDtypeStruct((T_S, D_MODEL), F32)] * 2,
        name="mix_out_q_sample",
    )(x1, conv, ret, wo, gca, wq)


def _post_b_s_kernel(x2_ref, o_ref, wco_ref, g2_ref, w1_ref, w3_ref, w2_ref, gf_ref, y_ref, *, final):
    y_ref[...] = _finish(x2_ref[...], o_ref[...].astype(BF16), wco_ref, g2_ref, w1_ref, w3_ref, w2_ref,
                         gf_ref, final)


def _post_b_s(x2, o, layer, wco, g2, w1, w3, w2, gf, final):
    row = lambda: pl.BlockSpec((TM, D_MODEL), lambda i: (i, 0))
    return pl.pallas_call(
        functools.partial(_post_b_s_kernel, final=final),
        grid=(T_S // TM,),
        in_specs=[row(), row(), _whole(wco), _resident((1, D_MODEL), layer), _whole(w1), _whole(w3), _whole(w2),
                  pl.BlockSpec((1, D_MODEL), lambda i: (0, 0))],
        out_specs=row(),
        out_shape=jax.ShapeDtypeStruct((T_S, D_MODEL), F32),
        compiler_params=pltpu.CompilerParams(vmem_limit_bytes=VMEM_LIMIT),
        name="attn_out_ffn_sample",
    )(x2, o, wco, g2, w1, w3, w2, gf)


def _attn_s_kernel(q_ref, k_ref, v_ref, o_ref):
    nh = CA_HEADS * SUBLANES
    lane = lax.broadcasted_iota(jnp.int32, (nh, KV_ROWS), 1)
    row = lax.broadcasted_iota(jnp.int32, (nh, KV_ROWS), 0)
    valid = (lane % (2 * CA_HEADS)) == (row // SUBLANES)
    second = lax.broadcasted_iota(jnp.int32, (2 * nh, LANES), 0) % SUBLANES >= DEC_SEQ
    batches = range(ATT_S_BT)
    qts = []
    for pr in range(ATT_S_BT // PAIR):
        q = q_ref[pr * SUBLANES:(pr + 1) * SUBLANES, :]
        blocks = [q[:, (h * 2 + half) * LANES:(h * 2 + half + 1) * LANES]
                  for half in range(2) for h in range(CA_HEADS)]
        qts.append(jnp.concatenate(blocks, axis=0).astype(BF16))
    sts = [_dot_nt(qts[b // PAIR], k_ref[0, b].astype(BF16)) for b in batches]
    ws = []
    for st in sts:
        s = (st[:nh] + pltpu.roll(st[nh:], KV_ROWS - CA_HEADS, 1)) * (CA_HEAD_DIM ** -0.5)
        p = _softmax_rows(jnp.where(valid, s, NEG_INF))
        ws.append(jnp.concatenate([p, pltpu.roll(p, CA_HEADS, 1)], axis=0).astype(BF16))
    outs = [_dot(ws[b], v_ref[0, b].astype(BF16)) for b in batches]
    for pr in range(ATT_S_BT // PAIR):
        o = jnp.where(second, outs[pr * PAIR + 1], outs[pr * PAIR])
        for half in range(2):
            for h in range(CA_HEADS):
                r0 = (half * CA_HEADS + h) * SUBLANES
                c0 = (h * 2 + half) * LANES
                o_ref[pr * SUBLANES:(pr + 1) * SUBLANES, c0:c0 + LANES] = o[r0:r0 + SUBLANES]


def _flat_cache(c):
    c = c.reshape(DEPTH, DEC_BATCH, N_MEM, CA_HEADS, 2, LANES)
    return c.transpose(0, 1, 2, 4, 3, 5).reshape(DEPTH, DEC_BATCH, KV_ROWS, LANES)


def _unflat_cache(f):
    f = f.reshape(DEPTH, BATCH, N_MEM, 2, CA_HEADS, LANES)
    return f.transpose(0, 1, 2, 4, 3, 5).reshape(DEPTH, BATCH, N_MEM, CA_HEADS, CA_HEAD_DIM)


def _attn_s(q, cache_k, cache_v, layer):
    bt = ATT_S_BT
    qs = lambda: pl.BlockSpec((bt * DEC_SEQ, D_MODEL), lambda i: (i, 0))
    kv = lambda: pl.BlockSpec((1, bt, KV_ROWS, LANES), lambda i: (layer, i, 0, 0))
    return pl.pallas_call(
        _attn_s_kernel,
        grid=(DEC_BATCH // bt,),
        in_specs=[qs(), kv(), kv()],
        out_specs=qs(),
        out_shape=jax.ShapeDtypeStruct((T_S, D_MODEL), F32),
        compiler_params=pltpu.CompilerParams(vmem_limit_bytes=VMEM_LIMIT),
        name="cross_attn_sample",
    )(q, cache_k, cache_v)


def _rope_tables(pos):
    inv_freq = ROPE_BASE ** (-jnp.arange(0, RET_DK, 2, dtype=F32) / RET_DK)
    ang = pos[:, None] * inv_freq[None, :]
    cos, sin = jnp.cos(ang), jnp.sin(ang)
    return jnp.concatenate([cos, cos], axis=-1), jnp.concatenate([-sin, sin], axis=-1)


def _decay_tables(c):
    log_gamma = jnp.log1p(-jnp.exp2(-5.0 - jnp.arange(RET_HEADS, dtype=F32)))
    idx = jnp.arange(c, dtype=F32)
    rel = idx[:, None] - idx[None, :]
    decay = jnp.where(rel[None] >= 0,
                      jnp.exp(log_gamma[:, None, None] * jnp.maximum(rel, 0.0)[None]), 0.0)
    q_dec = jnp.exp(log_gamma[:, None] * (idx[None, :] + 1.0))
    k_dec = jnp.exp(log_gamma[:, None] * (c - 1.0 - idx[None, :]))
    chunk_dec = jnp.exp(log_gamma * c)
    return decay, q_dec, k_dec, chunk_dec


def _prompt_tables():
    cos2, sin2 = _rope_tables(jnp.arange(SEQ, dtype=F32))
    decay, q_dec, k_dec, chunk_dec = _decay_tables(RET_CHUNK)
    lanes = (RET_HEADS, RET_CHUNK, RET_DK)
    return (cos2, sin2, decay,
            jnp.broadcast_to(q_dec[:, :, None], lanes),
            jnp.broadcast_to(k_dec[:, :, None], lanes),
            jnp.broadcast_to(chunk_dec[:, None, None], lanes))


def _sample_tables():
    slab = lambda a: jnp.concatenate([a] * PAIR, axis=-2)
    cos2, sin2 = _rope_tables(PAST_LEN + jnp.arange(DEC_SEQ, dtype=F32))
    decay, q_dec, k_dec, chunk_dec = _decay_tables(DEC_SEQ)
    rows = (RET_HEADS, DEC_SEQ, RET_DK)
    dec = jnp.broadcast_to(jnp.swapaxes(decay, 1, 2)[:, :, :, None], (RET_HEADS, DEC_SEQ, DEC_SEQ, RET_DK))
    return (slab(cos2), slab(sin2), slab(dec),
            slab(jnp.broadcast_to(q_dec[:, :, None], rows)),
            slab(jnp.broadcast_to(k_dec[:, :, None], rows)),
            jnp.broadcast_to(chunk_dec[:, None, None], (RET_HEADS, RET_DK, RET_DV)))


def kernel(x_prompt, x_sample, state_conv, state_ret, cache_mem_k, cache_mem_v, mem_prompt, g_ffn1, w1_ffn1, w3_ffn1, w2_ffn1, g_mix, w_in, conv_w, conv_b, conv_ln_g, conv_ln_b, ret_gn_g, w_out, g_ca, g_mem, w_cq, w_ck, w_cv, w_co, g_ffn2, w1_ffn2, w3_ffn2, w2_ffn2, g_final):
    vec = lambda g: g.reshape(DEPTH, 1, -1)
    w1a, w3a, w2a, w1b, w3b, w2b = w1_ffn1, w3_ffn1, w2_ffn1, w1_ffn2, w3_ffn2, w2_ffn2
    win, wout, wcq, wck, wcv, wco = w_in, w_out, w_cq, w_ck, w_cv, w_co
    g1, gm, gca, gmem, g2 = map(vec, (g_ffn1, g_mix, g_ca, g_mem, g_ffn2))
    cb, clg, clb, gn = map(vec, (conv_b, conv_ln_g, conv_ln_b, ret_gn_g))
    gf = g_final.reshape(1, D_MODEL)
    cache_k = _flat_cache(cache_mem_k)
    cache_v = _flat_cache(cache_mem_v)
    mem = mem_prompt.reshape(BATCH * N_MEM, D_MODEL)
    tabs_p = _prompt_tables()
    tabs_s = _sample_tables()
    cw8 = conv_w.reshape(DEPTH, CONV_K, CONV_LT, LANES).transpose(0, 2, 1, 3)
    cw8 = jnp.broadcast_to(cw8[:, :, :, None, :], (DEPTH, CONV_LT, CONV_K, SUBLANES, LANES))

    state_t = jnp.swapaxes(state_conv, 1, 2)

    xp = x_prompt.reshape(T_P, D_MODEL)
    xs = x_sample.reshape(T_S, D_MODEL)
    conv_p = ret_p = conv_s = ret_s = None
    mem_p = (None, None)
    for l in range(DEPTH):
        final = l == DEPTH - 1
        memk_f, memv_f, mk, mv = _memkv(mem, l, gmem, wck, wcv, mem_p)
        mem_p = (memk_f, memv_f)
        x1, u, r, *mix_w = _ffn_inproj(xp, l, g1, gm, (w1a, w3a, w2a, win), cast=True)
        cv, conv_p = _conv_p(u, l, cw8, cb, clg, clb, conv_p)
        rt, ret_p = _ret_p(r, l, tabs_p, gn, ret_p)
        xp, wout_b, wcq_b, wco_b, w1_b, w3_b, w2_b = _post_p(
            x1, cv, rt, mk, mv, l, wout, gca, wcq, wco, g2, w1b, w3b, w2b, gf, final)

        x1, u, r = _ffn_inproj(xs, l, g1, gm, mix_w, cast=False)
        ut = jnp.swapaxes(u.reshape(DEC_BATCH, DEC_SEQ, CONV_CH), 0, 1)
        cv, conv_s = _conv_s(state_t, ut, l, conv_w, cb, clg, clb, conv_s)
        cv = jnp.swapaxes(cv, 0, 1).reshape(T_S, CONV_CH)
        rt, ret_s = _ret_s(r, state_ret, l, tabs_s, gn, ret_s)
        x2, q = _post_a_s(x1, cv, rt, l, wout_b, gca, wcq_b)
        o = _attn_s(q, cache_k, cache_v, l)
        xs = _post_b_s(x2, o, l, wco_b, g2, w1_b, w3_b, w2_b, gf, final)

    return (xp.reshape(BATCH, SEQ, D_MODEL), xs.reshape(DEC_BATCH, DEC_SEQ, D_MODEL),
            conv_p, ret_p, _unflat_cache(mem_p[0]), _unflat_cache(mem_p[1]),
            jnp.swapaxes(conv_s, 1, 2), ret_s)
```

```python
import functools

import jax
import jax.numpy as jnp
from jax import lax
from jax.experimental import pallas as pl
from jax.experimental.pallas import tpu as pltpu

F32 = jnp.float32
BF16 = jnp.bfloat16

D_MODEL = 1024
BATCH = 8
SEQ = 2048
DEPTH = 2
DEC_BATCH = 128
DEC_SEQ = 4
PAST_LEN = 16384
CONV_CH = 512
CONV_K = 31
RET_HEADS = 4
RET_DK = 128
RET_DV = 128
RET_WIDTH = 512
MIX_IN = 3072
RET_CHUNK = 128
ROPE_BASE = 10000.0
D_FF = 4096
N_MEM = 256
CA_HEADS = 4
CA_HEAD_DIM = 256
EPS = 1e-6
GN_EPS = 1e-5

T_P = BATCH * SEQ
T_S = DEC_BATCH * DEC_SEQ

TM = 512
TF = 1024
CONV_TT = 512
CONV_RC = 128
CONV_HALO = 32
SUBLANES = 8
LANES = 128
CONV_ROWS = CONV_HALO + CONV_TT
CONV_LT = CONV_CH // LANES
CONV_NORM_UNROLL = 4
RET_CPS = 4
RET_PB = 2
PAIR = SUBLANES // DEC_SEQ
RET_S_BT = 8
ATT_S_BT = 8
KV_ROWS = N_MEM * 2 * CA_HEADS
NEG_INF = float("-inf")
CONV_S_BT = 32
STAGE_BYTES = 512 * 1024
STAGE_SLOTS = 8
VMEM_LIMIT = 60 * 1024 * 1024


def _dot(a, b):
    return jnp.dot(a, b, preferred_element_type=F32)


def _dot_nt(a, b):
    return lax.dot_general(a, b, (((1,), (1,)), ((), ())), preferred_element_type=F32)


def _rms(x, g):
    return x * lax.rsqrt(jnp.mean(x * x, axis=-1, keepdims=True) + EPS) * g


def _silu(x):
    return x * jax.nn.sigmoid(x)


def _ffn(x, g_ref, w1_ref, w3_ref, w2_ref):
    xn = _rms(x, g_ref[...]).astype(BF16)
    acc = None
    for c in range(D_FF // TF):
        sl = slice(c * TF, (c + 1) * TF)
        h1 = _dot(xn, w1_ref[:, sl])
        h3 = _dot(xn, w3_ref[:, sl])
        a = (_silu(h1) * h3).astype(BF16)
        part = _dot(a, w2_ref[sl, :])
        acc = part if acc is None else acc + part
    return x + 0.5 * acc


def _softmax_rows(s):
    m = jnp.max(s, axis=-1, keepdims=True)
    e = jnp.exp(s - m)
    return e * (1.0 / jnp.sum(e, axis=-1, keepdims=True))


def _group_norm_gate(o, gate, gn):
    mu = jnp.mean(o, axis=-1, keepdims=True)
    d = o - mu
    var = jnp.mean(d * d, axis=-1, keepdims=True)
    return _silu(gate) * (d * lax.rsqrt(var + GN_EPS) * gn)


def _rotary(t, cos2, sin2):
    return t * cos2 + pltpu.roll(t, RET_DK // 2, 1) * sin2


def _ffn_inproj_kernel(x_ref, g1_ref, w1_ref, w3_ref, w2_ref, gm_ref, win_ref,
                       x1_ref, u_ref, r_ref):
    x1 = _ffn(x_ref[...], g1_ref, w1_ref, w3_ref, w2_ref)
    x1_ref[...] = x1
    h = _rms(x1, gm_ref[...]).astype(BF16)
    proj = _dot(h, win_ref[...])
    u_ref[...] = proj[:, :CONV_CH] * jax.nn.sigmoid(proj[:, CONV_CH:2 * CONV_CH])
    r_ref[...] = proj[:, 2 * CONV_CH:]


def _resident(shape, layer):
    nd = len(shape)
    return pl.BlockSpec((None,) + tuple(shape), lambda *_: (layer,) + (0,) * nd,
                        pipeline_mode=pl.Buffered(1))


def _whole(a):
    return pl.BlockSpec(a.shape, lambda *_: (0,) * a.ndim, pipeline_mode=pl.Buffered(1))


def _stream_cast(w_hbm, layer, dst_ref):
    rows, cols = dst_ref.shape
    ch = 1 << ((STAGE_BYTES // (4 * cols)).bit_length() - 1)
    assert ch % SUBLANES == 0 and rows % ch == 0
    n = rows // ch

    ahead = min(STAGE_SLOTS - 1, n)

    def run(stage, sem):
        def chunk_copy(k):
            slot = lax.rem(k, STAGE_SLOTS)
            return pltpu.make_async_copy(w_hbm.at[layer, pl.ds(k * ch, ch), :], stage.at[slot], sem.at[slot])

        for k in range(ahead):
            chunk_copy(k).start()

        def body(k, carry):
            @pl.when(k + ahead < n)
            def _():
                chunk_copy(k + ahead).start()

            chunk_copy(k).wait()
            dst_ref[pl.ds(pl.multiple_of(k * ch, ch), ch), :] = stage[lax.rem(k, STAGE_SLOTS)].astype(BF16)
            return carry

        lax.fori_loop(0, n, body, 0)

    pl.run_scoped(run, pltpu.VMEM((STAGE_SLOTS, ch, cols), F32), pltpu.SemaphoreType.DMA((STAGE_SLOTS,)))


def _weights_call(body, layer, weights, emit=False, **kw):
    idx = sorted(weights)
    nw = len(idx)
    in_specs = list(kw.pop("in_specs"))
    for i in idx:
        in_specs[i] = pl.BlockSpec(memory_space=pl.ANY)
    scratch = list(kw.pop("scratch_shapes", ())) + [pltpu.VMEM(weights[i], BF16) for i in idx]
    out_specs, out_shape = list(kw.pop("out_specs")), list(kw.pop("out_shape"))
    n_io = len(in_specs) + len(out_specs)
    if emit:
        assert kw["grid"][0] >= 2
        out_specs += [pl.BlockSpec(memory_space=pl.ANY)] * nw
        out_shape += [jax.ShapeDtypeStruct(weights[i], BF16) for i in idx]
        scratch.append(pltpu.SemaphoreType.DMA((nw,)))

    def with_weights(*refs):
        refs = list(refs)
        sem = refs.pop() if emit else None
        slabs = refs[len(refs) - nw:]
        del refs[len(refs) - nw:]
        copies = []
        if emit:
            copies = [pltpu.make_async_copy(slab, out, sem.at[k])
                      for k, (slab, out) in enumerate(zip(slabs, refs[n_io:n_io + nw]))]
            del refs[n_io:n_io + nw]

        @pl.when(pl.program_id(0) == 0)
        def _():
            for i, slab in zip(idx, slabs):
                _stream_cast(refs[i], layer, slab)

        if emit:
            @pl.when(pl.program_id(0) == 1)
            def _():
                for c in copies:
                    c.start()

        for i, slab in zip(idx, slabs):
            refs[i] = slab
        body(*refs)

        if emit:
            @pl.when(pl.program_id(0) == pl.num_programs(0) - 1)
            def _():
                for c in copies:
                    c.wait()

    return pl.pallas_call(with_weights, in_specs=in_specs, out_specs=out_specs, out_shape=out_shape,
                          scratch_shapes=scratch, **kw)


def _layered_call(body, n_in, layer, prev, make_call=pl.pallas_call, **kw):
    idx = sorted(prev)
    if all(prev[o] is None for o in idx):
        out_specs = list(kw.pop("out_specs"))
        for o in idx:
            spec = out_specs[o]
            out_specs[o] = pl.BlockSpec(
                (DEPTH,) + tuple(spec.block_shape[1:]),
                lambda *a, index_map=spec.index_map: (0,) + tuple(index_map(*a))[1:])

        def with_fill(*refs):
            refs = list(refs)
            for o in idx:
                full = refs[n_in + o]
                for d in range(DEPTH):
                    if d != layer:
                        full[d] = jnp.zeros(full.shape[1:], full.dtype)
                refs[n_in + o] = full.at[pl.ds(layer, 1)]
            return body(*refs)

        return make_call(with_fill, out_specs=out_specs, **kw)

    kw["in_specs"] = list(kw["in_specs"]) + [pl.BlockSpec(memory_space=pl.ANY)] * len(idx)

    def with_aliased(*refs):
        return body(*refs[:n_in], *refs[n_in + len(idx):])

    call = make_call(with_aliased, input_output_aliases={n_in + k: o for k, o in enumerate(idx)}, **kw)
    return lambda *args: call(*args, *[prev[o] for o in idx])


def _ffn_inproj(x, layer, g1, gm, w, cast):
    t = x.shape[0]
    row = lambda n: pl.BlockSpec((TM, n), lambda i: (i, 0))
    kw = dict(
        grid=(t // TM,),
        out_specs=[row(D_MODEL), row(CONV_CH), row(MIX_IN - 2 * CONV_CH)],
        out_shape=[jax.ShapeDtypeStruct((t, D_MODEL), F32),
                   jax.ShapeDtypeStruct((t, CONV_CH), F32),
                   jax.ShapeDtypeStruct((t, MIX_IN - 2 * CONV_CH), F32)],
        compiler_params=pltpu.CompilerParams(vmem_limit_bytes=VMEM_LIMIT),
        name="ffn_inproj")
    vec = _resident((1, D_MODEL), layer)
    args = (x, g1, w[0], w[1], w[2], gm, w[3])
    if not cast:
        return pl.pallas_call(
            _ffn_inproj_kernel,
            in_specs=[row(D_MODEL), vec, _whole(w[0]), _whole(w[1]), _whole(w[2]), vec, _whole(w[3])], **kw)(*args)
    return _weights_call(
        _ffn_inproj_kernel, layer,
        {2: (D_MODEL, D_FF), 3: (D_MODEL, D_FF), 4: (D_FF, D_MODEL), 6: (D_MODEL, MIX_IN)}, emit=True,
        in_specs=[row(D_MODEL), vec, None, None, None, vec, None], **kw)(*args)


def _layer_norm_silu(y, lg, lb):
    mu = jnp.mean(y, axis=-1, keepdims=True)
    d = y - mu
    var = jnp.mean(d * d, axis=-1, keepdims=True)
    return _silu(d * lax.rsqrt(var + EPS) * lg + lb)


def _conv_p_kernel(u_ref, w_ref, b_ref, lg_ref, lb_ref, y_ref, nb_ref, win_ref, acc_ref):
    j = pl.program_id(1)
    for t in range(CONV_LT):
        @pl.when(j == 0)
        def _():
            win_ref[0, t, 0:CONV_HALO, :] = jnp.zeros((CONV_HALO, LANES), F32)

        @pl.when(j > 0)
        def _():
            win_ref[0, t, 0:CONV_HALO, :] = win_ref[0, t, CONV_TT:CONV_TT + CONV_HALO, :]

        win_ref[0, t, CONV_HALO:CONV_ROWS, :] = u_ref[:, t * LANES:(t + 1) * LANES]
        win_ref[0, t, CONV_ROWS:CONV_ROWS + SUBLANES, :] = jnp.zeros((SUBLANES, LANES), F32)
        for sh in range(1, SUBLANES):
            win_ref[sh, t, 0:CONV_ROWS, :] = win_ref[0, t, sh:sh + CONV_ROWS, :]

    first = CONV_HALO - (CONV_K - 1)
    n_rc = CONV_TT // CONV_RC

    def taps(idx, carry):
        t = idx // n_rc
        r0 = pl.multiple_of((idx % n_rc) * CONV_RC, CONV_RC)
        accs = [None] * (CONV_RC // SUBLANES)
        for k in range(CONV_K):
            sh = (first + k) % SUBLANES
            wk = w_ref[t, k]
            for i in range(CONV_RC // SUBLANES):
                a0 = pl.multiple_of(r0 + (first + k - sh) + i * SUBLANES, SUBLANES)
                term = wk * win_ref[sh, t, pl.ds(a0, SUBLANES), :]
                accs[i] = term if accs[i] is None else accs[i] + term
        acc_ref[t, pl.ds(r0, CONV_RC), :] = jnp.concatenate(accs, axis=0)
        return carry

    lax.fori_loop(0, CONV_LT * n_rc, taps, 0)

    def norm(c, carry):
        r0 = pl.multiple_of(c * CONV_RC, CONV_RC)
        y = jnp.concatenate([acc_ref[t, pl.ds(r0, CONV_RC), :] for t in range(CONV_LT)], axis=-1) + b_ref[...]
        y_ref[pl.ds(r0, CONV_RC), :] = _layer_norm_silu(y, lg_ref[...], lb_ref[...])
        return carry

    lax.fori_loop(0, n_rc, norm, 0, unroll=CONV_NORM_UNROLL)

    @pl.when(j == pl.num_programs(1) - 1)
    def _():
        nb_ref[0, 0] = u_ref[CONV_TT - (CONV_K - 1):CONV_TT, :]


def _conv_p(u, layer, conv_w, conv_b, ln_g, ln_b, prev_nb):
    nt = SEQ // CONV_TT
    vec = lambda: pl.BlockSpec((None, 1, CONV_CH), lambda b, j: (layer, 0, 0))
    return _layered_call(
        _conv_p_kernel, 5, layer, {1: prev_nb},
        grid=(BATCH, nt),
        in_specs=[pl.BlockSpec((CONV_TT, CONV_CH), lambda b, j: (b * nt + j, 0)),
                  pl.BlockSpec((None, CONV_LT, CONV_K, SUBLANES, LANES), lambda b, j: (layer, 0, 0, 0, 0)),
                  vec(), vec(), vec()],
        out_specs=[pl.BlockSpec((CONV_TT, CONV_CH), lambda b, j: (b * nt + j, 0)),
                   pl.BlockSpec((1, 1, CONV_K - 1, CONV_CH), lambda b, j: (layer, b, 0, 0))],
        out_shape=[jax.ShapeDtypeStruct((T_P, CONV_CH), F32),
                   jax.ShapeDtypeStruct((DEPTH, BATCH, CONV_K - 1, CONV_CH), F32)],
        scratch_shapes=[pltpu.VMEM((SUBLANES, CONV_LT, CONV_ROWS + SUBLANES, LANES), F32),
                        pltpu.VMEM((CONV_LT, CONV_TT, LANES), F32)],
        name="conv_prompt",
    )(u, conv_w, conv_b, ln_g, ln_b)


def _conv_s_kernel(st_ref, ut_ref, w_ref, b_ref, lg_ref, lb_ref, y_ref, nb_ref):
    keep = CONV_K - 1 - DEC_SEQ
    for t in range(DEC_SEQ):
        y = None
        for j in range(t, CONV_K - 1):
            term = st_ref[0, j] * w_ref[j - t:j - t + 1, :]
            y = term if y is None else y + term
        for j in range(t + 1):
            k = CONV_K - 1 - t + j
            y = y + ut_ref[j] * w_ref[k:k + 1, :]
        y_ref[t] = _layer_norm_silu(y + b_ref[...], lg_ref[...], lb_ref[...])
    for j in range(keep):
        nb_ref[0, j] = st_ref[0, j + DEC_SEQ]
    for t in range(DEC_SEQ):
        nb_ref[0, keep + t] = ut_ref[t]


def _conv_s(state_t, ut, layer, conv_w, conv_b, ln_g, ln_b, prev_nb):
    bt = CONV_S_BT
    vec = lambda: pl.BlockSpec((None, 1, CONV_CH), lambda i: (layer, 0, 0))
    return _layered_call(
        _conv_s_kernel, 6, layer, {1: prev_nb},
        grid=(DEC_BATCH // bt,),
        in_specs=[pl.BlockSpec((1, CONV_K - 1, bt, CONV_CH), lambda i: (layer, 0, i, 0)),
                  pl.BlockSpec((DEC_SEQ, bt, CONV_CH), lambda i: (0, i, 0)),
                  pl.BlockSpec((None, CONV_K, CONV_CH), lambda i: (layer, 0, 0)),
                  vec(), vec(), vec()],
        out_specs=[pl.BlockSpec((DEC_SEQ, bt, CONV_CH), lambda i: (0, i, 0)),
                   pl.BlockSpec((1, CONV_K - 1, bt, CONV_CH), lambda i: (layer, 0, i, 0))],
        out_shape=[jax.ShapeDtypeStruct((DEC_SEQ, DEC_BATCH, CONV_CH), F32),
                   jax.ShapeDtypeStruct((DEPTH, CONV_K - 1, DEC_BATCH, CONV_CH), F32)],
        name="conv_sample",
    )(state_t, ut, conv_w, conv_b, ln_g, ln_b)


def _ret_p_kernel(q_ref, k_ref, v_ref, g_ref, cos_ref, sin_ref, dec_ref, qd_ref, kd_ref,
                  cd_ref, gn_ref, o_ref, ns_ref, s_ref):
    j = pl.program_id(1)

    @pl.when(j == 0)
    def _():
        s_ref[...] = jnp.zeros_like(s_ref)

    units = [(e, h, slice(h * RET_DK, (h + 1) * RET_DK)) for h in range(RET_HEADS) for e in range(RET_PB)]
    chunks = []
    for c in range(RET_CPS):
        rows = slice(c * RET_CHUNK, (c + 1) * RET_CHUNK)
        cos2 = cos_ref[rows, :]
        sin2 = sin_ref[rows, :]
        qs = [_rotary(q_ref[e, rows, sl], cos2, sin2).astype(BF16) for e, h, sl in units]
        ks = [_rotary(k_ref[e, rows, sl], cos2, sin2) * (RET_DK ** -0.5) for e, h, sl in units]
        vs = [v_ref[e, rows, sl].astype(BF16) for e, h, sl in units]
        scores = [(_dot_nt(qb, kh.astype(BF16)) * dec_ref[h]).astype(BF16)
                  for qb, kh, (e, h, sl) in zip(qs, ks, units)]
        intra = [_dot(sc, vb) for sc, vb in zip(scores, vs)]
        upd = [lax.dot_general((kh * kd_ref[h]).astype(BF16), vb, (((0,), (0,)), ((), ())),
                               preferred_element_type=F32) for kh, vb, (e, h, sl) in zip(ks, vs, units)]
        chunks.append((rows, qs, intra, upd))

    states = [s_ref[e, h] for e, h, sl in units]
    outs = []
    for rows, qs, intra, upd in chunks:
        outs.append([a + _dot(qb, s_h.astype(BF16)) * qd_ref[h]
                     for a, qb, s_h, (e, h, sl) in zip(intra, qs, states, units)])
        states = [s_h * cd_ref[h] + u for s_h, u, (e, h, sl) in zip(states, upd, units)]

    for (rows, _, _, _), out in zip(chunks, outs):
        for o, (e, h, sl) in zip(out, units):
            o_ref[e, rows, sl] = _group_norm_gate(o, g_ref[e, rows, sl], gn_ref[:, sl])
    for s_h, (e, h, sl) in zip(states, units):
        s_ref[e, h] = s_h

    @pl.when(j == pl.num_programs(1) - 1)
    def _():
        ns_ref[0] = s_ref[...]


def _ret_p(r, layer, tabs, gn, prev_ns):
    cos2, sin2, dec, qd, kd, cd = tabs
    rows = RET_CPS * RET_CHUNK
    r3 = r.reshape(BATCH, SEQ, r.shape[-1])
    col = lambda c: pl.BlockSpec((RET_PB, rows, RET_WIDTH), lambda b, j: (b, j, c))
    const = lambda: pl.BlockSpec((RET_HEADS, RET_CHUNK, RET_DK), lambda b, j: (0, 0, 0))
    tab = lambda: pl.BlockSpec((rows, RET_DK), lambda b, j: (j, 0))
    out, ns = _layered_call(
        _ret_p_kernel, 11, layer, {1: prev_ns},
        grid=(BATCH // RET_PB, SEQ // rows),
        in_specs=[col(0), col(1), col(2), col(3), tab(), tab(), const(), const(), const(), const(),
                  pl.BlockSpec((None, 1, RET_WIDTH), lambda b, j: (layer, 0, 0))],
        out_specs=[pl.BlockSpec((RET_PB, rows, RET_WIDTH), lambda b, j: (b, j, 0)),
                   pl.BlockSpec((1, RET_PB, RET_HEADS, RET_DK, RET_DV), lambda b, j: (layer, b, 0, 0, 0))],
        out_shape=[jax.ShapeDtypeStruct((BATCH, SEQ, RET_WIDTH), F32),
                   jax.ShapeDtypeStruct((DEPTH, BATCH, RET_HEADS, RET_DK, RET_DV), F32)],
        scratch_shapes=[pltpu.VMEM((RET_PB, RET_HEADS, RET_DK, RET_DV), F32)],
        name="retention_prompt",
    )(r3, r3, r3, r3, cos2, sin2, dec, qd, kd, cd, gn)
    return out.reshape(T_P, RET_WIDTH), ns


def _ret_s_kernel(q_ref, k_ref, v_ref, g_ref, st_ref, cos_ref, sin_ref, dec_ref, qd_ref, kd_ref,
                  cd_ref, gn_ref, o_ref, ns_ref):
    cos2 = cos_ref[...]
    sin2 = sin_ref[...]
    zpad = jnp.zeros((RET_DK - SUBLANES, RET_DK), F32)
    second = lax.broadcasted_iota(jnp.int32, (SUBLANES, RET_DK), 0) >= DEC_SEQ

    def pick(a, jj):
        return jnp.where(second, a[DEC_SEQ + jj:DEC_SEQ + jj + 1, :], a[jj:jj + 1, :])

    def body(p, carry):
        rows = pl.ds(pl.multiple_of(p * SUBLANES, SUBLANES), SUBLANES)
        for h in range(RET_HEADS):
            sl = slice(h * RET_DK, (h + 1) * RET_DK)
            qh = _rotary(q_ref[rows, sl], cos2, sin2)
            kh = _rotary(k_ref[rows, sl], cos2, sin2) * (RET_DK ** -0.5)
            vh = v_ref[rows, sl]
            qb = qh.astype(BF16)
            kdec = kh * kd_ref[h]
            vpad = jnp.concatenate([vh, zpad], axis=0).astype(BF16)
            from_state = []
            for x in range(PAIR):
                s_x = st_ref[0, p * PAIR + x, h]
                from_state.append(_dot(qb, s_x.astype(BF16)))
                mine = second if x else jnp.logical_not(second)
                kx = jnp.concatenate([jnp.where(mine, kdec, 0.0), zpad], axis=0)
                upd = _dot(kx.T.astype(BF16), vpad)
                ns_ref[0, p * PAIR + x, h] = s_x * cd_ref[h] + upd
            o = jnp.where(second, from_state[1], from_state[0]) * qd_ref[h]
            for jj in range(DEC_SEQ):
                sj = jnp.sum(qh * pick(kh, jj), axis=-1, keepdims=True)
                o = o + (sj * dec_ref[h, jj]) * pick(vh, jj)
            o_ref[rows, sl] = _group_norm_gate(o, g_ref[rows, sl], gn_ref[:, sl])
        return carry

    lax.fori_loop(0, RET_S_BT // PAIR, body, 0, unroll=2)


def _ret_s(r, state_ret, layer, tabs, gn, prev_ns):
    cos2, sin2, dec, qd, kd, cd = tabs
    bt = RET_S_BT
    col = lambda c: pl.BlockSpec((bt * DEC_SEQ, RET_WIDTH), lambda i: (i, c))
    full = lambda a: pl.BlockSpec(a.shape, lambda i: (0,) * a.ndim)
    st = lambda: pl.BlockSpec((1, bt, RET_HEADS, RET_DK, RET_DV), lambda i: (layer, i, 0, 0, 0))
    return _layered_call(
        _ret_s_kernel, 12, layer, {1: prev_ns},
        grid=(DEC_BATCH // bt,),
        in_specs=[col(0), col(1), col(2), col(3), st(), full(cos2), full(sin2), full(dec), full(qd),
                  full(kd), full(cd), pl.BlockSpec((None, 1, RET_WIDTH), lambda i: (layer, 0, 0))],
        out_specs=[pl.BlockSpec((bt * DEC_SEQ, RET_WIDTH), lambda i: (i, 0)), st()],
        out_shape=[jax.ShapeDtypeStruct((T_S, RET_WIDTH), F32),
                   jax.ShapeDtypeStruct((DEPTH, DEC_BATCH, RET_HEADS, RET_DK, RET_DV), F32)],
        name="retention_sample",
    )(r, r, r, r, state_ret, cos2, sin2, dec, qd, kd, cd, gn)


def _memkv_kernel(m_ref, g_ref, wk_ref, wv_ref, kf_ref, vf_ref, kb_ref, vb_ref):
    mn = _rms(m_ref[...], g_ref[...]).astype(BF16)
    for w_ref, flat_ref, b_ref in ((wk_ref, kf_ref, kb_ref), (wv_ref, vf_ref, vb_ref)):
        y = _dot(mn, w_ref[...])
        b_ref[...] = y.astype(BF16)
        for h in range(CA_HEADS):
            for half in range(2):
                c0 = (h * 2 + half) * LANES
                flat_ref[0, pl.ds(half * CA_HEADS + h, N_MEM, stride=2 * CA_HEADS), :] = y[:, c0:c0 + LANES]


def _memkv(mem, layer, g, wk, wv, prev):
    t = mem.shape[0]
    row = lambda: pl.BlockSpec((N_MEM, D_MODEL), lambda i: (i, 0))
    flat = lambda: pl.BlockSpec((1, KV_ROWS, LANES), lambda i: (layer, i, 0))
    flat_shape = jax.ShapeDtypeStruct((DEPTH, BATCH * KV_ROWS, LANES), F32)
    square = (D_MODEL, D_MODEL)
    return _layered_call(
        _memkv_kernel, 4, layer, {0: prev[0], 1: prev[1]},
        make_call=functools.partial(_weights_call, layer=layer, weights={2: square, 3: square}),
        grid=(t // N_MEM,),
        in_specs=[row(), _resident((1, D_MODEL), layer), None, None],
        out_specs=[flat(), flat(), row(), row()],
        out_shape=[flat_shape, flat_shape, jax.ShapeDtypeStruct((t, D_MODEL), BF16),
                   jax.ShapeDtypeStruct((t, D_MODEL), BF16)],
        name="memory_kv",
    )(mem, g, wk, wv)


def _mix_out_q(x1, conv, ret, wo_ref, gca_ref, wq_ref):
    mix = jnp.concatenate([conv, ret], axis=-1).astype(BF16)
    x2 = x1 + _dot(mix, wo_ref[...])
    q = _dot(_rms(x2, gca_ref[...]).astype(BF16), wq_ref[...])
    return x2, q


def _finish(x2, o, wco_ref, g2_ref, w1_ref, w3_ref, w2_ref, gf_ref, final):
    x3 = x2 + _dot(o, wco_ref[...])
    x4 = _ffn(x3, g2_ref, w1_ref, w3_ref, w2_ref)
    return _rms(x4, gf_ref[...]) if final else x4


def _post_p_kernel(x1_ref, conv_ref, ret_ref, mk_ref, mv_ref, wo_ref, gca_ref, wq_ref, wco_ref,
                   g2_ref, w1_ref, w3_ref, w2_ref, gf_ref, y_ref, *, final):
    x2, q = _mix_out_q(x1_ref[...], conv_ref[...], ret_ref[...], wo_ref, gca_ref, wq_ref)
    cols = [slice(h * CA_HEAD_DIM, (h + 1) * CA_HEAD_DIM) for h in range(CA_HEADS)]
    qb = q.astype(BF16)
    scores = [_dot_nt(qb[:, sl], mk_ref[:, sl]) * (CA_HEAD_DIM ** -0.5) for sl in cols]
    probs = [_softmax_rows(s).astype(BF16) for s in scores]
    heads = [_dot(p, mv_ref[:, sl]) for p, sl in zip(probs, cols)]
    o = jnp.concatenate(heads, axis=-1).astype(BF16)
    y_ref[...] = _finish(x2, o, wco_ref, g2_ref, w1_ref, w3_ref, w2_ref, gf_ref, final)


def _post_p(x1, conv, ret, mk, mv, layer, wo, gca, wq, wco, g2, w1, w3, w2, gf, final):
    row = lambda n: pl.BlockSpec((TM, n), lambda i: (i, 0))
    mem = lambda: pl.BlockSpec((N_MEM, D_MODEL), lambda i: (i // (SEQ // TM), 0))
    square = (D_MODEL, D_MODEL)
    return _weights_call(
        functools.partial(_post_p_kernel, final=final), layer,
        {5: square, 7: square, 8: square, 10: (D_MODEL, D_FF), 11: (D_MODEL, D_FF), 12: (D_FF, D_MODEL)},
        emit=True,
        grid=(T_P // TM,),
        in_specs=[row(D_MODEL), row(CONV_CH), row(RET_WIDTH), mem(), mem(),
                  None, _resident((1, D_MODEL), layer), None, None, _resident((1, D_MODEL), layer),
                  None, None, None,
                  pl.BlockSpec((1, D_MODEL), lambda i: (0, 0))],
        out_specs=[row(D_MODEL)],
        out_shape=[jax.ShapeDtypeStruct((T_P, D_MODEL), F32)],
        compiler_params=pltpu.CompilerParams(vmem_limit_bytes=VMEM_LIMIT),
        name="post_prompt",
    )(x1, conv, ret, mk, mv, wo, gca, wq, wco, g2, w1, w3, w2, gf)


def _post_b_s_kernel(x2_ref, o_ref, wco_ref, g2_ref, w1_ref, w3_ref, w2_ref, gf_ref, y_ref, *, final):
    y_ref[...] = _finish(x2_ref[...], o_ref[...].astype(BF16), wco_ref, g2_ref, w1_ref, w3_ref, w2_ref,
                         gf_ref, final)


def _post_b_s(x2, o, layer, wco, g2, w1, w3, w2, gf, final):
    row = lambda: pl.BlockSpec((TM, D_MODEL), lambda i: (i, 0))
    return pl.pallas_call(
        functools.partial(_post_b_s_kernel, final=final),
        grid=(T_S // TM,),
        in_specs=[row(), row(), _whole(wco), _resident((1, D_MODEL), layer), _whole(w1), _whole(w3), _whole(w2),
                  pl.BlockSpec((1, D_MODEL), lambda i: (0, 0))],
        out_specs=row(),
        out_shape=jax.ShapeDtypeStruct((T_S, D_MODEL), F32),
        compiler_params=pltpu.CompilerParams(vmem_limit_bytes=VMEM_LIMIT),
        name="attn_out_ffn_sample",
    )(x2, o, wco, g2, w1, w3, w2, gf)


def _attn_s_kernel(x1_ref, conv_ref, ret_ref, wo_ref, gca_ref, wq_ref, k_ref, v_ref, x2_ref, o_ref):
    nh = CA_HEADS * SUBLANES
    lane = lax.broadcasted_iota(jnp.int32, (nh, KV_ROWS), 1)
    row = lax.broadcasted_iota(jnp.int32, (nh, KV_ROWS), 0)
    valid = (lane % (2 * CA_HEADS)) == (row // SUBLANES)
    second = lax.broadcasted_iota(jnp.int32, (2 * nh, LANES), 0) % SUBLANES >= DEC_SEQ
    batches = range(ATT_S_BT)
    x2, q_all = _mix_out_q(x1_ref[...], conv_ref[...], ret_ref[...], wo_ref, gca_ref, wq_ref)
    qts = []
    for pr in range(ATT_S_BT // PAIR):
        q = q_all[pr * SUBLANES:(pr + 1) * SUBLANES, :]
        blocks = [q[:, (h * 2 + half) * LANES:(h * 2 + half + 1) * LANES]
                  for half in range(2) for h in range(CA_HEADS)]
        qts.append(jnp.concatenate(blocks, axis=0).astype(BF16))
    sts = [_dot_nt(qts[b // PAIR], k_ref[0, b].astype(BF16)) for b in batches]
    ws = []
    for st in sts:
        s = (st[:nh] + pltpu.roll(st[nh:], KV_ROWS - CA_HEADS, 1)) * (CA_HEAD_DIM ** -0.5)
        p = _softmax_rows(jnp.where(valid, s, NEG_INF))
        ws.append(jnp.concatenate([p, pltpu.roll(p, CA_HEADS, 1)], axis=0).astype(BF16))
    outs = [_dot(ws[b], v_ref[0, b].astype(BF16)) for b in batches]
    x2_ref[...] = x2
    for pr in range(ATT_S_BT // PAIR):
        o = jnp.where(second, outs[pr * PAIR + 1], outs[pr * PAIR])
        for half in range(2):
            for h in range(CA_HEADS):
                r0 = (half * CA_HEADS + h) * SUBLANES
                c0 = (h * 2 + half) * LANES
                o_ref[pr * SUBLANES:(pr + 1) * SUBLANES, c0:c0 + LANES] = o[r0:r0 + SUBLANES]


def _flat_cache(c):
    c = c.reshape(DEPTH, DEC_BATCH, N_MEM, CA_HEADS, 2, LANES)
    return c.transpose(0, 1, 2, 4, 3, 5).reshape(DEPTH, DEC_BATCH, KV_ROWS, LANES)


def _unflat_cache(f):
    f = f.reshape(DEPTH, BATCH, N_MEM, 2, CA_HEADS, LANES)
    return f.transpose(0, 1, 2, 4, 3, 5).reshape(DEPTH, BATCH, N_MEM, CA_HEADS, CA_HEAD_DIM)


def _attn_s(x1, conv, ret, cache_k, cache_v, layer, wo, gca, wq):
    bt = ATT_S_BT
    rows = lambda n: pl.BlockSpec((bt * DEC_SEQ, n), lambda i: (i, 0))
    kv = lambda: pl.BlockSpec((1, bt, KV_ROWS, LANES), lambda i: (layer, i, 0, 0))
    return pl.pallas_call(
        _attn_s_kernel,
        grid=(DEC_BATCH // bt,),
        in_specs=[rows(D_MODEL), rows(CONV_CH), rows(RET_WIDTH), _whole(wo), _resident((1, D_MODEL), layer),
                  _whole(wq), kv(), kv()],
        out_specs=[rows(D_MODEL), rows(D_MODEL)],
        out_shape=[jax.ShapeDtypeStruct((T_S, D_MODEL), F32)] * 2,
        compiler_params=pltpu.CompilerParams(vmem_limit_bytes=VMEM_LIMIT),
        name="cross_attn_sample",
    )(x1, conv, ret, wo, gca, wq, cache_k, cache_v)


def _rope_tables(pos):
    inv_freq = ROPE_BASE ** (-jnp.arange(0, RET_DK, 2, dtype=F32) / RET_DK)
    ang = pos[:, None] * inv_freq[None, :]
    cos, sin = jnp.cos(ang), jnp.sin(ang)
    return jnp.concatenate([cos, cos], axis=-1), jnp.concatenate([-sin, sin], axis=-1)


def _decay_tables(c):
    log_gamma = jnp.log1p(-jnp.exp2(-5.0 - jnp.arange(RET_HEADS, dtype=F32)))
    idx = jnp.arange(c, dtype=F32)
    rel = idx[:, None] - idx[None, :]
    decay = jnp.where(rel[None] >= 0,
                      jnp.exp(log_gamma[:, None, None] * jnp.maximum(rel, 0.0)[None]), 0.0)
    q_dec = jnp.exp(log_gamma[:, None] * (idx[None, :] + 1.0))
    k_dec = jnp.exp(log_gamma[:, None] * (c - 1.0 - idx[None, :]))
    chunk_dec = jnp.exp(log_gamma * c)
    return decay, q_dec, k_dec, chunk_dec


def _prompt_tables():
    cos2, sin2 = _rope_tables(jnp.arange(SEQ, dtype=F32))
    decay, q_dec, k_dec, chunk_dec = _decay_tables(RET_CHUNK)
    lanes = (RET_HEADS, RET_CHUNK, RET_DK)
    return (cos2, sin2, decay,
            jnp.broadcast_to(q_dec[:, :, None], lanes),
            jnp.broadcast_to(k_dec[:, :, None], lanes),
            jnp.broadcast_to(chunk_dec[:, None, None], lanes))


def _sample_tables():
    slab = lambda a: jnp.concatenate([a] * PAIR, axis=-2)
    cos2, sin2 = _rope_tables(PAST_LEN + jnp.arange(DEC_SEQ, dtype=F32))
    decay, q_dec, k_dec, chunk_dec = _decay_tables(DEC_SEQ)
    rows = (RET_HEADS, DEC_SEQ, RET_DK)
    dec = jnp.broadcast_to(jnp.swapaxes(decay, 1, 2)[:, :, :, None], (RET_HEADS, DEC_SEQ, DEC_SEQ, RET_DK))
    return (slab(cos2), slab(sin2), slab(dec),
            slab(jnp.broadcast_to(q_dec[:, :, None], rows)),
            slab(jnp.broadcast_to(k_dec[:, :, None], rows)),
            jnp.broadcast_to(chunk_dec[:, None, None], (RET_HEADS, RET_DK, RET_DV)))


def kernel(x_prompt, x_sample, state_conv, state_ret, cache_mem_k, cache_mem_v, mem_prompt, g_ffn1, w1_ffn1, w3_ffn1, w2_ffn1, g_mix, w_in, conv_w, conv_b, conv_ln_g, conv_ln_b, ret_gn_g, w_out, g_ca, g_mem, w_cq, w_ck, w_cv, w_co, g_ffn2, w1_ffn2, w3_ffn2, w2_ffn2, g_final):
    vec = lambda g: g.reshape(DEPTH, 1, -1)
    w1a, w3a, w2a, w1b, w3b, w2b = w1_ffn1, w3_ffn1, w2_ffn1, w1_ffn2, w3_ffn2, w2_ffn2
    win, wout, wcq, wck, wcv, wco = w_in, w_out, w_cq, w_ck, w_cv, w_co
    g1, gm, gca, gmem, g2 = map(vec, (g_ffn1, g_mix, g_ca, g_mem, g_ffn2))
    cb, clg, clb, gn = map(vec, (conv_b, conv_ln_g, conv_ln_b, ret_gn_g))
    gf = g_final.reshape(1, D_MODEL)
    cache_k = _flat_cache(cache_mem_k)
    cache_v = _flat_cache(cache_mem_v)
    mem = mem_prompt.reshape(BATCH * N_MEM, D_MODEL)
    tabs_p = _prompt_tables()
    tabs_s = _sample_tables()
    cw8 = conv_w.reshape(DEPTH, CONV_K, CONV_LT, LANES).transpose(0, 2, 1, 3)
    cw8 = jnp.broadcast_to(cw8[:, :, :, None, :], (DEPTH, CONV_LT, CONV_K, SUBLANES, LANES))

    state_t = jnp.swapaxes(state_conv, 1, 2)

    xp = x_prompt.reshape(T_P, D_MODEL)
    xs = x_sample.reshape(T_S, D_MODEL)
    conv_p = ret_p = conv_s = ret_s = None
    mem_p = (None, None)
    for l in range(DEPTH):
        final = l == DEPTH - 1
        memk_f, memv_f, mk, mv = _memkv(mem, l, gmem, wck, wcv, mem_p)
        mem_p = (memk_f, memv_f)
        x1, u, r, *mix_w = _ffn_inproj(xp, l, g1, gm, (w1a, w3a, w2a, win), cast=True)
        cv, conv_p = _conv_p(u, l, cw8, cb, clg, clb, conv_p)
        rt, ret_p = _ret_p(r, l, tabs_p, gn, ret_p)
        xp, wout_b, wcq_b, wco_b, w1_b, w3_b, w2_b = _post_p(
            x1, cv, rt, mk, mv, l, wout, gca, wcq, wco, g2, w1b, w3b, w2b, gf, final)

        x1, u, r = _ffn_inproj(xs, l, g1, gm, mix_w, cast=False)
        ut = jnp.swapaxes(u.reshape(DEC_BATCH, DEC_SEQ, CONV_CH), 0, 1)
        cv, conv_s = _conv_s(state_t, ut, l, conv_w, cb, clg, clb, conv_s)
        cv = jnp.swapaxes(cv, 0, 1).reshape(T_S, CONV_CH)
        rt, ret_s = _ret_s(r, state_ret, l, tabs_s, gn, ret_s)
        x2, o = _attn_s(x1, cv, rt, cache_k, cache_v, l, wout_b, gca, wcq_b)
        xs = _post_b_s(x2, o, l, wco_b, g2, w1_b, w3_b, w2_b, gf, final)

    return (xp.reshape(BATCH, SEQ, D_MODEL), xs.reshape(DEC_BATCH, DEC_SEQ, D_MODEL),
            conv_p, ret_p, _unflat_cache(mem_p[0]), _unflat_cache(mem_p[1]),
            jnp.swapaxes(conv_s, 1, 2), ret_s)
```
